```python
import math
import jax, jax.numpy as jnp
from jax import lax
import numpy as np

D_MODEL = 4096
BATCH = 1
SEQ = 8192
DEPTH = 4

CHUNK = 64
N_MEM = 256
D_FF = 4096
EPS = 1e-6
MAX_POS_OFFSET = 4096

GLA_HEADS = 4
GLA_DK = 256
GLA_DV = 512
GLA_RANK = 16
GLA_TAU = 16.0
GLA_QK = GLA_HEADS * GLA_DK
GLA_V = GLA_HEADS * GLA_DV
SSM_HEADS = 32
SSM_HEADDIM = 64
SSM_GROUPS = 4
SSM_STATE = 128
SSM_CONV = 4
SSM_INNER = SSM_HEADS * SSM_HEADDIM
SSM_BC = SSM_GROUPS * SSM_STATE
SSM_CONV_DIM = SSM_INNER + 2 * SSM_BC
RET_HEADS = 8
RET_DK = 128
RET_DV = 256
RET_QK = RET_HEADS * RET_DK
RET_V = RET_HEADS * RET_DV
ROPE_BASE = 10000.0
XA_HEADS = 4
XA_DIM = 256
N_BRANCH = 3
BRANCH_W = 2048
GATE_RANK = 512
IN_WIDTHS = (GLA_QK, GLA_QK, GLA_V, GLA_V, GLA_RANK,
             SSM_INNER, SSM_CONV_DIM, SSM_HEADS,
             RET_QK, RET_QK, RET_V, RET_V,
             GATE_RANK)
IN_COLS = sum(IN_WIDTHS)

kernel_name = 'hybrid_gla_ssd_retention_encoder'


def rmsnorm(x, g):
    xf = x.astype(jnp.float32)
    y = xf * lax.rsqrt(jnp.mean(xf * xf, axis=-1, keepdims=True) + EPS)
    return (y * g.astype(jnp.float32)).astype(x.dtype)


def group_rmsnorm(y, g, n_groups):
    shp = y.shape
    yg = y.reshape(*shp[:-1], n_groups, shp[-1] // n_groups)
    yg = yg * lax.rsqrt(jnp.mean(yg * yg, axis=-1, keepdims=True) + EPS)
    return yg.reshape(shp) * g.astype(jnp.float32)


def to_chunks(t):
    return t.reshape(t.shape[0], t.shape[1] // CHUNK, CHUNK, *t.shape[2:])


def from_chunks(t):
    return t.reshape(t.shape[0], t.shape[1] * t.shape[2], *t.shape[3:])


def within_chunk_mask():
    return jnp.tril(jnp.ones((CHUNK, CHUNK), dtype=bool))


def chunk_prefix_states(decay, contrib):
    d = jnp.moveaxis(decay, 1, 0)
    u = jnp.moveaxis(contrib, 1, 0)

    def step(state, du):
        dc, uc = du
        return dc * state + uc, state

    _, prev = lax.scan(step, jnp.zeros_like(u[0]), (d, u))
    return jnp.moveaxis(prev, 0, 1)


def gla_branch(q, k, v, r, gk_low, w_gk, b_gk, norm_g):
    b_, s_ = q.shape[:2]
    log_f = jax.nn.log_sigmoid(gk_low @ w_gk.astype(jnp.float32) + b_gk.astype(jnp.float32)) / GLA_TAU
    qc = to_chunks(q.reshape(b_, s_, GLA_HEADS, GLA_DK)) * (GLA_DK ** -0.5)
    kc = to_chunks(k.reshape(b_, s_, GLA_HEADS, GLA_DK))
    vc = to_chunks(v.reshape(b_, s_, GLA_HEADS, GLA_DV))
    cum = jnp.cumsum(to_chunks(log_f.reshape(b_, s_, GLA_HEADS, GLA_DK)), axis=2)
    up, down = jnp.exp(cum), jnp.exp(-cum)
    q_dec = qc * up
    a_past = jnp.einsum('bcihd,bcjhd->bchij', q_dec, kc * down)
    a_future = jnp.einsum('bcihd,bcjhd->bchij', qc * down, kc * up)
    attn = jnp.where(within_chunk_mask(), a_past, a_future)
    intra = jnp.einsum('bchij,bcjhv->bcihv', attn, vc)
    cum_end = cum[:, :, -1]
    contrib = jnp.einsum('bcjhd,bcjhv->bchdv', kc * jnp.exp(cum_end[:, :, None] - cum), vc)
    s_prev = chunk_prefix_states(jnp.exp(cum_end)[..., None], contrib)
    inter = jnp.einsum('bcihd,bchdv->bcihv', q_dec, s_prev)
    o = from_chunks(intra + inter).reshape(b_, s_, GLA_V)
    return group_rmsnorm(o, norm_g, GLA_HEADS) * jax.nn.silu(r)


def ssd_branch(z, xbc, dt_raw, conv_w, conv_b, dt_bias, a_log, d_skip, norm_g):
    b_, s_ = z.shape[:2]
    hpg = SSM_HEADS // SSM_GROUPS
    padded = jnp.pad(xbc, ((0, 0), (SSM_CONV - 1, 0), (0, 0)))
    cw = conv_w.astype(jnp.float32)
    conv = conv_b.astype(jnp.float32) + sum(padded[:, tap:tap + s_] * cw[tap] for tap in range(SSM_CONV))
    xbc = jax.nn.silu(conv)
    xs, bm, cm = jnp.split(xbc, [SSM_INNER, SSM_INNER + SSM_BC], axis=-1)
    dt = jax.nn.softplus(dt_raw + dt_bias.astype(jnp.float32))
    da = dt * -jnp.exp(a_log.astype(jnp.float32))
    xh = xs.reshape(b_, s_, SSM_GROUPS, hpg, SSM_HEADDIM)
    xdt = to_chunks(xh * dt.reshape(b_, s_, SSM_GROUPS, hpg)[..., None])
    bc = to_chunks(bm.reshape(b_, s_, SSM_GROUPS, SSM_STATE))
    cc = to_chunks(cm.reshape(b_, s_, SSM_GROUPS, SSM_STATE))
    cum = jnp.cumsum(to_chunks(da.reshape(b_, s_, SSM_GROUPS, hpg)), axis=2)
    cum_t = jnp.moveaxis(cum, 2, -1)
    seg = jnp.exp(-jnp.abs(cum_t[..., :, None] - cum_t[..., None, :]))
    cb = jnp.einsum('bcign,bcjgn->bcgij', cc, bc)
    intra = jnp.einsum('bcgeij,bcjgep->bcigep', cb[:, :, :, None] * seg, xdt)
    cum_end = cum[:, :, -1]
    contrib = jnp.einsum('bcjgn,bcjgep->bcgenp', bc,
                         xdt * jnp.exp(cum_end[:, :, None] - cum)[..., None])
    s_prev = chunk_prefix_states(jnp.exp(cum_end)[..., None, None], contrib)
    inter = jnp.einsum('bcign,bcgenp->bcigep', cc, s_prev) * jnp.exp(cum)[..., None]
    y = from_chunks(intra + inter) + xh * d_skip.astype(jnp.float32).reshape(SSM_GROUPS, hpg)[:, :, None]
    y = y.reshape(b_, s_, SSM_INNER) * jax.nn.silu(z)
    return group_rmsnorm(y, norm_g, SSM_GROUPS)


def rotary(t, positions):
    half = t.shape[-1] // 2
    inv_freq = ROPE_BASE ** (-jnp.arange(half, dtype=jnp.float32) / half)
    ang = positions.astype(jnp.float32)[..., None] * inv_freq
    cos = jnp.cos(ang)[:, :, None]
    sin = jnp.sin(ang)[:, :, None]
    t1, t2 = t[..., :half], t[..., half:]
    return jnp.concatenate([t1 * cos - t2 * sin, t1 * sin + t2 * cos], axis=-1)


def retention_branch(q, k, v, g, positions, norm_g):
    b_, s_ = q.shape[:2]
    n_chunks = s_ // CHUNK
    log_gamma = jnp.log1p(-jnp.exp2(-5.0 - jnp.arange(RET_HEADS, dtype=jnp.float32)))
    qc = to_chunks(rotary(q.reshape(b_, s_, RET_HEADS, RET_DK), positions))
    kc = to_chunks(rotary(k.reshape(b_, s_, RET_HEADS, RET_DK), positions)) * (RET_DK ** -0.5)
    vc = to_chunks(v.reshape(b_, s_, RET_HEADS, RET_DV))
    idx = jnp.arange(CHUNK, dtype=jnp.float32)
    dist = jnp.abs(idx[:, None] - idx[None, :])
    decay_mat = jnp.exp(log_gamma[:, None, None] * dist)
    scores = jnp.einsum('bcihd,bcjhd->bchij', qc, kc) * decay_mat
    intra = jnp.einsum('bchij,bcjhv->bcihv', scores, vc)
    k_decay = jnp.exp((CHUNK - 1 - idx)[:, None] * log_gamma)
    contrib = jnp.einsum('bcjhd,jh,bcjhv->bchdv', kc, k_decay, vc)
    chunk_decay = jnp.broadcast_to(jnp.exp(CHUNK * log_gamma)[:, None, None],
                                   (b_, n_chunks, RET_HEADS, 1, 1))
    s_prev = chunk_prefix_states(chunk_decay, contrib)
    q_decay = jnp.exp((idx + 1.0)[:, None] * log_gamma)
    inter = jnp.einsum('bcihd,ih,bchdv->bcihv', qc, q_decay, s_prev)
    o = from_chunks(intra + inter).reshape(b_, s_, RET_V)
    return group_rmsnorm(o, norm_g, RET_HEADS) * jax.nn.silu(g)


def hybrid_mixer(h, positions, w_in, gla_w_gk, gla_b_gk, gla_norm_g, ssm_conv_w, ssm_conv_b,
                 ssm_dt_bias, ssm_a_log, ssm_d, ssm_norm_g, ret_norm_g, gate_w_up, gate_b, w_out):
    proj = (h @ w_in).astype(jnp.float32)
    split_at = np.cumsum(IN_WIDTHS)[:-1].tolist()
    (gla_q, gla_k, gla_v, gla_r, gla_gk, ssm_z, ssm_xbc, ssm_dt,
     ret_q, ret_k, ret_v, ret_g, gate_low) = jnp.split(proj, split_at, axis=-1)
    branches = (
        gla_branch(gla_q, gla_k, gla_v, gla_r, gla_gk, gla_w_gk, gla_b_gk, gla_norm_g),
        ssd_branch(ssm_z, ssm_xbc, ssm_dt, ssm_conv_w, ssm_conv_b, ssm_dt_bias, ssm_a_log, ssm_d, ssm_norm_g),
        retention_branch(ret_q, ret_k, ret_v, ret_g, positions, ret_norm_g),
    )
    mixed = jnp.zeros(h.shape, jnp.float32)
    for n, y in enumerate(branches):
        gate = jax.nn.sigmoid(gate_low @ gate_w_up[:, n].astype(jnp.float32) + gate_b[n].astype(jnp.float32))
        mixed = mixed + gate * (y @ w_out[n].astype(jnp.float32))
    return mixed.astype(h.dtype)


def cross_attention(h, m, w_q, w_kv, w_o):
    b_, s_ = h.shape[:2]
    q = (h @ w_q).reshape(b_, s_, XA_HEADS, XA_DIM)
    k, v = jnp.split(m @ w_kv, 2, axis=-1)
    k = k.reshape(b_, m.shape[1], XA_HEADS, XA_DIM)
    v = v.reshape(b_, m.shape[1], XA_HEADS, XA_DIM)
    s = jnp.einsum('bshd,bmhd->bhsm', q, k).astype(jnp.float32) * (XA_DIM ** -0.5)
    p = jax.nn.softmax(s, axis=-1).astype(v.dtype)
    o = jnp.einsum('bhsm,bmhd->bshd', p, v).reshape(b_, s_, XA_HEADS * XA_DIM)
    return o @ w_o


def swiglu(h, w_in, w_out):
    gt, up = jnp.split(h @ w_in, 2, axis=-1)
    return (jax.nn.silu(gt) * up) @ w_out


def setup_inputs(seed: int = 0) -> dict:
    key = jax.random.key(seed)
    ks = jax.random.split(key, 27)
    f32 = jnp.float32
    L = DEPTH

    def dense(k, shape, fan_in):
        return jax.random.normal(k, shape, f32) * (fan_in ** -0.5)

    def gain(k, shape):
        return 1.0 + 0.05 * jax.random.normal(k, shape, f32)

    def small(k, shape, scale=0.01):
        return scale * jax.random.normal(k, shape, f32)

    dt0 = jnp.exp(jax.random.uniform(ks[13], (L, SSM_HEADS), f32, math.log(1e-3), math.log(1e-1)))
    positions = (jax.random.randint(ks[2], (BATCH, 1), 0, MAX_POS_OFFSET, dtype=jnp.int32)
                 + jnp.arange(SEQ, dtype=jnp.int32)[None, :])
    return {
        'x': jax.random.normal(ks[0], (BATCH, SEQ, D_MODEL), f32),
        'mem': jax.random.normal(ks[1], (BATCH, N_MEM, D_MODEL), f32),
        'positions': positions,
        'norm_g': gain(ks[3], (L, 8, D_MODEL)),
        'mem_norm_g': gain(ks[4], (L, D_MODEL)),
        'ffn1_w_in': dense(ks[5], (L, D_MODEL, 2 * D_FF), D_MODEL),
        'ffn1_w_out': dense(ks[6], (L, D_FF, D_MODEL), D_FF),
        'mix_w_in': dense(ks[7], (L, D_MODEL, IN_COLS), D_MODEL),
        'gla_w_gk': dense(ks[8], (L, GLA_RANK, GLA_QK), GLA_RANK),
        'gla_b_gk': small(ks[9], (L, GLA_QK), 0.1),
        'gla_norm_g': gain(ks[10], (L, GLA_V)),
        'ssm_conv_w': dense(ks[11], (L, SSM_CONV, SSM_CONV_DIM), SSM_CONV),
        'ssm_conv_b': small(ks[12], (L, SSM_CONV_DIM)),
        'ssm_dt_bias': dt0 + jnp.log(-jnp.expm1(-dt0)),
        'ssm_a_log': jnp.log(jax.random.uniform(ks[14], (L, SSM_HEADS), f32, 1.0, 16.0)),
        'ssm_d': gain(ks[15], (L, SSM_HEADS)),
        'ssm_norm_g': gain(ks[16], (L, SSM_INNER)),
        'ret_norm_g': gain(ks[17], (L, RET_V)),
        'gate_w_up': dense(ks[18], (L, GATE_RANK, N_BRANCH, D_MODEL), GATE_RANK),
        'gate_b': small(ks[19], (L, N_BRANCH, D_MODEL)),
        'mix_w_out': dense(ks[20], (L, N_BRANCH, BRANCH_W, D_MODEL), BRANCH_W),
        'xattn_w_q': dense(ks[21], (L, D_MODEL, XA_HEADS * XA_DIM), D_MODEL),
        'xattn_w_kv': dense(ks[22], (L, D_MODEL, 2 * XA_HEADS * XA_DIM), D_MODEL),
        'xattn_w_o': dense(ks[23], (L, XA_HEADS * XA_DIM, D_MODEL), XA_HEADS * XA_DIM),
        'ffn2_w_in': dense(ks[24], (L, D_MODEL, 2 * D_FF), D_MODEL),
        'ffn2_w_out': dense(ks[25], (L, D_FF, D_MODEL), D_FF),
    }


def reference(x, mem, positions, norm_g, mem_norm_g, ffn1_w_in, ffn1_w_out, mix_w_in,
              gla_w_gk, gla_b_gk, gla_norm_g, ssm_conv_w, ssm_conv_b, ssm_dt_bias, ssm_a_log,
              ssm_d, ssm_norm_g, ret_norm_g, gate_w_up, gate_b, mix_w_out,
              xattn_w_q, xattn_w_kv, xattn_w_o, ffn2_w_in, ffn2_w_out):
    for l in range(DEPTH):
        ng = norm_g[l]
        x = x + 0.5 * rmsnorm(swiglu(rmsnorm(x, ng[0]), ffn1_w_in[l], ffn1_w_out[l]), ng[1])
        mix = hybrid_mixer(rmsnorm(x, ng[2]), positions, mix_w_in[l], gla_w_gk[l], gla_b_gk[l],
                           gla_norm_g[l], ssm_conv_w[l], ssm_conv_b[l], ssm_dt_bias[l], ssm_a_log[l],
                           ssm_d[l], ssm_norm_g[l], ret_norm_g[l], gate_w_up[l], gate_b[l], mix_w_out[l])
        x = x + rmsnorm(mix, ng[3])
        xa = cross_attention(rmsnorm(x, ng[4]), rmsnorm(mem, mem_norm_g[l]),
                             xattn_w_q[l], xattn_w_kv[l], xattn_w_o[l])
        x = x + rmsnorm(xa, ng[5])
        x = x + 0.5 * rmsnorm(swiglu(rmsnorm(x, ng[6]), ffn2_w_in[l], ffn2_w_out[l]), ng[7])
    return x
```

```python
import functools
import math

import numpy as np
import jax
import jax.numpy as jnp
from jax import lax
from jax.experimental import pallas as pl
from jax.experimental.pallas import tpu as pltpu

F32 = jnp.float32
BF16 = jnp.bfloat16
HIGHEST = lax.Precision.HIGHEST

D_MODEL = 4096
DEPTH = 4
CHUNK = 64
D_FF = 4096
EPS = 1e-6
GLA_HEADS, GLA_DK, GLA_DV, GLA_RANK, GLA_TAU = 4, 256, 512, 16, 16.0
GLA_QK, GLA_V = GLA_HEADS * GLA_DK, GLA_HEADS * GLA_DV
SSM_HEADS, SSM_HEADDIM, SSM_GROUPS, SSM_STATE, SSM_CONV = 32, 64, 4, 128, 4
SSM_INNER = SSM_HEADS * SSM_HEADDIM
SSM_BC = SSM_GROUPS * SSM_STATE
SSM_CONV_DIM = SSM_INNER + 2 * SSM_BC
SSM_HPG = SSM_HEADS // SSM_GROUPS
SSM_GW = SSM_INNER // SSM_GROUPS
RET_HEADS, RET_DK, RET_DV = 8, 128, 256
RET_QK, RET_V = RET_HEADS * RET_DK, RET_HEADS * RET_DV
ROPE_BASE = 10000.0
XA_HEADS, XA_DIM = 4, 256
XA_W = XA_HEADS * XA_DIM
N_BRANCH, BRANCH_W, GATE_RANK = 3, 2048, 512

_IN_WIDTHS = (GLA_QK, GLA_QK, GLA_V, GLA_V, GLA_RANK, SSM_INNER, SSM_CONV_DIM, SSM_HEADS,
              RET_QK, RET_QK, RET_V, RET_V, GATE_RANK)
_IN_OFF = np.concatenate([[0], np.cumsum(_IN_WIDTHS)]).tolist()
P_GLA_Q, P_GLA_K, P_GLA_V, P_GLA_R = 0, 1024, 2048, 4096
P_SSM_Z, P_SSM_X, P_SSM_B, P_SSM_C = 6144, 8192, 10240, 10752
P_RET_Q, P_RET_K, P_RET_V, P_RET_G = 11264, 12288, 13312, 15360
P_GATE = 17408
P_COLS = 17920
SMALL_W = 128
SMALL_DT0 = GLA_RANK

V7X_VMEM_LIMIT = 60 * 1024 * 1024

RET_BLOCK = 512
MIX_ROWS = 256


def _cparams(*sem):
    return pltpu.CompilerParams(dimension_semantics=sem, vmem_limit_bytes=V7X_VMEM_LIMIT)


def _silu(x):
    return x * jax.nn.sigmoid(x)


def _dot(a, b):
    return jnp.dot(a, b, preferred_element_type=F32)


def _dot_nt(a, b):
    return lax.dot_general(a, b, (((1,), (1,)), ((), ())), preferred_element_type=F32)


def _dot_tn(a, b):
    return lax.dot_general(a, b, (((0,), (0,)), ((), ())), preferred_element_type=F32)


def _norm_cast_kernel(x_ref, g_ref, o_ref):
    x = x_ref[...]
    r = lax.rsqrt(jnp.mean(x * x, axis=-1, keepdims=True) + EPS)
    o_ref[...] = (x * r * g_ref[...]).astype(o_ref.dtype)


def _norm_cast(x, g, rows=256):
    m, d = x.shape
    rows = min(rows, m)
    return pl.pallas_call(
        _norm_cast_kernel,
        out_shape=jax.ShapeDtypeStruct((m, d), BF16),
        grid=(m // rows,),
        in_specs=[pl.BlockSpec((rows, d), lambda i: (i, 0)), pl.BlockSpec((1, d), lambda i: (0, 0))],
        out_specs=pl.BlockSpec((rows, d), lambda i: (i, 0)),
        compiler_params=_cparams("parallel"),
        name="norm_cast",
    )(x, g.reshape(1, d))


def _post_kernel(scale, emit_h, y_ref, x_ref, gp_ref, gn_ref, xo_ref, *maybe_h):
    y = y_ref[...].astype(F32)
    r = lax.rsqrt(jnp.mean(y * y, axis=-1, keepdims=True) + EPS)
    xn = x_ref[...] + scale * (y * r * gp_ref[...])
    xo_ref[...] = xn
    if emit_h:
        r2 = lax.rsqrt(jnp.mean(xn * xn, axis=-1, keepdims=True) + EPS)
        maybe_h[0][...] = (xn * r2 * gn_ref[...]).astype(BF16)


def _post(y, x, g_post, scale, g_next, rows=256):
    m, d = x.shape
    emit_h = g_next is not None
    gn = g_next if emit_h else g_post
    row_spec = pl.BlockSpec((rows, d), lambda i: (i, 0))
    vec_spec = pl.BlockSpec((1, d), lambda i: (0, 0))
    out_shape = [jax.ShapeDtypeStruct((m, d), F32)]
    out_specs = [row_spec]
    if emit_h:
        out_shape.append(jax.ShapeDtypeStruct((m, d), BF16))
        out_specs.append(row_spec)
    res = pl.pallas_call(
        functools.partial(_post_kernel, scale, emit_h),
        out_shape=out_shape,
        grid=(m // rows,),
        in_specs=[row_spec, row_spec, vec_spec, vec_spec],
        out_specs=out_specs,
        compiler_params=_cparams("parallel"),
        name="post_norm",
    )(y, x, g_post.reshape(1, d), gn.reshape(1, d))
    return (res[0], res[1]) if emit_h else (res[0], None)


def _mm_kernel(x_ref, w_ref, o_ref):
    o_ref[...] = _dot(x_ref[...], w_ref[...]).astype(o_ref.dtype)


def _matmul(x, w, out_dtype, bm, bn, name):
    m, k = x.shape
    n = w.shape[1]
    bm, bn = min(bm, m), min(bn, n)
    return pl.pallas_call(
        _mm_kernel,
        out_shape=jax.ShapeDtypeStruct((m, n), out_dtype),
        grid=(m // bm, n // bn),
        in_specs=[pl.BlockSpec((bm, k), lambda i, j: (i, 0)), pl.BlockSpec((k, bn), lambda i, j: (0, j))],
        out_specs=pl.BlockSpec((bm, bn), lambda i, j: (i, j)),
        compiler_params=_cparams("parallel", "parallel"),
        name=name,
    )(x, w)


def _swiglu_kernel(x_ref, wg_ref, wu_ref, o_ref):
    x = x_ref[...]
    g = _dot(x, wg_ref[...])
    u = _dot(x, wu_ref[...])
    o_ref[...] = (_silu(g) * u).astype(o_ref.dtype)


def _swiglu_in(x, w, bm=1024, bn=512):
    m, k = x.shape
    f = w.shape[1] // 2
    nb = f // bn
    return pl.pallas_call(
        _swiglu_kernel,
        out_shape=jax.ShapeDtypeStruct((m, f), BF16),
        grid=(m // bm, nb),
        in_specs=[pl.BlockSpec((bm, k), lambda i, j: (i, 0)),
                  pl.BlockSpec((k, bn), lambda i, j: (0, j)),
                  pl.BlockSpec((k, bn), lambda i, j: (0, j + nb))],
        out_specs=pl.BlockSpec((bm, bn), lambda i, j: (i, j)),
        compiler_params=_cparams("parallel", "parallel"),
        name="swiglu_in",
    )(x, w, w)


def _gla_kernel(q_ref, k_ref, v_ref, r_ref, sm_ref, wgk_ref, bgk_ref, ng_ref, o_ref, s_ref):
    @pl.when(pl.program_id(1) == 0)
    def _():
        s_ref[...] = jnp.zeros_like(s_ref)

    rows = q_ref.shape[0]
    ri = lax.broadcasted_iota(jnp.int32, (CHUNK, CHUNK), 0)
    ci = lax.broadcasted_iota(jnp.int32, (CHUNK, CHUNK), 1)
    tril = ri >= ci
    ltri = tril.astype(F32)
    ones_c = jnp.ones((CHUNK, 128), F32)
    for c in range(rows // CHUNK):
        sl = pl.ds(c * CHUNK, CHUNK)
        z = jnp.dot(sm_ref[sl, :], wgk_ref[...], precision=HIGHEST, preferred_element_type=F32) + bgk_ref[...]
        logf = (jnp.minimum(z, 0.0) - jnp.log1p(jnp.exp(-jnp.abs(z)))) * (1.0 / GLA_TAU)
        cum = jnp.dot(ltri, logf, precision=HIGHEST, preferred_element_type=F32)
        cum_end = cum[CHUNK - 1:CHUNK, :]
        end_col = lax.dot_general(logf, ones_c, (((0,), (0,)), ((), ())), precision=HIGHEST,
                                  preferred_element_type=F32)[:, 0:1]
        up, down = jnp.exp(cum), jnp.exp(-cum)
        q = q_ref[sl, :].astype(F32) * (GLA_DK ** -0.5)
        k = k_ref[sl, :].astype(F32)
        v = v_ref[sl, :]
        q_dec = (q * up).astype(BF16)
        a_past = _dot_nt(q_dec, (k * down).astype(BF16))
        a_future = _dot_nt((q * down).astype(BF16), (k * up).astype(BF16))
        attn = jnp.where(tril, a_past, a_future).astype(BF16)
        state = s_ref[...]
        o = _dot(attn, v) + _dot(q_dec, state.astype(BF16))
        contrib = _dot_tn((k * jnp.exp(cum_end - cum)).astype(BF16), v)
        s_ref[...] = jnp.exp(end_col) * state + contrib
        o = o * lax.rsqrt(jnp.mean(o * o, axis=-1, keepdims=True) + EPS) * ng_ref[...]
        o_ref[sl, :] = (o * _silu(r_ref[sl, :].astype(F32))).astype(o_ref.dtype)


def _gla(proj, small, w_gk_pad, b_gk, norm_g):
    s = proj.shape[0]
    rows = min(MIX_ROWS, s)
    qb, vb = P_GLA_Q // GLA_DK, P_GLA_V // GLA_DV
    kb, rb = P_GLA_K // GLA_DK, P_GLA_R // GLA_DV
    return pl.pallas_call(
        _gla_kernel,
        out_shape=jax.ShapeDtypeStruct((s, GLA_V), BF16),
        grid=(GLA_HEADS, s // rows),
        in_specs=[pl.BlockSpec((rows, GLA_DK), lambda h, i: (i, qb + h)),
                  pl.BlockSpec((rows, GLA_DK), lambda h, i: (i, kb + h)),
                  pl.BlockSpec((rows, GLA_DV), lambda h, i: (i, vb + h)),
                  pl.BlockSpec((rows, GLA_DV), lambda h, i: (i, rb + h)),
                  pl.BlockSpec((rows, SMALL_W), lambda h, i: (i, 0)),
                  pl.BlockSpec((SMALL_W, GLA_DK), lambda h, i: (0, h)),
                  pl.BlockSpec((1, GLA_DK), lambda h, i: (0, h)),
                  pl.BlockSpec((1, GLA_DV), lambda h, i: (0, h))],
        out_specs=pl.BlockSpec((rows, GLA_DV), lambda h, i: (i, h)),
        scratch_shapes=[pltpu.VMEM((GLA_DK, GLA_DV), F32)],
        compiler_params=_cparams("parallel", "arbitrary"),
        name="gla",
    )(proj, proj, proj, proj, small, w_gk_pad, b_gk.reshape(1, GLA_QK), norm_g.reshape(1, GLA_V))


def _causal_conv_silu(xp_ref, x_ref, w_ref, b_ref, first):
    rows = x_ref.shape[0]

    @pl.when(first)
    def _():
        xp_ref[pl.ds(0, 8), :] = jnp.zeros((8, xp_ref.shape[1]), F32)

    xp_ref[pl.ds(8, rows), :] = x_ref[...].astype(F32)
    acc = b_ref[...]
    for tap in range(SSM_CONV):
        acc = acc + xp_ref[pl.ds(8 - (SSM_CONV - 1) + tap, rows), :] * w_ref[tap:tap + 1, :]
    xp_ref[pl.ds(0, 8), :] = xp_ref[pl.ds(rows, 8), :]
    return _silu(acc)


def _ssd_kernel(z_ref, xs_ref, b_ref, c_ref, sm_ref, e_ref, cwx_ref, cwb_ref, cwc_ref,
                cbx_ref, cbb_ref, cbc_ref, dtb_ref, alog_ref, dsk_ref, ng_ref,
                o_ref, s_ref, xpx_ref, xpb_ref, xpc_ref):
    first = pl.program_id(1) == 0

    @pl.when(first)
    def _():
        s_ref[...] = jnp.zeros_like(s_ref)

    rows = xs_ref.shape[0]
    xs_all = _causal_conv_silu(xpx_ref, xs_ref, cwx_ref, cbx_ref, first)
    b_all = _causal_conv_silu(xpb_ref, b_ref, cwb_ref, cbb_ref, first)
    c_all = _causal_conv_silu(xpc_ref, c_ref, cwc_ref, cbc_ref, first)

    gw = SSM_GW
    ri = lax.broadcasted_iota(jnp.int32, (CHUNK, CHUNK), 0)
    ci = lax.broadcasted_iota(jnp.int32, (CHUNK, CHUNK), 1)
    ltri = (ri >= ci).astype(F32)
    rw = lax.broadcasted_iota(jnp.int32, (CHUNK, gw), 0)
    cw = lax.broadcasted_iota(jnp.int32, (CHUNK, gw), 1)
    diag_tile = (jnp.bitwise_and(cw, SSM_HEADDIM - 1) == rw).astype(F32)
    half = gw // 2
    rb = lax.broadcasted_iota(jnp.int32, (half, half), 0)
    cb_ = lax.broadcasted_iota(jnp.int32, (half, half), 1)
    head_mask = jnp.right_shift(rb, 6) == jnp.right_shift(cb_, 6)
    neg_a = -jnp.exp(alog_ref[...])
    reps = half // CHUNK
    for c in range(rows // CHUNK):
        sl = slice(c * CHUNK, (c + 1) * CHUNK)
        dt_raw = jnp.dot(sm_ref[pl.ds(c * CHUNK, CHUNK), :], e_ref[0], precision=HIGHEST,
                         preferred_element_type=F32)
        xdt_in = dt_raw + dtb_ref[...]
        dt = jnp.maximum(xdt_in, 0.0) + jnp.log1p(jnp.exp(-jnp.abs(xdt_in)))
        da = dt * neg_a
        cum = jnp.dot(ltri, da, precision=HIGHEST, preferred_element_type=F32)
        cum_row = jnp.sum(cum * diag_tile, axis=0, keepdims=True)
        seg = jnp.exp(-jnp.abs(cum - cum_row))
        xs = xs_all[sl]
        bm = b_all[sl].astype(BF16)
        cm = c_all[sl].astype(BF16)
        xdt = xs * dt
        cbt = _dot_nt(cm, jnp.concatenate([bm] * SSM_HPG, axis=0))
        m_all = (cbt * seg).astype(BF16)
        intra = []
        for hlf in range(2):
            xh = xdt[:, hlf * half:(hlf + 1) * half]
            bd = jnp.where(head_mask, jnp.concatenate([xh] * reps, axis=0), 0.0).astype(BF16)
            intra.append(_dot(m_all[:, hlf * half:(hlf + 1) * half], bd))
        intra = jnp.concatenate(intra, axis=1)
        cum_end = cum[CHUNK - 1:CHUNK, :]
        state = s_ref[...]
        inter = _dot(cm, state.astype(BF16)) * jnp.exp(cum)
        contrib = _dot_tn(bm, (xdt * jnp.exp(cum_end - cum)).astype(BF16))
        s_ref[...] = state * jnp.exp(cum_end) + contrib
        y = intra + inter + xs * dsk_ref[...]
        y = y * _silu(z_ref[pl.ds(c * CHUNK, CHUNK), :].astype(F32))
        y = y * lax.rsqrt(jnp.mean(y * y, axis=-1, keepdims=True) + EPS) * ng_ref[...]
        o_ref[pl.ds(c * CHUNK, CHUNK), :] = y.astype(o_ref.dtype)


def _ssd(proj, small, expand, conv_w, conv_b, dt_bias, a_log, d_skip, norm_g):
    s = proj.shape[0]
    rows = min(MIX_ROWS, s)
    gw, st = SSM_GW, SSM_STATE
    zb, xb = P_SSM_Z // gw, P_SSM_X // gw
    bb, cb = P_SSM_B // st, P_SSM_C // st
    cwb, cwc = SSM_INNER // st, (SSM_INNER + SSM_BC) // st

    def lane_expand(p):
        return jnp.repeat(p.astype(F32), SSM_HEADDIM).reshape(1, SSM_INNER)

    conv_b2 = conv_b.reshape(1, SSM_CONV_DIM)
    vec = lambda blk: pl.BlockSpec((1, gw), lambda g, i: (0, g))
    return pl.pallas_call(
        _ssd_kernel,
        out_shape=jax.ShapeDtypeStruct((s, SSM_INNER), BF16),
        grid=(SSM_GROUPS, s // rows),
        in_specs=[pl.BlockSpec((rows, gw), lambda g, i: (i, zb + g)),
                  pl.BlockSpec((rows, gw), lambda g, i: (i, xb + g)),
                  pl.BlockSpec((rows, st), lambda g, i: (i, bb + g)),
                  pl.BlockSpec((rows, st), lambda g, i: (i, cb + g)),
                  pl.BlockSpec((rows, SMALL_W), lambda g, i: (i, 0)),
                  pl.BlockSpec((1, SMALL_W, gw), lambda g, i: (g, 0, 0)),
                  pl.BlockSpec((SSM_CONV, gw), lambda g, i: (0, g)),
                  pl.BlockSpec((SSM_CONV, st), lambda g, i: (0, cwb + g)),
                  pl.BlockSpec((SSM_CONV, st), lambda g, i: (0, cwc + g)),
                  pl.BlockSpec((1, gw), lambda g, i: (0, g)),
                  pl.BlockSpec((1, st), lambda g, i: (0, cwb + g)),
                  pl.BlockSpec((1, st), lambda g, i: (0, cwc + g)),
                  vec(0), vec(0), vec(0), vec(0)],
        out_specs=pl.BlockSpec((rows, gw), lambda g, i: (i, g)),
        scratch_shapes=[pltpu.VMEM((st, gw), F32),
                        pltpu.VMEM((rows + 8, gw), F32),
                        pltpu.VMEM((rows + 8, st), F32),
                        pltpu.VMEM((rows + 8, st), F32)],
        compiler_params=_cparams("parallel", "arbitrary"),
        name="ssd",
    )(proj, proj, proj, proj, small, expand, conv_w, conv_w, conv_w, conv_b2, conv_b2, conv_b2,
      lane_expand(dt_bias), lane_expand(a_log), lane_expand(d_skip), norm_g.reshape(1, SSM_INNER))


def _ssd_expand_table():
    rows = jnp.arange(SMALL_W, dtype=jnp.int32)[None, :, None]
    cols = jnp.arange(SSM_GW, dtype=jnp.int32)[None, None, :]
    grp = jnp.arange(SSM_GROUPS, dtype=jnp.int32)[:, None, None]
    return (rows == SMALL_DT0 + grp * SSM_HPG + cols // SSM_HEADDIM).astype(F32)


def _rope_table_kernel(pos_ref, freq_ref, sign_ref, cos_ref, sin_ref):
    ang = pos_ref[...] * freq_ref[...]
    cos_ref[...] = jnp.cos(ang)
    sin_ref[...] = jnp.sin(ang) * sign_ref[...]


def _rope_tables(positions):
    s = positions.shape[-1]
    half = RET_DK // 2
    inv_freq = ROPE_BASE ** (-jnp.arange(half, dtype=F32) / half)
    freq = jnp.concatenate([inv_freq, inv_freq]).reshape(1, RET_DK)
    sign = jnp.concatenate([-jnp.ones((half,), F32), jnp.ones((half,), F32)]).reshape(1, RET_DK)
    pos = positions.astype(F32).reshape(s, 1)
    rows = min(1024, s)
    tab = jax.ShapeDtypeStruct((s, RET_DK), F32)
    return pl.pallas_call(
        _rope_table_kernel,
        out_shape=[tab, tab],
        grid=(s // rows,),
        in_specs=[pl.BlockSpec((rows, 1), lambda i: (i, 0)),
                  pl.BlockSpec((1, RET_DK), lambda i: (0, 0)),
                  pl.BlockSpec((1, RET_DK), lambda i: (0, 0))],
        out_specs=[pl.BlockSpec((rows, RET_DK), lambda i: (i, 0))] * 2,
        compiler_params=_cparams("parallel"),
        name="rope_tables",
    )(pos, freq, sign)


def _ret_kernel(q_ref, k_ref, v_ref, g_ref, cos_ref, sin_ref, dmat_ref, qdec_ref, kdec_ref, sdec_ref,
                ng_ref, o_ref, s_ref):
    @pl.when(pl.program_id(1) == 0)
    def _():
        s_ref[...] = jnp.zeros_like(s_ref)

    cos, sin = cos_ref[...], sin_ref[...]

    def rot(t):
        t = t.astype(F32)
        return t * cos + pltpu.roll(t, RET_DK // 2, axis=1) * sin

    q = rot(q_ref[...])
    k = rot(k_ref[...]) * (RET_DK ** -0.5)
    v = v_ref[...]
    scores = (_dot_nt(q.astype(BF16), k.astype(BF16)) * dmat_ref[0]).astype(BF16)
    state = s_ref[...]
    o = _dot(scores, v) + _dot((q * qdec_ref[0]).astype(BF16), state.astype(BF16))
    s_ref[...] = state * sdec_ref[0] + _dot_tn((k * kdec_ref[0]).astype(BF16), v)
    o = o * lax.rsqrt(jnp.mean(o * o, axis=-1, keepdims=True) + EPS) * ng_ref[...]
    o_ref[...] = (o * _silu(g_ref[...].astype(F32))).astype(o_ref.dtype)


def _ret_tables(rows):
    log_gamma = jnp.log1p(-jnp.exp2(-5.0 - jnp.arange(RET_HEADS, dtype=F32)))[:, None, None]
    i = jnp.arange(rows, dtype=jnp.int32)
    dist = (i[:, None] - i[None, :]).astype(F32)[None]
    ch_i, ch_j = (i // CHUNK)[:, None], (i // CHUNK)[None, :]
    dmat = jnp.where((ch_i == ch_j)[None], jnp.exp(log_gamma * jnp.abs(dist)),
                     jnp.where((ch_j < ch_i)[None], jnp.exp(log_gamma * dist), 0.0))
    fi = i.astype(F32)[None, :, None]
    qdec = jnp.broadcast_to(jnp.exp(log_gamma * (fi + 1.0)), (RET_HEADS, rows, RET_DK))
    kdec = jnp.broadcast_to(jnp.exp(log_gamma * (rows - 1.0 - fi)), (RET_HEADS, rows, RET_DK))
    sdec = jnp.broadcast_to(jnp.exp(log_gamma * float(rows)), (RET_HEADS, 1, RET_DV))
    return dmat, qdec, kdec, sdec


def _retention(proj, cos_t, sin_t, norm_g):
    s = proj.shape[0]
    rows = min(RET_BLOCK, s)
    dmat, qdec, kdec, sdec = _ret_tables(rows)
    qb, kb = P_RET_Q // RET_DK, P_RET_K // RET_DK
    vb, gb = P_RET_V // RET_DV, P_RET_G // RET_DV
    return pl.pallas_call(
        _ret_kernel,
        out_shape=jax.ShapeDtypeStruct((s, RET_V), BF16),
        grid=(RET_HEADS, s // rows),
        in_specs=[pl.BlockSpec((rows, RET_DK), lambda h, i: (i, qb + h)),
                  pl.BlockSpec((rows, RET_DK), lambda h, i: (i, kb + h)),
                  pl.BlockSpec((rows, RET_DV), lambda h, i: (i, vb + h)),
                  pl.BlockSpec((rows, RET_DV), lambda h, i: (i, gb + h)),
                  pl.BlockSpec((rows, RET_DK), lambda h, i: (i, 0)),
                  pl.BlockSpec((rows, RET_DK), lambda h, i: (i, 0)),
                  pl.BlockSpec((1, rows, rows), lambda h, i: (h, 0, 0)),
                  pl.BlockSpec((1, rows, RET_DK), lambda h, i: (h, 0, 0)),
                  pl.BlockSpec((1, rows, RET_DK), lambda h, i: (h, 0, 0)),
                  pl.BlockSpec((1, 1, RET_DV), lambda h, i: (h, 0, 0)),
                  pl.BlockSpec((1, RET_DV), lambda h, i: (0, h))],
        out_specs=pl.BlockSpec((rows, RET_DV), lambda h, i: (i, h)),
        scratch_shapes=[pltpu.VMEM((RET_DK, RET_DV), F32)],
        compiler_params=_cparams("parallel", "arbitrary"),
        name="retention",
    )(proj, proj, proj, proj, cos_t, sin_t, dmat, qdec, kdec, sdec, norm_g.reshape(1, RET_V))


def _merge_kernel(gl_ref, y0_ref, y1_ref, y2_ref, gu0_ref, gu1_ref, gu2_ref, gb0_ref, gb1_ref, gb2_ref,
                  wo0_ref, wo1_ref, wo2_ref, o_ref):
    gl = gl_ref[...]
    acc = None
    for y_ref, gu_ref, gb_ref, wo_ref in ((y0_ref, gu0_ref, gb0_ref, wo0_ref),
                                          (y1_ref, gu1_ref, gb1_ref, wo1_ref),
                                          (y2_ref, gu2_ref, gb2_ref, wo2_ref)):
        gate = jax.nn.sigmoid(_dot(gl, gu_ref[...]) + gb_ref[...])
        term = gate * _dot(y_ref[...], wo_ref[...])
        acc = term if acc is None else acc + term
    o_ref[...] = acc


def _merge(proj, ys, gate_w_up, gate_b, w_out, bm=1024, bn=512):
    s = proj.shape[0]
    bm = min(bm, s)
    d = D_MODEL
    nb = d // bn
    gate_w2 = gate_w_up.reshape(GATE_RANK, N_BRANCH * d)
    gate_b3 = gate_b.reshape(N_BRANCH, 1, d)
    y_spec = pl.BlockSpec((bm, BRANCH_W), lambda i, j: (i, 0))
    in_specs = [pl.BlockSpec((bm, GATE_RANK), lambda i, j: (i, P_GATE // GATE_RANK)), y_spec, y_spec, y_spec]
    in_specs += [pl.BlockSpec((GATE_RANK, bn), functools.partial(lambda n, i, j: (0, n * nb + j), n))
                 for n in range(N_BRANCH)]
    in_specs += [pl.BlockSpec((None, 1, bn), functools.partial(lambda n, i, j: (n, 0, j), n))
                 for n in range(N_BRANCH)]
    in_specs += [pl.BlockSpec((None, BRANCH_W, bn), functools.partial(lambda n, i, j: (n, 0, j), n))
                 for n in range(N_BRANCH)]
    return pl.pallas_call(
        _merge_kernel,
        out_shape=jax.ShapeDtypeStruct((s, d), F32),
        grid=(s // bm, nb),
        in_specs=in_specs,
        out_specs=pl.BlockSpec((bm, bn), lambda i, j: (i, j)),
        compiler_params=_cparams("parallel", "parallel"),
        name="gated_merge",
    )(proj, ys[0], ys[1], ys[2], gate_w2, gate_w2, gate_w2, gate_b3, gate_b3, gate_b3, w_out, w_out, w_out)


def _xattn_kernel(q_ref, kv_ref, o_ref):
    for h in range(XA_HEADS):
        lo, hi = h * XA_DIM, (h + 1) * XA_DIM
        s = _dot_nt(q_ref[:, lo:hi], kv_ref[:, lo:hi]) * (XA_DIM ** -0.5)
        p = jnp.exp(s - jnp.max(s, axis=-1, keepdims=True))
        p = p / jnp.sum(p, axis=-1, keepdims=True)
        o_ref[:, lo:hi] = _dot(p.astype(BF16), kv_ref[:, XA_W + lo:XA_W + hi]).astype(o_ref.dtype)


def _xattn(q, kv, rows=512):
    s = q.shape[0]
    rows = min(rows, s)
    n_mem = kv.shape[0]
    return pl.pallas_call(
        _xattn_kernel,
        out_shape=jax.ShapeDtypeStruct((s, XA_W), BF16),
        grid=(s // rows,),
        in_specs=[pl.BlockSpec((rows, XA_W), lambda i: (i, 0)),
                  pl.BlockSpec((n_mem, 2 * XA_W), lambda i: (0, 0))],
        out_specs=pl.BlockSpec((rows, XA_W), lambda i: (i, 0)),
        compiler_params=_cparams("parallel"),
        name="xattn",
    )(q, kv)


def _pack_mix_w_in(w):
    o = _IN_OFF
    big = jnp.concatenate([w[:, o[0]:o[4]], w[:, o[5]:o[7]], w[:, o[8]:o[13]]], axis=1).astype(BF16)
    small = jnp.concatenate([w[:, o[4]:o[5]], w[:, o[7]:o[8]],
                             jnp.zeros((w.shape[0], SMALL_W - GLA_RANK - SSM_HEADS), w.dtype)], axis=1).astype(BF16)
    return big, small


def kernel(x, mem, positions, norm_g, mem_norm_g, ffn1_w_in, ffn1_w_out, mix_w_in, gla_w_gk, gla_b_gk,
           gla_norm_g, ssm_conv_w, ssm_conv_b, ssm_dt_bias, ssm_a_log, ssm_d, ssm_norm_g, ret_norm_g,
           gate_w_up, gate_b, mix_w_out, xattn_w_q, xattn_w_kv, xattn_w_o, ffn2_w_in, ffn2_w_out):
    b, s, d = x.shape
    assert b == 1 and d == D_MODEL and s % max(RET_BLOCK, MIX_ROWS, 1024) == 0
    xr = x.reshape(s, d)
    memr = mem.reshape(mem.shape[1], d)
    cos_t, sin_t = _rope_tables(positions)
    expand = _ssd_expand_table()
    h = _norm_cast(xr, norm_g[0, 0])
    for l in range(DEPTH):
        ng = norm_g[l]
        a = _swiglu_in(h, ffn1_w_in[l].astype(BF16))
        y = _matmul(a, ffn1_w_out[l].astype(BF16), F32, 1024, 1024, "ffn_out")
        xr, h = _post(y, xr, ng[1], 0.5, ng[2])
        w_big, w_small = _pack_mix_w_in(mix_w_in[l])
        proj = _matmul(h, w_big, BF16, 1024, 1280, "mix_in")
        small = _matmul(h, w_small, F32, 1024, SMALL_W, "mix_in_small")
        w_gk_pad = jnp.concatenate([gla_w_gk[l], jnp.zeros((SMALL_W - GLA_RANK, GLA_QK), F32)], axis=0)
        y_gla = _gla(proj, small, w_gk_pad, gla_b_gk[l], gla_norm_g[l])
        y_ssd = _ssd(proj, small, expand, ssm_conv_w[l], ssm_conv_b[l], ssm_dt_bias[l], ssm_a_log[l],
                     ssm_d[l], ssm_norm_g[l])
        y_ret = _retention(proj, cos_t, sin_t, ret_norm_g[l])
        mixed = _merge(proj, (y_gla, y_ssd, y_ret), gate_w_up[l].astype(BF16), gate_b[l],
                       mix_w_out[l].astype(BF16))
        xr, h = _post(mixed, xr, ng[3], 1.0, ng[4])
        q = _matmul(h, xattn_w_q[l].astype(BF16), BF16, 1024, 1024, "xattn_q")
        mem_n = _norm_cast(memr, mem_norm_g[l])
        kv = _matmul(mem_n, xattn_w_kv[l].astype(BF16), BF16, 256, 1024, "xattn_kv")
        o = _xattn(q, kv)
        xa = _matmul(o, xattn_w_o[l].astype(BF16), F32, 1024, 1024, "xattn_o")
        xr, h = _post(xa, xr, ng[5], 1.0, ng[6])
        a = _swiglu_in(h, ffn2_w_in[l].astype(BF16))
        y = _matmul(a, ffn2_w_out[l].astype(BF16), F32, 1024, 1024, "ffn_out")
        g_next = norm_g[l + 1, 0] if l + 1 < DEPTH else None
        xr, h = _post(y, xr, ng[7], 0.5, g_next)
    return xr.reshape(b, s, d)
```

```python
import functools

import numpy as np
import jax
import jax.numpy as jnp
from jax import lax
from jax.experimental import pallas as pl
from jax.experimental.pallas import tpu as pltpu

F32 = jnp.float32
BF16 = jnp.bfloat16

D_MODEL = 4096
DEPTH = 4
CHUNK = 64
D_FF = 4096
EPS = 1e-6
GLA_HEADS, GLA_DK, GLA_DV, GLA_RANK, GLA_TAU = 4, 256, 512, 16, 16.0
GLA_QK, GLA_V = GLA_HEADS * GLA_DK, GLA_HEADS * GLA_DV
SSM_HEADS, SSM_HEADDIM, SSM_GROUPS, SSM_STATE, SSM_CONV = 32, 64, 4, 128, 4
SSM_INNER = SSM_HEADS * SSM_HEADDIM
SSM_BC = SSM_GROUPS * SSM_STATE
SSM_CONV_DIM = SSM_INNER + 2 * SSM_BC
SSM_HPG = SSM_HEADS // SSM_GROUPS
SSM_GW = SSM_INNER // SSM_GROUPS
RET_HEADS, RET_DK, RET_DV = 8, 128, 256
RET_QK, RET_V = RET_HEADS * RET_DK, RET_HEADS * RET_DV
ROPE_BASE = 10000.0
XA_HEADS, XA_DIM = 4, 256
XA_W = XA_HEADS * XA_DIM
N_BRANCH, BRANCH_W, GATE_RANK = 3, 2048, 512

_IN_WIDTHS = (GLA_QK, GLA_QK, GLA_V, GLA_V, GLA_RANK, SSM_INNER, SSM_CONV_DIM, SSM_HEADS,
              RET_QK, RET_QK, RET_V, RET_V, GATE_RANK)
_IN_OFF = np.concatenate([[0], np.cumsum(_IN_WIDTHS)]).tolist()
IN_COLS = _IN_OFF[-1]
P_GLA_Q, P_GLA_K, P_GLA_V, P_GLA_R = 0, 1024, 2048, 4096
P_SSM_Z, P_SSM_X, P_SSM_B, P_SSM_C = 6144, 8192, 10240, 10752
P_RET_Q, P_RET_K, P_RET_V, P_RET_G = 11264, 12288, 13312, 15360
P_GATE = 17408
P_COLS = 17920
SMALL_W = 128
SMALL_DT0 = GLA_RANK

V7X_VMEM_LIMIT = 60 * 1024 * 1024

RET_BLOCK = 512
MIX_ROWS = 512
CUM_BLOCK = 256


def _cparams(*sem):
    return pltpu.CompilerParams(dimension_semantics=sem, vmem_limit_bytes=V7X_VMEM_LIMIT)


def _silu(x):
    return x * jax.nn.sigmoid(x)


def _dot(a, b):
    return jnp.dot(a, b, preferred_element_type=F32)


def _dot_nt(a, b):
    return lax.dot_general(a, b, (((1,), (1,)), ((), ())), preferred_element_type=F32)


def _dot_tn(a, b):
    return lax.dot_general(a, b, (((0,), (0,)), ((), ())), preferred_element_type=F32)


def _split3(x):
    hi = x.astype(BF16)
    r1 = x - hi.astype(F32)
    mid = r1.astype(BF16)
    lo = (r1 - mid.astype(F32)).astype(BF16)
    return hi, mid, lo


def _sel_dot(sel, x):
    hi, mid, lo = _split3(x)
    return _dot(sel, hi) + _dot(sel, mid) + _dot(sel, lo)


def _chunk_tril(n):
    ri = lax.broadcasted_iota(jnp.int32, (n, n), 0)
    ci = lax.broadcasted_iota(jnp.int32, (n, n), 1)
    same = jnp.right_shift(ri, 6) == jnp.right_shift(ci, 6)
    return jnp.where(same & (ri >= ci), 1.0, 0.0).astype(BF16)


def _cast_kernel(w_ref, o_ref):
    o_ref[...] = w_ref[...].astype(o_ref.dtype)


def _cast_bf16(w, rows):
    r, c = w.shape
    return pl.pallas_call(
        _cast_kernel,
        out_shape=jax.ShapeDtypeStruct((r, c), BF16),
        grid=(r // rows,),
        in_specs=[pl.BlockSpec((rows, c), lambda i: (i, 0))],
        out_specs=pl.BlockSpec((rows, c), lambda i: (i, 0)),
        compiler_params=_cparams("parallel"),
        name="cast_bf16",
    )(w)


def _repack_kernel(w_ref, big_ref, small_ref):
    o = _IN_OFF
    big_ref[:, 0:P_SSM_Z] = w_ref[:, o[0]:o[4]].astype(BF16)
    big_ref[:, P_SSM_Z:P_RET_Q] = w_ref[:, o[5]:o[7]].astype(BF16)
    big_ref[:, P_RET_Q:P_COLS] = w_ref[:, o[8]:o[13]].astype(BF16)
    small_ref[...] = jnp.zeros(small_ref.shape, BF16)
    small_ref[:, 0:GLA_RANK] = w_ref[:, o[4]:o[5]].astype(BF16)
    small_ref[:, GLA_RANK:GLA_RANK + SSM_HEADS] = w_ref[:, o[7]:o[8]].astype(BF16)


def _repack_mix_w_in(w, rows=128):
    nl, d, c = w.shape
    return pl.pallas_call(
        _repack_kernel,
        out_shape=[jax.ShapeDtypeStruct((nl, d, P_COLS), BF16), jax.ShapeDtypeStruct((nl, d, SMALL_W), BF16)],
        grid=(nl, d // rows),
        in_specs=[pl.BlockSpec((None, rows, c), lambda l, i: (l, i, 0))],
        out_specs=[pl.BlockSpec((None, rows, P_COLS), lambda l, i: (l, i, 0)),
                   pl.BlockSpec((None, rows, SMALL_W), lambda l, i: (l, i, 0))],
        compiler_params=_cparams("parallel", "parallel"),
        name="repack_mix_w_in",
    )(w)


def _norm_cast_kernel(x_ref, g_ref, o_ref):
    x = x_ref[...]
    r = lax.rsqrt(jnp.mean(x * x, axis=-1, keepdims=True) + EPS)
    o_ref[...] = (x * r * g_ref[...]).astype(o_ref.dtype)


def _norm_cast(x, g, rows=256):
    m, d = x.shape
    rows = min(rows, m)
    return pl.pallas_call(
        _norm_cast_kernel,
        out_shape=jax.ShapeDtypeStruct((m, d), BF16),
        grid=(m // rows,),
        in_specs=[pl.BlockSpec((rows, d), lambda i: (i, 0)), pl.BlockSpec((1, d), lambda i: (0, 0))],
        out_specs=pl.BlockSpec((rows, d), lambda i: (i, 0)),
        compiler_params=_cparams("parallel"),
        name="norm_cast",
    )(x, g.reshape(1, d))


def _post_kernel(scale, emit_h, y_ref, x_ref, gp_ref, gn_ref, xo_ref, *maybe_h):
    y = y_ref[...].astype(F32)
    r = lax.rsqrt(jnp.mean(y * y, axis=-1, keepdims=True) + EPS)
    xn = x_ref[...] + scale * (y * r * gp_ref[...])
    xo_ref[...] = xn
    if emit_h:
        r2 = lax.rsqrt(jnp.mean(xn * xn, axis=-1, keepdims=True) + EPS)
        maybe_h[0][...] = (xn * r2 * gn_ref[...]).astype(BF16)


def _post(y, x, g_post, scale, g_next, rows=256):
    m, d = x.shape
    emit_h = g_next is not None
    gn = g_next if emit_h else g_post
    row_spec = pl.BlockSpec((rows, d), lambda i: (i, 0))
    vec_spec = pl.BlockSpec((1, d), lambda i: (0, 0))
    out_shape = [jax.ShapeDtypeStruct((m, d), F32)]
    out_specs = [row_spec]
    if emit_h:
        out_shape.append(jax.ShapeDtypeStruct((m, d), BF16))
        out_specs.append(row_spec)
    res = pl.pallas_call(
        functools.partial(_post_kernel, scale, emit_h),
        out_shape=out_shape,
        grid=(m // rows,),
        in_specs=[row_spec, row_spec, vec_spec, vec_spec],
        out_specs=out_specs,
        compiler_params=_cparams("parallel"),
        name="post_norm",
    )(y, x, g_post.reshape(1, d), gn.reshape(1, d))
    return (res[0], res[1]) if emit_h else (res[0], None)


def _mm_kernel(x_ref, w_ref, o_ref):
    o_ref[...] = _dot(x_ref[...], w_ref[...]).astype(o_ref.dtype)


def _matmul(x, w, layer, out_dtype, bm, bn, name):
    m, k = x.shape
    n = w.shape[2]
    bm, bn = min(bm, m), min(bn, n)
    return pl.pallas_call(
        _mm_kernel,
        out_shape=jax.ShapeDtypeStruct((m, n), out_dtype),
        grid=(m // bm, n // bn),
        in_specs=[pl.BlockSpec((bm, k), lambda i, j: (i, 0)),
                  pl.BlockSpec((None, k, bn), lambda i, j: (layer, 0, j))],
        out_specs=pl.BlockSpec((bm, bn), lambda i, j: (i, j)),
        compiler_params=_cparams("parallel", "parallel"),
        name=name,
    )(x, w)


def _mm_wcast_kernel(x_ref, w_ref, o_ref, w_scr):
    @pl.when(pl.program_id(1) == 0)
    def _():
        w_scr[...] = w_ref[...].astype(BF16)

    o_ref[...] = _dot(x_ref[...], w_scr[...]).astype(o_ref.dtype)


def _matmul_wcast(x, w, layer, out_dtype, bm, bn, name):
    m, k = x.shape
    n = w.shape[2]
    bm, bn = min(bm, m), min(bn, n)
    return pl.pallas_call(
        _mm_wcast_kernel,
        out_shape=jax.ShapeDtypeStruct((m, n), out_dtype),
        grid=(n // bn, m // bm),
        in_specs=[pl.BlockSpec((bm, k), lambda j, i: (i, 0)),
                  pl.BlockSpec((None, k, bn), lambda j, i: (layer, 0, j))],
        out_specs=pl.BlockSpec((bm, bn), lambda j, i: (i, j)),
        scratch_shapes=[pltpu.VMEM((k, bn), BF16)],
        compiler_params=_cparams("arbitrary", "arbitrary"),
        name=name,
    )(x, w)


def _swiglu_kernel(bn, x_ref, wg_ref, wu_ref, o_ref, w_scr):
    @pl.when(pl.program_id(1) == 0)
    def _():
        w_scr[:, 0:bn] = wg_ref[...].astype(BF16)
        w_scr[:, bn:2 * bn] = wu_ref[...].astype(BF16)

    r = _dot(x_ref[...], w_scr[...])
    o_ref[...] = (_silu(r[:, 0:bn]) * r[:, bn:2 * bn]).astype(o_ref.dtype)


def _swiglu_in(x, w, layer, bm=1024, bn=256):
    m, k = x.shape
    f = w.shape[2] // 2
    nb = f // bn
    return pl.pallas_call(
        functools.partial(_swiglu_kernel, bn),
        out_shape=jax.ShapeDtypeStruct((m, f), BF16),
        grid=(nb, m // bm),
        in_specs=[pl.BlockSpec((bm, k), lambda j, i: (i, 0)),
                  pl.BlockSpec((None, k, bn), lambda j, i: (layer, 0, j)),
                  pl.BlockSpec((None, k, bn), lambda j, i: (layer, 0, j + nb))],
        out_specs=pl.BlockSpec((bm, bn), lambda j, i: (i, j)),
        scratch_shapes=[pltpu.VMEM((k, 2 * bn), BF16)],
        compiler_params=_cparams("arbitrary", "arbitrary"),
        name="swiglu_in",
    )(x, w, w)


def _gla_kernel(q_ref, k_ref, v_ref, r_ref, sm_ref, wgk_ref, bgk_ref, ng_ref, o_ref, st_ref, cum_ref):
    @pl.when(pl.program_id(1) == 0)
    def _():
        st_ref[...] = jnp.zeros_like(st_ref)

    rows = q_ref.shape[0]
    a_hi, a_mid, _ = _split3(sm_ref[...])
    w_hi, w_mid, _ = _split3(wgk_ref[...])
    z = _dot(a_hi, w_hi) + _dot(a_hi, w_mid) + _dot(a_mid, w_hi) + bgk_ref[...]
    logf = (jnp.minimum(z, 0.0) - jnp.log1p(jnp.exp(-jnp.abs(z)))) * (1.0 / GLA_TAU)
    tri = _chunk_tril(CUM_BLOCK)
    for b in range(rows // CUM_BLOCK):
        sl = slice(b * CUM_BLOCK, (b + 1) * CUM_BLOCK)
        cum_ref[sl, :] = _sel_dot(tri, logf[sl])

    ri = lax.broadcasted_iota(jnp.int32, (CHUNK, CHUNK), 0)
    ci = lax.broadcasted_iota(jnp.int32, (CHUNK, CHUNK), 1)
    tril = ri >= ci
    for c in range(rows // CHUNK):
        sl = pl.ds(c * CHUNK, CHUNK)
        cum = cum_ref[sl, :]
        cum_end = cum[CHUNK - 1:CHUNK, :]
        up, down = jnp.exp(cum), jnp.exp(-cum)
        q = q_ref[sl, :].astype(F32) * (GLA_DK ** -0.5)
        k = k_ref[sl, :].astype(F32)
        v = v_ref[sl, :]
        q_dec = (q * up).astype(BF16)
        a_past = _dot_nt(q_dec, (k * down).astype(BF16))
        a_future = _dot_nt((q * down).astype(BF16), (k * up).astype(BF16))
        attn = jnp.where(tril, a_past, a_future).astype(BF16)
        state_t = st_ref[...]
        o = _dot(attn, v) + _dot_nt(q_dec, state_t.astype(BF16))
        contrib_t = _dot_tn(v, (k * jnp.exp(cum_end - cum)).astype(BF16))
        st_ref[...] = state_t * jnp.exp(cum_end) + contrib_t
        o = o * lax.rsqrt(jnp.mean(o * o, axis=-1, keepdims=True) + EPS) * ng_ref[...]
        o_ref[sl, :] = (o * _silu(r_ref[sl, :].astype(F32))).astype(o_ref.dtype)


def _gla(proj, small, w_gk_pad, b_gk, norm_g):
    s = proj.shape[0]
    rows = min(MIX_ROWS, s)
    qb, vb = P_GLA_Q // GLA_DK, P_GLA_V // GLA_DV
    kb, rb = P_GLA_K // GLA_DK, P_GLA_R // GLA_DV
    return pl.pallas_call(
        _gla_kernel,
        out_shape=jax.ShapeDtypeStruct((s, GLA_V), BF16),
        grid=(GLA_HEADS, s // rows),
        in_specs=[pl.BlockSpec((rows, GLA_DK), lambda h, i: (i, qb + h)),
                  pl.BlockSpec((rows, GLA_DK), lambda h, i: (i, kb + h)),
                  pl.BlockSpec((rows, GLA_DV), lambda h, i: (i, vb + h)),
                  pl.BlockSpec((rows, GLA_DV), lambda h, i: (i, rb + h)),
                  pl.BlockSpec((rows, SMALL_W), lambda h, i: (i, 0)),
                  pl.BlockSpec((SMALL_W, GLA_DK), lambda h, i: (0, h)),
                  pl.BlockSpec((1, GLA_DK), lambda h, i: (0, h)),
                  pl.BlockSpec((1, GLA_DV), lambda h, i: (0, h))],
        out_specs=pl.BlockSpec((rows, GLA_DV), lambda h, i: (i, h)),
        scratch_shapes=[pltpu.VMEM((GLA_DV, GLA_DK), F32), pltpu.VMEM((rows, GLA_DK), F32)],
        compiler_params=_cparams("parallel", "arbitrary"),
        name="gla",
    )(proj, proj, proj, proj, small, w_gk_pad, b_gk.reshape(1, GLA_QK), norm_g.reshape(1, GLA_V))


def _causal_conv_silu(xp_ref, x_ref, w_ref, b_ref, first):
    rows = x_ref.shape[0]

    @pl.when(first)
    def _():
        xp_ref[pl.ds(0, 8), :] = jnp.zeros((8, xp_ref.shape[1]), F32)

    xp_ref[pl.ds(8, rows), :] = x_ref[...].astype(F32)
    acc = b_ref[...]
    for tap in range(SSM_CONV):
        acc = acc + xp_ref[pl.ds(8 - (SSM_CONV - 1) + tap, rows), :] * w_ref[tap:tap + 1, :]
    xp_ref[pl.ds(0, 8), :] = xp_ref[pl.ds(rows, 8), :]
    return _silu(acc)


def _ssd_kernel(z_ref, xs_ref, b_ref, c_ref, sm_ref, e_ref, cwx_ref, cwb_ref, cwc_ref,
                cbx_ref, cbb_ref, cbc_ref, dtb_ref, alog_ref, dsk_ref, ng_ref,
                o_ref, s_ref, xpx_ref, xpb_ref, xpc_ref, xs_scr, b_scr, c_scr, dt_scr, cum_scr):
    first = pl.program_id(1) == 0

    @pl.when(first)
    def _():
        s_ref[...] = jnp.zeros_like(s_ref)

    rows = xs_ref.shape[0]
    xs_scr[...] = _causal_conv_silu(xpx_ref, xs_ref, cwx_ref, cbx_ref, first)
    b_scr[...] = _causal_conv_silu(xpb_ref, b_ref, cwb_ref, cbb_ref, first).astype(BF16)
    c_scr[...] = _causal_conv_silu(xpc_ref, c_ref, cwc_ref, cbc_ref, first).astype(BF16)

    dt_in = _sel_dot_rhs(sm_ref[...], e_ref[0]) + dtb_ref[...]
    dt = jnp.maximum(dt_in, 0.0) + jnp.log1p(jnp.exp(-jnp.abs(dt_in)))
    dt_scr[...] = dt
    da = dt * (-jnp.exp(alog_ref[...]))
    tri = _chunk_tril(CUM_BLOCK)
    for b in range(rows // CUM_BLOCK):
        sl = slice(b * CUM_BLOCK, (b + 1) * CUM_BLOCK)
        cum_scr[sl, :] = _sel_dot(tri, da[sl])

    gw = SSM_GW
    rw = lax.broadcasted_iota(jnp.int32, (CHUNK, gw), 0)
    cw = lax.broadcasted_iota(jnp.int32, (CHUNK, gw), 1)
    diag_tile = (jnp.bitwise_and(cw, SSM_HEADDIM - 1) == rw).astype(F32)
    half = gw // 2
    rb = lax.broadcasted_iota(jnp.int32, (half, half), 0)
    cb_ = lax.broadcasted_iota(jnp.int32, (half, half), 1)
    head_mask = jnp.right_shift(rb, 6) == jnp.right_shift(cb_, 6)
    reps = half // CHUNK
    for c in range(rows // CHUNK):
        sl = pl.ds(c * CHUNK, CHUNK)
        cum = cum_scr[sl, :]
        cum_row = jnp.sum(cum * diag_tile, axis=0, keepdims=True)
        seg = jnp.exp(-jnp.abs(cum - cum_row))
        xs = xs_scr[sl, :]
        bm = b_scr[sl, :]
        cm = c_scr[sl, :]
        xdt = xs * dt_scr[sl, :]
        cbt = _dot_nt(cm, jnp.concatenate([bm] * SSM_HPG, axis=0))
        m_all = (cbt * seg).astype(BF16)
        intra = []
        for hlf in range(2):
            xh = xdt[:, hlf * half:(hlf + 1) * half]
            bd = jnp.where(head_mask, jnp.concatenate([xh] * reps, axis=0), 0.0).astype(BF16)
            intra.append(_dot(m_all[:, hlf * half:(hlf + 1) * half], bd))
        intra = jnp.concatenate(intra, axis=1)
        cum_end = cum[CHUNK - 1:CHUNK, :]
        state = s_ref[...]
        inter = _dot(cm, state.astype(BF16)) * jnp.exp(cum)
        contrib = _dot_tn(bm, (xdt * jnp.exp(cum_end - cum)).astype(BF16))
        s_ref[...] = state * jnp.exp(cum_end) + contrib
        y = intra + inter + xs * dsk_ref[...]
        y = y * _silu(z_ref[sl, :].astype(F32))
        y = y * lax.rsqrt(jnp.mean(y * y, axis=-1, keepdims=True) + EPS) * ng_ref[...]
        o_ref[sl, :] = y.astype(o_ref.dtype)


def _sel_dot_rhs(x, sel):
    hi, mid, lo = _split3(x)
    return _dot(hi, sel) + _dot(mid, sel) + _dot(lo, sel)


def _ssd(proj, small, expand, conv_w, conv_b, dt_bias, a_log, d_skip, norm_g):
    s = proj.shape[0]
    rows = min(MIX_ROWS, s)
    gw, st = SSM_GW, SSM_STATE
    zb, xb = P_SSM_Z // gw, P_SSM_X // gw
    bb, cb = P_SSM_B // st, P_SSM_C // st
    cwb, cwc = SSM_INNER // st, (SSM_INNER + SSM_BC) // st

    def lane_expand(p):
        return jnp.repeat(p.astype(F32), SSM_HEADDIM).reshape(1, SSM_INNER)

    conv_b2 = conv_b.reshape(1, SSM_CONV_DIM)
    vec = pl.BlockSpec((1, gw), lambda g, i: (0, g))
    return pl.pallas_call(
        _ssd_kernel,
        out_shape=jax.ShapeDtypeStruct((s, SSM_INNER), BF16),
        grid=(SSM_GROUPS, s // rows),
        in_specs=[pl.BlockSpec((rows, gw), lambda g, i: (i, zb + g)),
                  pl.BlockSpec((rows, gw), lambda g, i: (i, xb + g)),
                  pl.BlockSpec((rows, st), lambda g, i: (i, bb + g)),
                  pl.BlockSpec((rows, st), lambda g, i: (i, cb + g)),
                  pl.BlockSpec((rows, SMALL_W), lambda g, i: (i, 0)),
                  pl.BlockSpec((1, SMALL_W, gw), lambda g, i: (g, 0, 0)),
                  pl.BlockSpec((SSM_CONV, gw), lambda g, i: (0, g)),
                  pl.BlockSpec((SSM_CONV, st), lambda g, i: (0, cwb + g)),
                  pl.BlockSpec((SSM_CONV, st), lambda g, i: (0, cwc + g)),
                  pl.BlockSpec((1, gw), lambda g, i: (0, g)),
                  pl.BlockSpec((1, st), lambda g, i: (0, cwb + g)),
                  pl.BlockSpec((1, st), lambda g, i: (0, cwc + g)),
                  vec, vec, vec, vec],
        out_specs=pl.BlockSpec((rows, gw), lambda g, i: (i, g)),
        scratch_shapes=[pltpu.VMEM((st, gw), F32),
                        pltpu.VMEM((rows + 8, gw), F32),
                        pltpu.VMEM((rows + 8, st), F32),
                        pltpu.VMEM((rows + 8, st), F32),
                        pltpu.VMEM((rows, gw), F32),
                        pltpu.VMEM((rows, st), BF16),
                        pltpu.VMEM((rows, st), BF16),
                        pltpu.VMEM((rows, gw), F32),
                        pltpu.VMEM((rows, gw), F32)],
        compiler_params=_cparams("parallel", "arbitrary"),
        name="ssd",
    )(proj, proj, proj, proj, small, expand, conv_w, conv_w, conv_w, conv_b2, conv_b2, conv_b2,
      lane_expand(dt_bias), lane_expand(a_log), lane_expand(d_skip), norm_g.reshape(1, SSM_INNER))


def _ssd_expand_table():
    rows = jnp.arange(SMALL_W, dtype=jnp.int32)[None, :, None]
    cols = jnp.arange(SSM_GW, dtype=jnp.int32)[None, None, :]
    grp = jnp.arange(SSM_GROUPS, dtype=jnp.int32)[:, None, None]
    return (rows == SMALL_DT0 + grp * SSM_HPG + cols // SSM_HEADDIM).astype(BF16)


def _rope_table_kernel(pos_ref, freq_ref, sign_ref, cos_ref, sin_ref):
    ang = pos_ref[...] * freq_ref[...]
    cos_ref[...] = jnp.cos(ang)
    sin_ref[...] = jnp.sin(ang) * sign_ref[...]


def _rope_tables(positions):
    s = positions.shape[-1]
    half = RET_DK // 2
    inv_freq = ROPE_BASE ** (-jnp.arange(half, dtype=F32) / half)
    freq = jnp.concatenate([inv_freq, inv_freq]).reshape(1, RET_DK)
    sign = jnp.concatenate([-jnp.ones((half,), F32), jnp.ones((half,), F32)]).reshape(1, RET_DK)
    pos = positions.astype(F32).reshape(s, 1)
    rows = min(1024, s)
    tab = jax.ShapeDtypeStruct((s, RET_DK), F32)
    return pl.pallas_call(
        _rope_table_kernel,
        out_shape=[tab, tab],
        grid=(s // rows,),
        in_specs=[pl.BlockSpec((rows, 1), lambda i: (i, 0)),
                  pl.BlockSpec((1, RET_DK), lambda i: (0, 0)),
                  pl.BlockSpec((1, RET_DK), lambda i: (0, 0))],
        out_specs=[pl.BlockSpec((rows, RET_DK), lambda i: (i, 0))] * 2,
        compiler_params=_cparams("parallel"),
        name="rope_tables",
    )(pos, freq, sign)


def _ret_kernel(q_ref, k_ref, v_ref, g_ref, cos_ref, sin_ref, dmat_ref, qdec_ref, kdec_ref, sdec_ref,
                ng_ref, o_ref, s_ref):
    @pl.when(pl.program_id(1) == 0)
    def _():
        s_ref[...] = jnp.zeros_like(s_ref)

    cos, sin = cos_ref[...], sin_ref[...]

    def rot(t):
        t = t.astype(F32)
        return t * cos + pltpu.roll(t, RET_DK // 2, axis=1) * sin

    q = rot(q_ref[...])
    k = rot(k_ref[...]) * (RET_DK ** -0.5)
    v = v_ref[...]
    scores = (_dot_nt(q.astype(BF16), k.astype(BF16)) * dmat_ref[0]).astype(BF16)
    state = s_ref[...]
    o = _dot(scores, v) + _dot((q * qdec_ref[0]).astype(BF16), state.astype(BF16))
    s_ref[...] = state * sdec_ref[0] + _dot_tn((k * kdec_ref[0]).astype(BF16), v)
    o = o * lax.rsqrt(jnp.mean(o * o, axis=-1, keepdims=True) + EPS) * ng_ref[...]
    o_ref[...] = (o * _silu(g_ref[...].astype(F32))).astype(o_ref.dtype)


def _ret_tables(rows):
    log_gamma = jnp.log1p(-jnp.exp2(-5.0 - jnp.arange(RET_HEADS, dtype=F32)))[:, None, None]
    i = jnp.arange(rows, dtype=jnp.int32)
    dist = (i[:, None] - i[None, :]).astype(F32)[None]
    ch_i, ch_j = (i // CHUNK)[:, None], (i // CHUNK)[None, :]
    dmat = jnp.where((ch_i == ch_j)[None], jnp.exp(log_gamma * jnp.abs(dist)),
                     jnp.where((ch_j < ch_i)[None], jnp.exp(log_gamma * dist), 0.0))
    fi = i.astype(F32)[None, :, None]
    qdec = jnp.broadcast_to(jnp.exp(log_gamma * (fi + 1.0)), (RET_HEADS, rows, RET_DK))
    kdec = jnp.broadcast_to(jnp.exp(log_gamma * (rows - 1.0 - fi)), (RET_HEADS, rows, RET_DK))
    sdec = jnp.broadcast_to(jnp.exp(log_gamma * float(rows)), (RET_HEADS, 1, RET_DV))
    return dmat, qdec, kdec, sdec


def _retention(proj, cos_t, sin_t, tables, norm_g):
    s = proj.shape[0]
    rows = min(RET_BLOCK, s)
    dmat, qdec, kdec, sdec = tables
    qb, kb = P_RET_Q // RET_DK, P_RET_K // RET_DK
    vb, gb = P_RET_V // RET_DV, P_RET_G // RET_DV
    return pl.pallas_call(
        _ret_kernel,
        out_shape=jax.ShapeDtypeStruct((s, RET_V), BF16),
        grid=(RET_HEADS, s // rows),
        in_specs=[pl.BlockSpec((rows, RET_DK), lambda h, i: (i, qb + h)),
                  pl.BlockSpec((rows, RET_DK), lambda h, i: (i, kb + h)),
                  pl.BlockSpec((rows, RET_DV), lambda h, i: (i, vb + h)),
                  pl.BlockSpec((rows, RET_DV), lambda h, i: (i, gb + h)),
                  pl.BlockSpec((rows, RET_DK), lambda h, i: (i, 0)),
                  pl.BlockSpec((rows, RET_DK), lambda h, i: (i, 0)),
                  pl.BlockSpec((1, rows, rows), lambda h, i: (h, 0, 0)),
                  pl.BlockSpec((1, rows, RET_DK), lambda h, i: (h, 0, 0)),
                  pl.BlockSpec((1, rows, RET_DK), lambda h, i: (h, 0, 0)),
                  pl.BlockSpec((1, 1, RET_DV), lambda h, i: (h, 0, 0)),
                  pl.BlockSpec((1, RET_DV), lambda h, i: (0, h))],
        out_specs=pl.BlockSpec((rows, RET_DV), lambda h, i: (i, h)),
        scratch_shapes=[pltpu.VMEM((RET_DK, RET_DV), F32)],
        compiler_params=_cparams("parallel", "arbitrary"),
        name="retention",
    )(proj, proj, proj, proj, cos_t, sin_t, dmat, qdec, kdec, sdec, norm_g.reshape(1, RET_V))


def _merge_kernel(gl_ref, y0_ref, y1_ref, y2_ref, gu0_ref, gu1_ref, gu2_ref, gb0_ref, gb1_ref, gb2_ref,
                  wo0_ref, wo1_ref, wo2_ref, o_ref):
    gl = gl_ref[...]
    acc = None
    for y_ref, gu_ref, gb_ref, wo_ref in ((y0_ref, gu0_ref, gb0_ref, wo0_ref),
                                          (y1_ref, gu1_ref, gb1_ref, wo1_ref),
                                          (y2_ref, gu2_ref, gb2_ref, wo2_ref)):
        gate = jax.nn.sigmoid(_dot(gl, gu_ref[...]) + gb_ref[...])
        term = gate * _dot(y_ref[...], wo_ref[...])
        acc = term if acc is None else acc + term
    o_ref[...] = acc.astype(o_ref.dtype)


def _merge(proj, ys, gate_w2, gate_b, w_out, layer, bm=1024, bn=512):
    s = proj.shape[0]
    bm = min(bm, s)
    d = D_MODEL
    nb = d // bn
    gate_b4 = gate_b.reshape(DEPTH, N_BRANCH, 1, d)
    y_spec = pl.BlockSpec((bm, BRANCH_W), lambda i, j: (i, 0))
    in_specs = [pl.BlockSpec((bm, GATE_RANK), lambda i, j: (i, P_GATE // GATE_RANK)), y_spec, y_spec, y_spec]
    in_specs += [pl.BlockSpec((None, GATE_RANK, bn), functools.partial(lambda n, i, j: (layer, 0, n * nb + j), n))
                 for n in range(N_BRANCH)]
    in_specs += [pl.BlockSpec((None, None, 1, bn), functools.partial(lambda n, i, j: (layer, n, 0, j), n))
                 for n in range(N_BRANCH)]
    in_specs += [pl.BlockSpec((None, BRANCH_W, bn),
                              functools.partial(lambda n, i, j: (layer * N_BRANCH + n, 0, j), n))
                 for n in range(N_BRANCH)]
    return pl.pallas_call(
        _merge_kernel,
        out_shape=jax.ShapeDtypeStruct((s, d), BF16),
        grid=(s // bm, nb),
        in_specs=in_specs,
        out_specs=pl.BlockSpec((bm, bn), lambda i, j: (i, j)),
        compiler_params=_cparams("parallel", "parallel"),
        name="gated_merge",
    )(proj, ys[0], ys[1], ys[2], gate_w2, gate_w2, gate_w2, gate_b4, gate_b4, gate_b4, w_out, w_out, w_out)


def _xattn_kernel(q_ref, kv_ref, o_ref):
    for h in range(XA_HEADS):
        lo, hi = h * XA_DIM, (h + 1) * XA_DIM
        s = _dot_nt(q_ref[:, lo:hi], kv_ref[:, lo:hi]) * (XA_DIM ** -0.5)
        p = jnp.exp(s - jnp.max(s, axis=-1, keepdims=True))
        p = p / jnp.sum(p, axis=-1, keepdims=True)
        o_ref[:, lo:hi] = _dot(p.astype(BF16), kv_ref[:, XA_W + lo:XA_W + hi]).astype(o_ref.dtype)


def _xattn(q, kv, rows=512):
    s = q.shape[0]
    rows = min(rows, s)
    n_mem = kv.shape[0]
    return pl.pallas_call(
        _xattn_kernel,
        out_shape=jax.ShapeDtypeStruct((s, XA_W), BF16),
        grid=(s // rows,),
        in_specs=[pl.BlockSpec((rows, XA_W), lambda i: (i, 0)),
                  pl.BlockSpec((n_mem, 2 * XA_W), lambda i: (0, 0))],
        out_specs=pl.BlockSpec((rows, XA_W), lambda i: (i, 0)),
        compiler_params=_cparams("parallel"),
        name="xattn",
    )(q, kv)


def kernel(x, mem, positions, norm_g, mem_norm_g, ffn1_w_in, ffn1_w_out, mix_w_in, gla_w_gk, gla_b_gk,
           gla_norm_g, ssm_conv_w, ssm_conv_b, ssm_dt_bias, ssm_a_log, ssm_d, ssm_norm_g, ret_norm_g,
           gate_w_up, gate_b, mix_w_out, xattn_w_q, xattn_w_kv, xattn_w_o, ffn2_w_in, ffn2_w_out):
    b, s, d = x.shape
    assert b == 1 and d == D_MODEL and s % max(RET_BLOCK, MIX_ROWS, 1024) == 0
    nl = DEPTH
    xr = x.reshape(s, d)
    memr = mem.reshape(mem.shape[1], d)
    cos_t, sin_t = _rope_tables(positions)
    ret_tabs = _ret_tables(min(RET_BLOCK, s))
    expand = _ssd_expand_table()

    ffn1_out = _cast_bf16(ffn1_w_out.reshape(nl * D_FF, d), 512).reshape(nl, D_FF, d)
    ffn2_out = _cast_bf16(ffn2_w_out.reshape(nl * D_FF, d), 512).reshape(nl, D_FF, d)
    xa_out = _cast_bf16(xattn_w_o.reshape(nl * XA_W, d), 512).reshape(nl, XA_W, d)
    mix_out = _cast_bf16(mix_w_out.reshape(nl * N_BRANCH * BRANCH_W, d), 512).reshape(nl * N_BRANCH, BRANCH_W, d)
    gate_up = _cast_bf16(gate_w_up.reshape(nl * GATE_RANK, N_BRANCH * d), 256).reshape(nl, GATE_RANK, N_BRANCH * d)
    mix_in_big, mix_in_small = _repack_mix_w_in(mix_w_in)
    w_gk_pad = jnp.concatenate([gla_w_gk, jnp.zeros((nl, SMALL_W - GLA_RANK, GLA_QK), F32)], axis=1)

    h = _norm_cast(xr, norm_g[0, 0])
    for l in range(nl):
        ng = norm_g[l]
        a = _swiglu_in(h, ffn1_w_in, l)
        y = _matmul(a, ffn1_out, l, BF16, 1024, 1024, "ffn_out")
        xr, h = _post(y, xr, ng[1], 0.5, ng[2])
        proj = _matmul(h, mix_in_big, l, BF16, 1024, 1280, "mix_in")
        small = _matmul(h, mix_in_small, l, F32, 1024, SMALL_W, "mix_in_small")
        y_gla = _gla(proj, small, w_gk_pad[l], gla_b_gk[l], gla_norm_g[l])
        y_ssd = _ssd(proj, small, expand, ssm_conv_w[l], ssm_conv_b[l], ssm_dt_bias[l], ssm_a_log[l],
                     ssm_d[l], ssm_norm_g[l])
        y_ret = _retention(proj, cos_t, sin_t, ret_tabs, ret_norm_g[l])
        mixed = _merge(proj, (y_gla, y_ssd, y_ret), gate_up, gate_b, mix_out, l)
        xr, h = _post(mixed, xr, ng[3], 1.0, ng[4])
        q = _matmul_wcast(h, xattn_w_q, l, BF16, 1024, 512, "xattn_q")
        mem_n = _norm_cast(memr, mem_norm_g[l])
        kv = _matmul_wcast(mem_n, xattn_w_kv, l, BF16, 256, 512, "xattn_kv")
        o = _xattn(q, kv)
        xa = _matmul(o, xa_out, l, BF16, 1024, 1024, "xattn_o")
        xr, h = _post(xa, xr, ng[5], 1.0, ng[6])
        a = _swiglu_in(h, ffn2_w_in, l)
        y = _matmul(a, ffn2_out, l, BF16, 1024, 1024, "ffn_out")
        g_next = norm_g[l + 1, 0] if l + 1 < nl else None
        xr, h = _post(y, xr, ng[7], 0.5, g_next)
    return xr.reshape(b, s, d)
```

```python
import functools

import numpy as np
import jax
import jax.numpy as jnp
from jax import lax
from jax.experimental import pallas as pl
from jax.experimental.pallas import tpu as pltpu

F32 = jnp.float32
BF16 = jnp.bfloat16

D_MODEL = 4096
DEPTH = 4
CHUNK = 64
D_FF = 4096
EPS = 1e-6
GLA_HEADS, GLA_DK, GLA_DV, GLA_RANK, GLA_TAU = 4, 256, 512, 16, 16.0
GLA_QK, GLA_V = GLA_HEADS * GLA_DK, GLA_HEADS * GLA_DV
SSM_HEADS, SSM_HEADDIM, SSM_GROUPS, SSM_STATE, SSM_CONV = 32, 64, 4, 128, 4
SSM_INNER = SSM_HEADS * SSM_HEADDIM
SSM_BC = SSM_GROUPS * SSM_STATE
SSM_CONV_DIM = SSM_INNER + 2 * SSM_BC
SSM_HPG = SSM_HEADS // SSM_GROUPS
SSM_GW = SSM_INNER // SSM_GROUPS
RET_HEADS, RET_DK, RET_DV = 8, 128, 256
RET_QK, RET_V = RET_HEADS * RET_DK, RET_HEADS * RET_DV
ROPE_BASE = 10000.0
XA_HEADS, XA_DIM = 4, 256
XA_W = XA_HEADS * XA_DIM
N_BRANCH, BRANCH_W, GATE_RANK = 3, 2048, 512

_IN_WIDTHS = (GLA_QK, GLA_QK, GLA_V, GLA_V, GLA_RANK, SSM_INNER, SSM_CONV_DIM, SSM_HEADS,
              RET_QK, RET_QK, RET_V, RET_V, GATE_RANK)
_IN_OFF = np.concatenate([[0], np.cumsum(_IN_WIDTHS)]).tolist()
IN_COLS = _IN_OFF[-1]
P_GLA_Q, P_GLA_K, P_GLA_V, P_GLA_R = 0, 1024, 2048, 4096
P_SSM_Z, P_SSM_X, P_SSM_B, P_SSM_C = 6144, 8192, 10240, 10752
P_RET_Q, P_RET_K, P_RET_V, P_RET_G = 11264, 12288, 13312, 15360
P_GATE = 17408
P_COLS = 17920
SMALL_W = 128
SMALL_DT0 = GLA_RANK

V7X_VMEM_LIMIT = 60 * 1024 * 1024

RET_BLOCK = 512
MIX_ROWS = 512
CUM_BLOCK = 256


def _cparams(*sem):
    return pltpu.CompilerParams(dimension_semantics=sem, vmem_limit_bytes=V7X_VMEM_LIMIT)


def _silu(x):
    return x * jax.nn.sigmoid(x)


def _dot(a, b):
    return jnp.dot(a, b, preferred_element_type=F32)


def _dot_nt(a, b):
    return lax.dot_general(a, b, (((1,), (1,)), ((), ())), preferred_element_type=F32)


def _dot_tn(a, b):
    return lax.dot_general(a, b, (((0,), (0,)), ((), ())), preferred_element_type=F32)


def _split3(x):
    hi = x.astype(BF16)
    r1 = x - hi.astype(F32)
    mid = r1.astype(BF16)
    lo = (r1 - mid.astype(F32)).astype(BF16)
    return hi, mid, lo


def _sel_dot(sel, x):
    hi, mid, lo = _split3(x)
    return _dot(sel, hi) + _dot(sel, mid) + _dot(sel, lo)


def _chunk_tril(n):
    ri = lax.broadcasted_iota(jnp.int32, (n, n), 0)
    ci = lax.broadcasted_iota(jnp.int32, (n, n), 1)
    same = jnp.right_shift(ri, 6) == jnp.right_shift(ci, 6)
    return jnp.where(same & (ri >= ci), 1.0, 0.0).astype(BF16)


def _cast_kernel(w_ref, o_ref):
    o_ref[...] = w_ref[...].astype(o_ref.dtype)


def _cast_bf16(w, rows):
    r, c = w.shape
    return pl.pallas_call(
        _cast_kernel,
        out_shape=jax.ShapeDtypeStruct((r, c), BF16),
        grid=(r // rows,),
        in_specs=[pl.BlockSpec((rows, c), lambda i: (i, 0))],
        out_specs=pl.BlockSpec((rows, c), lambda i: (i, 0)),
        compiler_params=_cparams("parallel"),
        name="cast_bf16",
    )(w)


def _cast_gate_kernel(w_ref, o_ref):
    d = w_ref.shape[2]
    for n in range(N_BRANCH):
        o_ref[:, n * d:(n + 1) * d] = w_ref[:, n, :].astype(o_ref.dtype)


def _cast_gate_w_up(w, rows=64):
    nl, r, nbr, d = w.shape
    return pl.pallas_call(
        _cast_gate_kernel,
        out_shape=jax.ShapeDtypeStruct((nl, r, nbr * d), BF16),
        grid=(nl, r // rows),
        in_specs=[pl.BlockSpec((None, rows, nbr, d), lambda l, i: (l, i, 0, 0))],
        out_specs=pl.BlockSpec((None, rows, nbr * d), lambda l, i: (l, i, 0)),
        compiler_params=_cparams("parallel", "parallel"),
        name="cast_gate_w_up",
    )(w)


def _repack_kernel(w_ref, big_ref, small_ref):
    o = _IN_OFF
    big_ref[:, 0:P_SSM_Z] = w_ref[:, o[0]:o[4]].astype(BF16)
    big_ref[:, P_SSM_Z:P_RET_Q] = w_ref[:, o[5]:o[7]].astype(BF16)
    big_ref[:, P_RET_Q:P_COLS] = w_ref[:, o[8]:o[13]].astype(BF16)
    small_ref[...] = jnp.zeros(small_ref.shape, BF16)
    small_ref[:, 0:GLA_RANK] = w_ref[:, o[4]:o[5]].astype(BF16)
    small_ref[:, GLA_RANK:GLA_RANK + SSM_HEADS] = w_ref[:, o[7]:o[8]].astype(BF16)


def _repack_mix_w_in(w, rows=128):
    nl, d, c = w.shape
    return pl.pallas_call(
        _repack_kernel,
        out_shape=[jax.ShapeDtypeStruct((nl, d, P_COLS), BF16), jax.ShapeDtypeStruct((nl, d, SMALL_W), BF16)],
        grid=(nl, d // rows),
        in_specs=[pl.BlockSpec((None, rows, c), lambda l, i: (l, i, 0))],
        out_specs=[pl.BlockSpec((None, rows, P_COLS), lambda l, i: (l, i, 0)),
                   pl.BlockSpec((None, rows, SMALL_W), lambda l, i: (l, i, 0))],
        compiler_params=_cparams("parallel", "parallel"),
        name="repack_mix_w_in",
    )(w)


def _norm_cast_kernel(x_ref, g_ref, o_ref):
    x = x_ref[...]
    r = lax.rsqrt(jnp.mean(x * x, axis=-1, keepdims=True) + EPS)
    o_ref[...] = (x * r * g_ref[...]).astype(o_ref.dtype)


def _norm_cast(x, g, rows=256):
    m, d = x.shape
    rows = min(rows, m)
    return pl.pallas_call(
        _norm_cast_kernel,
        out_shape=jax.ShapeDtypeStruct((m, d), BF16),
        grid=(m // rows,),
        in_specs=[pl.BlockSpec((rows, d), lambda i: (i, 0)), pl.BlockSpec((1, d), lambda i: (0, 0))],
        out_specs=pl.BlockSpec((rows, d), lambda i: (i, 0)),
        compiler_params=_cparams("parallel"),
        name="norm_cast",
    )(x, g.reshape(1, d))


def _post_kernel(scale, emit_h, y_ref, x_ref, gp_ref, gn_ref, xo_ref, *maybe_h):
    y = y_ref[...].astype(F32)
    r = lax.rsqrt(jnp.mean(y * y, axis=-1, keepdims=True) + EPS)
    xn = x_ref[...] + scale * (y * r * gp_ref[...])
    xo_ref[...] = xn
    if emit_h:
        r2 = lax.rsqrt(jnp.mean(xn * xn, axis=-1, keepdims=True) + EPS)
        maybe_h[0][...] = (xn * r2 * gn_ref[...]).astype(BF16)


def _post(y, x, g_post, scale, g_next, rows=256):
    m, d = x.shape
    emit_h = g_next is not None
    gn = g_next if emit_h else g_post
    row_spec = pl.BlockSpec((rows, d), lambda i: (i, 0))
    vec_spec = pl.BlockSpec((1, d), lambda i: (0, 0))
    out_shape = [jax.ShapeDtypeStruct((m, d), F32)]
    out_specs = [row_spec]
    if emit_h:
        out_shape.append(jax.ShapeDtypeStruct((m, d), BF16))
        out_specs.append(row_spec)
    res = pl.pallas_call(
        functools.partial(_post_kernel, scale, emit_h),
        out_shape=out_shape,
        grid=(m // rows,),
        in_specs=[row_spec, row_spec, vec_spec, vec_spec],
        out_specs=out_specs,
        compiler_params=_cparams("parallel"),
        name="post_norm",
    )(y, x, g_post.reshape(1, d), gn.reshape(1, d))
    return (res[0], res[1]) if emit_h else (res[0], None)


ROW_SCALE_W = 128


def _mm_kernel(scaled, x_ref, *refs):
    w_ref, o_ref = refs[-2:]
    r = _dot(x_ref[...], w_ref[...])
    if scaled:
        r = r * refs[0][:, 0:1]
    o_ref[...] = r.astype(o_ref.dtype)


def _matmul(act, w, layer, out_dtype, bm, bn, name):
    x, rs = act
    m, k = x.shape
    n = w.shape[2]
    bm, bn = min(bm, m), min(bn, n)
    scaled = rs is not None
    in_specs = [pl.BlockSpec((bm, k), lambda i, j: (i, 0))]
    if scaled:
        in_specs.append(pl.BlockSpec((bm, ROW_SCALE_W), lambda i, j: (i, 0)))
    in_specs.append(pl.BlockSpec((None, k, bn), lambda i, j: (layer, 0, j)))
    return pl.pallas_call(
        functools.partial(_mm_kernel, scaled),
        out_shape=jax.ShapeDtypeStruct((m, n), out_dtype),
        grid=(m // bm, n // bn),
        in_specs=in_specs,
        out_specs=pl.BlockSpec((bm, bn), lambda i, j: (i, j)),
        compiler_params=_cparams("parallel", "parallel"),
        name=name,
    )(*((x, rs, w) if scaled else (x, w)))


def _mm_wcast_kernel(scaled, x_ref, *refs):
    w_ref, o_ref, w_scr = refs[-3:]

    @pl.when(pl.program_id(1) == 0)
    def _():
        w_scr[...] = w_ref[...].astype(BF16)

    r = _dot(x_ref[...], w_scr[...])
    if scaled:
        r = r * refs[0][:, 0:1]
    o_ref[...] = r.astype(o_ref.dtype)


def _matmul_wcast(act, w, layer, out_dtype, bm, bn, name):
    x, rs = act
    m, k = x.shape
    n = w.shape[2]
    bm, bn = min(bm, m), min(bn, n)
    scaled = rs is not None
    in_specs = [pl.BlockSpec((bm, k), lambda j, i: (i, 0))]
    if scaled:
        in_specs.append(pl.BlockSpec((bm, ROW_SCALE_W), lambda j, i: (i, 0)))
    in_specs.append(pl.BlockSpec((None, k, bn), lambda j, i: (layer, 0, j)))
    return pl.pallas_call(
        functools.partial(_mm_wcast_kernel, scaled),
        out_shape=jax.ShapeDtypeStruct((m, n), out_dtype),
        grid=(n // bn, m // bm),
        in_specs=in_specs,
        out_specs=pl.BlockSpec((bm, bn), lambda j, i: (i, j)),
        scratch_shapes=[pltpu.VMEM((k, bn), BF16)],
        compiler_params=_cparams("arbitrary", "arbitrary"),
        name=name,
    )(*((x, rs, w) if scaled else (x, w)))


def _swiglu_kernel(bn, scaled, x_ref, *refs):
    wg_ref, wu_ref, o_ref, w_scr = refs[-4:]

    @pl.when(pl.program_id(1) == 0)
    def _():
        w_scr[:, 0:bn] = wg_ref[...].astype(BF16)
        w_scr[:, bn:2 * bn] = wu_ref[...].astype(BF16)

    r = _dot(x_ref[...], w_scr[...])
    if scaled:
        r = r * refs[0][:, 0:1]
    o_ref[...] = (_silu(r[:, 0:bn]) * r[:, bn:2 * bn]).astype(o_ref.dtype)


def _swiglu_in(act, w, layer, bm=1024, bn=256):
    x, rs = act
    m, k = x.shape
    f = w.shape[2] // 2
    nb = f // bn
    scaled = rs is not None
    in_specs = [pl.BlockSpec((bm, k), lambda j, i: (i, 0))]
    if scaled:
        in_specs.append(pl.BlockSpec((bm, ROW_SCALE_W), lambda j, i: (i, 0)))
    in_specs += [pl.BlockSpec((None, k, bn), lambda j, i: (layer, 0, j)),
                 pl.BlockSpec((None, k, bn), lambda j, i: (layer, 0, j + nb))]
    return pl.pallas_call(
        functools.partial(_swiglu_kernel, bn, scaled),
        out_shape=jax.ShapeDtypeStruct((m, f), BF16),
        grid=(nb, m // bm),
        in_specs=in_specs,
        out_specs=pl.BlockSpec((bm, bn), lambda j, i: (i, j)),
        scratch_shapes=[pltpu.VMEM((k, 2 * bn), BF16)],
        compiler_params=_cparams("arbitrary", "arbitrary"),
        name="swiglu_in",
    )(*((x, rs, w, w) if scaled else (x, w, w)))


def _out_proj_kernel(nm, scale, emit_h, lhs_ref, w_ref, x_ref, gp_ref, gn_ref, xo_ref, *rest):
    if emit_h:
        ho_ref, rs_ref, y_scr, ssq_scr, ssq2_scr = rest
    else:
        y_scr, ssq_scr = rest
    i, n = pl.program_id(0), pl.program_id(1)
    cur = lax.rem(i, 2)
    prev = 1 - cur
    inv_d = 1.0 / (y_scr.shape[1] * y_scr.shape[3])

    def matmul_part():
        yb = _dot(lhs_ref[...], w_ref[...])
        y_scr[cur, n] = yb.astype(BF16)
        part = jnp.sum(yb * yb, axis=-1, keepdims=True)
        ssq_scr[cur] = jnp.where(n == 0, part, ssq_scr[cur] + part)

    def residual_part():
        r = lax.rsqrt(ssq_scr[prev] * inv_d + EPS)
        xn = x_ref[...] + scale * (y_scr[prev, n].astype(F32) * r * gp_ref[...])
        xo_ref[...] = xn
        if emit_h:
            ho_ref[...] = (xn * gn_ref[...]).astype(BF16)
            part = jnp.sum(xn * xn, axis=-1, keepdims=True)
            acc = jnp.where(n == 0, part, ssq2_scr[...] + part)
            ssq2_scr[...] = acc
            rs_ref[...] = jnp.broadcast_to(lax.rsqrt(acc * inv_d + EPS), rs_ref.shape)

    @pl.when(i == 0)
    def _():
        matmul_part()

    @pl.when((i > 0) & (i < nm))
    def _():
        residual_part()
        matmul_part()

    @pl.when(i == nm)
    def _():
        residual_part()


def _out_proj_post(lhs, w, layer, x, g_post, scale, g_next, name, bm=1024, bn=512):
    m, k = lhs.shape
    d = w.shape[2]
    nm, nn = m // bm, d // bn
    emit_h = g_next is not None
    gn = g_next if emit_h else g_post
    res_map = lambda i, n: (jnp.maximum(i - 1, 0), jnp.where(i == 0, 0, n))
    vec_map = lambda i, n: (0, jnp.where(i == 0, 0, n))
    res_spec = pl.BlockSpec((bm, bn), res_map)
    out_shape = [jax.ShapeDtypeStruct((m, d), F32)]
    out_specs = [res_spec]
    scratch = [pltpu.VMEM((2, nn, bm, bn), BF16), pltpu.VMEM((2, bm, 1), F32)]
    if emit_h:
        out_shape += [jax.ShapeDtypeStruct((m, d), BF16), jax.ShapeDtypeStruct((m, ROW_SCALE_W), F32)]
        out_specs += [res_spec, pl.BlockSpec((bm, ROW_SCALE_W), lambda i, n: (jnp.maximum(i - 1, 0), 0))]
        scratch.append(pltpu.VMEM((bm, 1), F32))
    res = pl.pallas_call(
        functools.partial(_out_proj_kernel, nm, scale, emit_h),
        out_shape=out_shape,
        grid=(nm + 1, nn),
        in_specs=[pl.BlockSpec((bm, k), lambda i, n: (jnp.minimum(i, nm - 1), 0)),
                  pl.BlockSpec((None, k, bn), lambda i, n: (layer, 0, n)),
                  res_spec,
                  pl.BlockSpec((1, bn), vec_map),
                  pl.BlockSpec((1, bn), vec_map)],
        out_specs=out_specs,
        scratch_shapes=scratch,
        compiler_params=_cparams("arbitrary", "arbitrary"),
        name=name,
    )(lhs, w, x, g_post.reshape(1, d), gn.reshape(1, d))
    return (res[0], (res[1], res[2])) if emit_h else (res[0], None)


def _gla_kernel(q_ref, k_ref, v_ref, r_ref, sm_ref, wgk_ref, bgk_ref, ng_ref, o_ref, st_ref, cum_ref):
    @pl.when(pl.program_id(1) == 0)
    def _():
        st_ref[...] = jnp.zeros_like(st_ref)

    rows = q_ref.shape[0]
    a_hi, a_mid, _ = _split3(sm_ref[...])
    w_hi, w_mid, _ = _split3(wgk_ref[...])
    z = _dot(a_hi, w_hi) + _dot(a_hi, w_mid) + _dot(a_mid, w_hi) + bgk_ref[...]
    logf = (jnp.minimum(z, 0.0) - jnp.log1p(jnp.exp(-jnp.abs(z)))) * (1.0 / GLA_TAU)
    tri = _chunk_tril(CUM_BLOCK)
    for b in range(rows // CUM_BLOCK):
        sl = slice(b * CUM_BLOCK, (b + 1) * CUM_BLOCK)
        cum_ref[sl, :] = _sel_dot(tri, logf[sl])

    ri = lax.broadcasted_iota(jnp.int32, (CHUNK, CHUNK), 0)
    ci = lax.broadcasted_iota(jnp.int32, (CHUNK, CHUNK), 1)
    tril = ri >= ci
    for c in range(rows // CHUNK):
        sl = pl.ds(c * CHUNK, CHUNK)
        cum = cum_ref[sl, :]
        cum_end = cum[CHUNK - 1:CHUNK, :]
        up, down = jnp.exp(cum), jnp.exp(-cum)
        q = q_ref[sl, :].astype(F32) * (GLA_DK ** -0.5)
        k = k_ref[sl, :].astype(F32)
        v = v_ref[sl, :]
        q_dec = (q * up).astype(BF16)
        a_past = _dot_nt(q_dec, (k * down).astype(BF16))
        a_future = _dot_nt((q * down).astype(BF16), (k * up).astype(BF16))
        attn = jnp.where(tril, a_past, a_future).astype(BF16)
        state_t = st_ref[...]
        o = _dot(attn, v) + _dot_nt(q_dec, state_t.astype(BF16))
        contrib_t = _dot_tn(v, (k * jnp.exp(cum_end - cum)).astype(BF16))
        st_ref[...] = state_t * jnp.exp(cum_end) + contrib_t
        o = o * lax.rsqrt(jnp.mean(o * o, axis=-1, keepdims=True) + EPS) * ng_ref[...]
        o_ref[sl, :] = (o * _silu(r_ref[sl, :].astype(F32))).astype(o_ref.dtype)


def _gla(proj, small, w_gk_pad, b_gk, norm_g):
    s = proj.shape[0]
    rows = min(MIX_ROWS, s)
    qb, vb = P_GLA_Q // GLA_DK, P_GLA_V // GLA_DV
    kb, rb = P_GLA_K // GLA_DK, P_GLA_R // GLA_DV
    return pl.pallas_call(
        _gla_kernel,
        out_shape=jax.ShapeDtypeStruct((s, GLA_V), BF16),
        grid=(GLA_HEADS, s // rows),
        in_specs=[pl.BlockSpec((rows, GLA_DK), lambda h, i: (i, qb + h)),
                  pl.BlockSpec((rows, GLA_DK), lambda h, i: (i, kb + h)),
                  pl.BlockSpec((rows, GLA_DV), lambda h, i: (i, vb + h)),
                  pl.BlockSpec((rows, GLA_DV), lambda h, i: (i, rb + h)),
                  pl.BlockSpec((rows, SMALL_W), lambda h, i: (i, 0)),
                  pl.BlockSpec((SMALL_W, GLA_DK), lambda h, i: (0, h)),
                  pl.BlockSpec((1, GLA_DK), lambda h, i: (0, h)),
                  pl.BlockSpec((1, GLA_DV), lambda h, i: (0, h))],
        out_specs=pl.BlockSpec((rows, GLA_DV), lambda h, i: (i, h)),
        scratch_shapes=[pltpu.VMEM((GLA_DV, GLA_DK), F32), pltpu.VMEM((rows, GLA_DK), F32)],
        compiler_params=_cparams("parallel", "arbitrary"),
        name="gla",
    )(proj, proj, proj, proj, small, w_gk_pad, b_gk.reshape(1, GLA_QK), norm_g.reshape(1, GLA_V))


def _causal_conv_silu(xp_ref, x_ref, w_ref, b_ref, first):
    rows = x_ref.shape[0]

    @pl.when(first)
    def _():
        xp_ref[pl.ds(0, 8), :] = jnp.zeros((8, xp_ref.shape[1]), F32)

    xp_ref[pl.ds(8, rows), :] = x_ref[...].astype(F32)
    acc = b_ref[...]
    for tap in range(SSM_CONV):
        acc = acc + xp_ref[pl.ds(8 - (SSM_CONV - 1) + tap, rows), :] * w_ref[tap:tap + 1, :]
    xp_ref[pl.ds(0, 8), :] = xp_ref[pl.ds(rows, 8), :]
    return _silu(acc)


def _ssd_kernel(z_ref, xs_ref, b_ref, c_ref, sm_ref, e_ref, cwx_ref, cwb_ref, cwc_ref,
                cbx_ref, cbb_ref, cbc_ref, dtb_ref, alog_ref, dsk_ref, ng_ref,
                o_ref, s_ref, xpx_ref, xpb_ref, xpc_ref, xs_scr, b_scr, c_scr, dt_scr, cum_scr):
    first = pl.program_id(1) == 0

    @pl.when(first)
    def _():
        s_ref[...] = jnp.zeros_like(s_ref)

    rows = xs_ref.shape[0]
    xs_scr[...] = _causal_conv_silu(xpx_ref, xs_ref, cwx_ref, cbx_ref, first)
    b_scr[...] = _causal_conv_silu(xpb_ref, b_ref, cwb_ref, cbb_ref, first).astype(BF16)
    c_scr[...] = _causal_conv_silu(xpc_ref, c_ref, cwc_ref, cbc_ref, first).astype(BF16)

    dt_in = _sel_dot_rhs(sm_ref[...], e_ref[0]) + dtb_ref[...]
    dt = jnp.maximum(dt_in, 0.0) + jnp.log1p(jnp.exp(-jnp.abs(dt_in)))
    dt_scr[...] = dt
    da = dt * (-jnp.exp(alog_ref[...]))
    tri = _chunk_tril(CUM_BLOCK)
    for b in range(rows // CUM_BLOCK):
        sl = slice(b * CUM_BLOCK, (b + 1) * CUM_BLOCK)
        cum_scr[sl, :] = _sel_dot(tri, da[sl])

    gw = SSM_GW
    rw = lax.broadcasted_iota(jnp.int32, (CHUNK, gw), 0)
    cw = lax.broadcasted_iota(jnp.int32, (CHUNK, gw), 1)
    diag_tile = (jnp.bitwise_and(cw, SSM_HEADDIM - 1) == rw).astype(F32)
    half = gw // 2
    rb = lax.broadcasted_iota(jnp.int32, (half, half), 0)
    cb_ = lax.broadcasted_iota(jnp.int32, (half, half), 1)
    head_mask = jnp.right_shift(rb, 6) == jnp.right_shift(cb_, 6)
    reps = half // CHUNK
    for c in range(rows // CHUNK):
        sl = pl.ds(c * CHUNK, CHUNK)
        cum = cum_scr[sl, :]
        cum_row = jnp.sum(cum * diag_tile, axis=0, keepdims=True)
        seg = jnp.exp(-jnp.abs(cum - cum_row))
        xs = xs_scr[sl, :]
        bm = b_scr[sl, :]
        cm = c_scr[sl, :]
        xdt = xs * dt_scr[sl, :]
        cbt = _dot_nt(cm, jnp.concatenate([bm] * SSM_HPG, axis=0))
        m_all = (cbt * seg).astype(BF16)
        intra = []
        for hlf in range(2):
            xh = xdt[:, hlf * half:(hlf + 1) * half]
            bd = jnp.where(head_mask, jnp.concatenate([xh] * reps, axis=0), 0.0).astype(BF16)
            intra.append(_dot(m_all[:, hlf * half:(hlf + 1) * half], bd))
        intra = jnp.concatenate(intra, axis=1)
        cum_end = cum[CHUNK - 1:CHUNK, :]
        state = s_ref[...]
        inter = _dot(cm, state.astype(BF16)) * jnp.exp(cum)
        contrib = _dot_tn(bm, (xdt * jnp.exp(cum_end - cum)).astype(BF16))
        s_ref[...] = state * jnp.exp(cum_end) + contrib
        y = intra + inter + xs * dsk_ref[...]
        y = y * _silu(z_ref[sl, :].astype(F32))
        y = y * lax.rsqrt(jnp.mean(y * y, axis=-1, keepdims=True) + EPS) * ng_ref[...]
        o_ref[sl, :] = y.astype(o_ref.dtype)


def _sel_dot_rhs(x, sel):
    hi, mid, lo = _split3(x)
    return _dot(hi, sel) + _dot(mid, sel) + _dot(lo, sel)


def _ssd(proj, small, expand, conv_w, conv_b, dt_bias, a_log, d_skip, norm_g):
    s = proj.shape[0]
    rows = min(MIX_ROWS, s)
    gw, st = SSM_GW, SSM_STATE
    zb, xb = P_SSM_Z // gw, P_SSM_X // gw
    bb, cb = P_SSM_B // st, P_SSM_C // st
    cwb, cwc = SSM_INNER // st, (SSM_INNER + SSM_BC) // st

    def lane_expand(p):
        return jnp.repeat(p.astype(F32), SSM_HEADDIM).reshape(1, SSM_INNER)

    conv_b2 = conv_b.reshape(1, SSM_CONV_DIM)
    vec = pl.BlockSpec((1, gw), lambda g, i: (0, g))
    return pl.pallas_call(
        _ssd_kernel,
        out_shape=jax.ShapeDtypeStruct((s, SSM_INNER), BF16),
        grid=(SSM_GROUPS, s // rows),
        in_specs=[pl.BlockSpec((rows, gw), lambda g, i: (i, zb + g)),
                  pl.BlockSpec((rows, gw), lambda g, i: (i, xb + g)),
                  pl.BlockSpec((rows, st), lambda g, i: (i, bb + g)),
                  pl.BlockSpec((rows, st), lambda g, i: (i, cb + g)),
                  pl.BlockSpec((rows, SMALL_W), lambda g, i: (i, 0)),
                  pl.BlockSpec((1, SMALL_W, gw), lambda g, i: (g, 0, 0)),
                  pl.BlockSpec((SSM_CONV, gw), lambda g, i: (0, g)),
                  pl.BlockSpec((SSM_CONV, st), lambda g, i: (0, cwb + g)),
                  pl.BlockSpec((SSM_CONV, st), lambda g, i: (0, cwc + g)),
                  pl.BlockSpec((1, gw), lambda g, i: (0, g)),
                  pl.BlockSpec((1, st), lambda g, i: (0, cwb + g)),
                  pl.BlockSpec((1, st), lambda g, i: (0, cwc + g)),
                  vec, vec, vec, vec],
        out_specs=pl.BlockSpec((rows, gw), lambda g, i: (i, g)),
        scratch_shapes=[pltpu.VMEM((st, gw), F32),
                        pltpu.VMEM((rows + 8, gw), F32),
                        pltpu.VMEM((rows + 8, st), F32),
                        pltpu.VMEM((rows + 8, st), F32),
                        pltpu.VMEM((rows, gw), F32),
                        pltpu.VMEM((rows, st), BF16),
                        pltpu.VMEM((rows, st), BF16),
                        pltpu.VMEM((rows, gw), F32),
                        pltpu.VMEM((rows, gw), F32)],
        compiler_params=_cparams("parallel", "arbitrary"),
        name="ssd",
    )(proj, proj, proj, proj, small, expand, conv_w, conv_w, conv_w, conv_b2, conv_b2, conv_b2,
      lane_expand(dt_bias), lane_expand(a_log), lane_expand(d_skip), norm_g.reshape(1, SSM_INNER))


def _ssd_expand_table():
    rows = jnp.arange(SMALL_W, dtype=jnp.int32)[None, :, None]
    cols = jnp.arange(SSM_GW, dtype=jnp.int32)[None, None, :]
    grp = jnp.arange(SSM_GROUPS, dtype=jnp.int32)[:, None, None]
    return (rows == SMALL_DT0 + grp * SSM_HPG + cols // SSM_HEADDIM).astype(BF16)


def _rope_table_kernel(pos_ref, freq_ref, sign_ref, cos_ref, sin_ref):
    ang = pos_ref[...] * freq_ref[...]
    cos_ref[...] = jnp.cos(ang)
    sin_ref[...] = jnp.sin(ang) * sign_ref[...]


def _rope_tables(positions):
    s = positions.shape[-1]
    half = RET_DK // 2
    inv_freq = ROPE_BASE ** (-jnp.arange(half, dtype=F32) / half)
    freq = jnp.concatenate([inv_freq, inv_freq]).reshape(1, RET_DK)
    sign = jnp.concatenate([-jnp.ones((half,), F32), jnp.ones((half,), F32)]).reshape(1, RET_DK)
    pos = positions.astype(F32).reshape(s, 1)
    rows = min(1024, s)
    tab = jax.ShapeDtypeStruct((s, RET_DK), F32)
    return pl.pallas_call(
        _rope_table_kernel,
        out_shape=[tab, tab],
        grid=(s // rows,),
        in_specs=[pl.BlockSpec((rows, 1), lambda i: (i, 0)),
                  pl.BlockSpec((1, RET_DK), lambda i: (0, 0)),
                  pl.BlockSpec((1, RET_DK), lambda i: (0, 0))],
        out_specs=[pl.BlockSpec((rows, RET_DK), lambda i: (i, 0))] * 2,
        compiler_params=_cparams("parallel"),
        name="rope_tables",
    )(pos, freq, sign)


def _ret_kernel(q_ref, k_ref, v_ref, g_ref, cos_ref, sin_ref, dmat_ref, qdec_ref, kdec_ref, sdec_ref,
                ng_ref, o_ref, s_ref):
    @pl.when(pl.program_id(1) == 0)
    def _():
        s_ref[...] = jnp.zeros_like(s_ref)

    cos, sin = cos_ref[...], sin_ref[...]

    def rot(t):
        t = t.astype(F32)
        return t * cos + pltpu.roll(t, RET_DK // 2, axis=1) * sin

    q = rot(q_ref[...])
    k = rot(k_ref[...]) * (RET_DK ** -0.5)
    v = v_ref[...]
    scores = (_dot_nt(q.astype(BF16), k.astype(BF16)) * dmat_ref[0]).astype(BF16)
    state = s_ref[...]
    o = _dot(scores, v) + _dot((q * qdec_ref[0]).astype(BF16), state.astype(BF16))
    s_ref[...] = state * sdec_ref[0] + _dot_tn((k * kdec_ref[0]).astype(BF16), v)
    o = o * lax.rsqrt(jnp.mean(o * o, axis=-1, keepdims=True) + EPS) * ng_ref[...]
    o_ref[...] = (o * _silu(g_ref[...].astype(F32))).astype(o_ref.dtype)


def _ret_tables(rows):
    log_gamma = jnp.log1p(-jnp.exp2(-5.0 - jnp.arange(RET_HEADS, dtype=F32)))[:, None, None]
    i = jnp.arange(rows, dtype=jnp.int32)
    dist = (i[:, None] - i[None, :]).astype(F32)[None]
    ch_i, ch_j = (i // CHUNK)[:, None], (i // CHUNK)[None, :]
    dmat = jnp.where((ch_i == ch_j)[None], jnp.exp(log_gamma * jnp.abs(dist)),
                     jnp.where((ch_j < ch_i)[None], jnp.exp(log_gamma * dist), 0.0))
    fi = i.astype(F32)[None, :, None]
    qdec = jnp.broadcast_to(jnp.exp(log_gamma * (fi + 1.0)), (RET_HEADS, rows, RET_DK))
    kdec = jnp.broadcast_to(jnp.exp(log_gamma * (rows - 1.0 - fi)), (RET_HEADS, rows, RET_DK))
    sdec = jnp.broadcast_to(jnp.exp(log_gamma * float(rows)), (RET_HEADS, 1, RET_DV))
    return dmat, qdec, kdec, sdec


def _retention(proj, cos_t, sin_t, tables, norm_g):
    s = proj.shape[0]
    rows = min(RET_BLOCK, s)
    dmat, qdec, kdec, sdec = tables
    qb, kb = P_RET_Q // RET_DK, P_RET_K // RET_DK
    vb, gb = P_RET_V // RET_DV, P_RET_G // RET_DV
    return pl.pallas_call(
        _ret_kernel,
        out_shape=jax.ShapeDtypeStruct((s, RET_V), BF16),
        grid=(RET_HEADS, s // rows),
        in_specs=[pl.BlockSpec((rows, RET_DK), lambda h, i: (i, qb + h)),
                  pl.BlockSpec((rows, RET_DK), lambda h, i: (i, kb + h)),
                  pl.BlockSpec((rows, RET_DV), lambda h, i: (i, vb + h)),
                  pl.BlockSpec((rows, RET_DV), lambda h, i: (i, gb + h)),
                  pl.BlockSpec((rows, RET_DK), lambda h, i: (i, 0)),
                  pl.BlockSpec((rows, RET_DK), lambda h, i: (i, 0)),
                  pl.BlockSpec((1, rows, rows), lambda h, i: (h, 0, 0)),
                  pl.BlockSpec((1, rows, RET_DK), lambda h, i: (h, 0, 0)),
                  pl.BlockSpec((1, rows, RET_DK), lambda h, i: (h, 0, 0)),
                  pl.BlockSpec((1, 1, RET_DV), lambda h, i: (h, 0, 0)),
                  pl.BlockSpec((1, RET_DV), lambda h, i: (0, h))],
        out_specs=pl.BlockSpec((rows, RET_DV), lambda h, i: (i, h)),
        scratch_shapes=[pltpu.VMEM((RET_DK, RET_DV), F32)],
        compiler_params=_cparams("parallel", "arbitrary"),
        name="retention",
    )(proj, proj, proj, proj, cos_t, sin_t, dmat, qdec, kdec, sdec, norm_g.reshape(1, RET_V))


def _merge_kernel(gl_ref, y0_ref, y1_ref, y2_ref, gu0_ref, gu1_ref, gu2_ref, gb0_ref, gb1_ref, gb2_ref,
                  wo0_ref, wo1_ref, wo2_ref, o_ref):
    gl = gl_ref[...]
    acc = None
    for y_ref, gu_ref, gb_ref, wo_ref in ((y0_ref, gu0_ref, gb0_ref, wo0_ref),
                                          (y1_ref, gu1_ref, gb1_ref, wo1_ref),
                                          (y2_ref, gu2_ref, gb2_ref, wo2_ref)):
        gate = jax.nn.sigmoid(_dot(gl, gu_ref[...]) + gb_ref[...])
        term = gate * _dot(y_ref[...], wo_ref[...])
        acc = term if acc is None else acc + term
    o_ref[...] = acc.astype(o_ref.dtype)


def _merge(proj, ys, gate_w2, gate_b, w_out, layer, bm=1024, bn=512):
    s = proj.shape[0]
    bm = min(bm, s)
    d = D_MODEL
    nb = d // bn
    gate_b4 = gate_b.reshape(DEPTH, N_BRANCH, 1, d)
    y_spec = pl.BlockSpec((bm, BRANCH_W), lambda i, j: (i, 0))
    in_specs = [pl.BlockSpec((bm, GATE_RANK), lambda i, j: (i, P_GATE // GATE_RANK)), y_spec, y_spec, y_spec]
    in_specs += [pl.BlockSpec((None, GATE_RANK, bn), functools.partial(lambda n, i, j: (layer, 0, n * nb + j), n))
                 for n in range(N_BRANCH)]
    in_specs += [pl.BlockSpec((None, None, 1, bn), functools.partial(lambda n, i, j: (layer, n, 0, j), n))
                 for n in range(N_BRANCH)]
    in_specs += [pl.BlockSpec((None, BRANCH_W, bn),
                              functools.partial(lambda n, i, j: (layer * N_BRANCH + n, 0, j), n))
                 for n in range(N_BRANCH)]
    return pl.pallas_call(
        _merge_kernel,
        out_shape=jax.ShapeDtypeStruct((s, d), BF16),
        grid=(s // bm, nb),
        in_specs=in_specs,
        out_specs=pl.BlockSpec((bm, bn), lambda i, j: (i, j)),
        compiler_params=_cparams("parallel", "parallel"),
        name="gated_merge",
    )(proj, ys[0], ys[1], ys[2], gate_w2, gate_w2, gate_w2, gate_b4, gate_b4, gate_b4, w_out, w_out, w_out)


def _xattn_kernel(q_ref, kv_ref, o_ref):
    for h in range(XA_HEADS):
        lo, hi = h * XA_DIM, (h + 1) * XA_DIM
        s = _dot_nt(q_ref[:, lo:hi], kv_ref[:, lo:hi]) * (XA_DIM ** -0.5)
        p = jnp.exp(s - jnp.max(s, axis=-1, keepdims=True))
        p = p / jnp.sum(p, axis=-1, keepdims=True)
        o_ref[:, lo:hi] = _dot(p.astype(BF16), kv_ref[:, XA_W + lo:XA_W + hi]).astype(o_ref.dtype)


def _xattn(q, kv, rows=512):
    s = q.shape[0]
    rows = min(rows, s)
    n_mem = kv.shape[0]
    return pl.pallas_call(
        _xattn_kernel,
        out_shape=jax.ShapeDtypeStruct((s, XA_W), BF16),
        grid=(s // rows,),
        in_specs=[pl.BlockSpec((rows, XA_W), lambda i: (i, 0)),
                  pl.BlockSpec((n_mem, 2 * XA_W), lambda i: (0, 0))],
        out_specs=pl.BlockSpec((rows, XA_W), lambda i: (i, 0)),
        compiler_params=_cparams("parallel"),
        name="xattn",
    )(q, kv)


def kernel(x, mem, positions, norm_g, mem_norm_g, ffn1_w_in, ffn1_w_out, mix_w_in, gla_w_gk, gla_b_gk,
           gla_norm_g, ssm_conv_w, ssm_conv_b, ssm_dt_bias, ssm_a_log, ssm_d, ssm_norm_g, ret_norm_g,
           gate_w_up, gate_b, mix_w_out, xattn_w_q, xattn_w_kv, xattn_w_o, ffn2_w_in, ffn2_w_out):
    b, s, d = x.shape
    assert b == 1 and d == D_MODEL and s % max(RET_BLOCK, MIX_ROWS, 1024) == 0
    nl = DEPTH
    xr = x.reshape(s, d)
    memr = mem.reshape(mem.shape[1], d)
    cos_t, sin_t = _rope_tables(positions)
    ret_tabs = _ret_tables(min(RET_BLOCK, s))
    expand = _ssd_expand_table()

    ffn1_out = _cast_bf16(ffn1_w_out.reshape(nl * D_FF, d), 512).reshape(nl, D_FF, d)
    ffn2_out = _cast_bf16(ffn2_w_out.reshape(nl * D_FF, d), 512).reshape(nl, D_FF, d)
    xa_out = _cast_bf16(xattn_w_o.reshape(nl * XA_W, d), 512).reshape(nl, XA_W, d)
    mix_out = _cast_bf16(mix_w_out.reshape(nl * N_BRANCH * BRANCH_W, d), 512).reshape(nl * N_BRANCH, BRANCH_W, d)
    gate_up = _cast_gate_w_up(gate_w_up)
    mix_in_big, mix_in_small = _repack_mix_w_in(mix_w_in)
    w_gk_pad = jnp.concatenate([gla_w_gk, jnp.zeros((nl, SMALL_W - GLA_RANK, GLA_QK), F32)], axis=1)

    h = (_norm_cast(xr, norm_g[0, 0]), None)
    for l in range(nl):
        ng = norm_g[l]
        a = _swiglu_in(h, ffn1_w_in, l)
        xr, h = _out_proj_post(a, ffn1_out, l, xr, ng[1], 0.5, ng[2], "ffn_out_post")
        proj = _matmul(h, mix_in_big, l, BF16, 1024, 1280, "mix_in")
        small = _matmul(h, mix_in_small, l, F32, 1024, SMALL_W, "mix_in_small")
        y_gla = _gla(proj, small, w_gk_pad[l], gla_b_gk[l], gla_norm_g[l])
        y_ssd = _ssd(proj, small, expand, ssm_conv_w[l], ssm_conv_b[l], ssm_dt_bias[l], ssm_a_log[l],
                     ssm_d[l], ssm_norm_g[l])
        y_ret = _retention(proj, cos_t, sin_t, ret_tabs, ret_norm_g[l])
        mixed = _merge(proj, (y_gla, y_ssd, y_ret), gate_up, gate_b, mix_out, l)
        xr, hn = _post(mixed, xr, ng[3], 1.0, ng[4])
        q = _matmul_wcast((hn, None), xattn_w_q, l, BF16, 1024, 512, "xattn_q")
        mem_n = _norm_cast(memr, mem_norm_g[l])
        kv = _matmul_wcast((mem_n, None), xattn_w_kv, l, BF16, 256, 512, "xattn_kv")
        o = _xattn(q, kv)
        xr, h = _out_proj_post(o, xa_out, l, xr, ng[5], 1.0, ng[6], "xattn_out_post")
        a = _swiglu_in(h, ffn2_w_in, l)
        g_next = norm_g[l + 1, 0] if l + 1 < nl else None
        xr, h = _out_proj_post(a, ffn2_out, l, xr, ng[7], 0.5, g_next, "ffn_out_post")
    return xr.reshape(b, s, d)
```

```python
import functools

import numpy as np
import jax
import jax.numpy as jnp
from jax import lax
from jax.experimental import pallas as pl
from jax.experimental.pallas import tpu as pltpu

F32 = jnp.float32
BF16 = jnp.bfloat16

D_MODEL = 4096
DEPTH = 4
CHUNK = 64
D_FF = 4096
EPS = 1e-6
GLA_HEADS, GLA_DK, GLA_DV, GLA_RANK, GLA_TAU = 4, 256, 512, 16, 16.0
GLA_QK, GLA_V = GLA_HEADS * GLA_DK, GLA_HEADS * GLA_DV
SSM_HEADS, SSM_HEADDIM, SSM_GROUPS, SSM_STATE, SSM_CONV = 32, 64, 4, 128, 4
SSM_INNER = SSM_HEADS * SSM_HEADDIM
SSM_BC = SSM_GROUPS * SSM_STATE
SSM_CONV_DIM = SSM_INNER + 2 * SSM_BC
SSM_HPG = SSM_HEADS // SSM_GROUPS
SSM_GW = SSM_INNER // SSM_GROUPS
RET_HEADS, RET_DK, RET_DV = 8, 128, 256
RET_QK, RET_V = RET_HEADS * RET_DK, RET_HEADS * RET_DV
ROPE_BASE = 10000.0
XA_HEADS, XA_DIM = 4, 256
XA_W = XA_HEADS * XA_DIM
N_BRANCH, BRANCH_W, GATE_RANK = 3, 2048, 512

_IN_WIDTHS = (GLA_QK, GLA_QK, GLA_V, GLA_V, GLA_RANK, SSM_INNER, SSM_CONV_DIM, SSM_HEADS,
              RET_QK, RET_QK, RET_V, RET_V, GATE_RANK)
_IN_OFF = np.concatenate([[0], np.cumsum(_IN_WIDTHS)]).tolist()
IN_COLS = _IN_OFF[-1]
P_GLA_Q, P_GLA_K, P_GLA_V, P_GLA_R = 0, 1024, 2048, 4096
P_SSM_Z, P_SSM_X, P_SSM_B, P_SSM_C = 6144, 8192, 10240, 10752
P_RET_Q, P_RET_K, P_RET_V, P_RET_G = 11264, 12288, 13312, 15360
P_GATE = 17408
P_COLS = 17920
SMALL_W = 128
SMALL_DT0 = GLA_RANK

V7X_VMEM_LIMIT = 60 * 1024 * 1024

RET_BLOCK = 512
MIX_ROWS = 512
CUM_BLOCK = 256


def _cparams(*sem):
    return pltpu.CompilerParams(dimension_semantics=sem, vmem_limit_bytes=V7X_VMEM_LIMIT)


def _silu(x):
    return x * jax.nn.sigmoid(x)


def _dot(a, b):
    return jnp.dot(a, b, preferred_element_type=F32)


def _dot_nt(a, b):
    return lax.dot_general(a, b, (((1,), (1,)), ((), ())), preferred_element_type=F32)


def _dot_tn(a, b):
    return lax.dot_general(a, b, (((0,), (0,)), ((), ())), preferred_element_type=F32)


def _split3(x):
    hi = x.astype(BF16)
    r1 = x - hi.astype(F32)
    mid = r1.astype(BF16)
    lo = (r1 - mid.astype(F32)).astype(BF16)
    return hi, mid, lo


def _sel_dot(sel, x):
    hi, mid, lo = _split3(x)
    return _dot(sel, hi) + _dot(sel, mid) + _dot(sel, lo)


def _chunk_tril(n):
    ri = lax.broadcasted_iota(jnp.int32, (n, n), 0)
    ci = lax.broadcasted_iota(jnp.int32, (n, n), 1)
    same = jnp.right_shift(ri, 6) == jnp.right_shift(ci, 6)
    return jnp.where(same & (ri >= ci), 1.0, 0.0).astype(BF16)


def _cast_kernel(w_ref, o_ref):
    o_ref[...] = w_ref[...].astype(o_ref.dtype)


def _cast_bf16(w, rows):
    r, c = w.shape
    return pl.pallas_call(
        _cast_kernel,
        out_shape=jax.ShapeDtypeStruct((r, c), BF16),
        grid=(r // rows,),
        in_specs=[pl.BlockSpec((rows, c), lambda i: (i, 0))],
        out_specs=pl.BlockSpec((rows, c), lambda i: (i, 0)),
        compiler_params=_cparams("parallel"),
        name="cast_bf16",
    )(w)


REPACK_COLS = 512


def _packed_to_source_col(c):
    skipped = jnp.where(c >= P_SSM_Z, GLA_RANK, 0) + jnp.where(c >= P_RET_Q, SSM_HEADS, 0)
    return pl.multiple_of(c + skipped, GLA_RANK)


def _repack_kernel(w_ref, o_ref):
    o_ref[...] = w_ref[...].T.astype(o_ref.dtype)


def _repack_mix_w_in(w_t):
    nl, _, d = w_t.shape
    return pl.pallas_call(
        _repack_kernel,
        out_shape=jax.ShapeDtypeStruct((nl, d, P_COLS), BF16),
        grid=(nl, P_COLS // REPACK_COLS),
        in_specs=[pl.BlockSpec((pl.Squeezed(), pl.Element(REPACK_COLS), pl.Element(d)),
                               lambda l, j: (l, _packed_to_source_col(j * REPACK_COLS), 0))],
        out_specs=pl.BlockSpec((None, d, REPACK_COLS), lambda l, j: (l, 0, j)),
        compiler_params=_cparams("parallel", "parallel"),
        name="repack_mix_w_in",
    )(w_t)


def _side_w_kernel(gk_ref, dt_ref, o_ref):
    o_ref[...] = jnp.zeros(o_ref.shape, o_ref.dtype)
    o_ref[0:GLA_RANK, :] = gk_ref[...].astype(o_ref.dtype)
    o_ref[GLA_RANK:GLA_RANK + SSM_HEADS, :] = dt_ref[...].astype(o_ref.dtype)


def _side_mix_w_in(w_t):
    nl, _, d = w_t.shape
    o = _IN_OFF
    return pl.pallas_call(
        _side_w_kernel,
        out_shape=jax.ShapeDtypeStruct((nl, SMALL_W, d), BF16),
        grid=(nl,),
        in_specs=[pl.BlockSpec((pl.Squeezed(), pl.Element(GLA_RANK), pl.Element(d)), lambda l: (l, o[4], 0)),
                  pl.BlockSpec((pl.Squeezed(), pl.Element(SSM_HEADS), pl.Element(d)), lambda l: (l, o[7], 0))],
        out_specs=pl.BlockSpec((None, SMALL_W, d), lambda l: (l, 0, 0)),
        compiler_params=_cparams("parallel"),
        name="side_mix_w_in",
    )(w_t, w_t)


def _norm_cast_kernel(x_ref, g_ref, o_ref):
    x = x_ref[...]
    r = lax.rsqrt(jnp.mean(x * x, axis=-1, keepdims=True) + EPS)
    o_ref[...] = (x * r * g_ref[...]).astype(o_ref.dtype)


def _norm_cast(x, g, rows=256):
    m, d = x.shape
    rows = min(rows, m)
    return pl.pallas_call(
        _norm_cast_kernel,
        out_shape=jax.ShapeDtypeStruct((m, d), BF16),
        grid=(m // rows,),
        in_specs=[pl.BlockSpec((rows, d), lambda i: (i, 0)), pl.BlockSpec((1, d), lambda i: (0, 0))],
        out_specs=pl.BlockSpec((rows, d), lambda i: (i, 0)),
        compiler_params=_cparams("parallel"),
        name="norm_cast",
    )(x, g.reshape(1, d))


def _post_kernel(scale, emit_h, y_ref, x_ref, gp_ref, gn_ref, xo_ref, *maybe_h):
    y = y_ref[...].astype(F32)
    r = lax.rsqrt(jnp.mean(y * y, axis=-1, keepdims=True) + EPS)
    xn = x_ref[...] + scale * (y * r * gp_ref[...])
    xo_ref[...] = xn
    if emit_h:
        r2 = lax.rsqrt(jnp.mean(xn * xn, axis=-1, keepdims=True) + EPS)
        maybe_h[0][...] = (xn * r2 * gn_ref[...]).astype(BF16)


def _post(y, x, g_post, scale, g_next, rows=256):
    m, d = x.shape
    emit_h = g_next is not None
    gn = g_next if emit_h else g_post
    row_spec = pl.BlockSpec((rows, d), lambda i: (i, 0))
    vec_spec = pl.BlockSpec((1, d), lambda i: (0, 0))
    out_shape = [jax.ShapeDtypeStruct((m, d), F32)]
    out_specs = [row_spec]
    if emit_h:
        out_shape.append(jax.ShapeDtypeStruct((m, d), BF16))
        out_specs.append(row_spec)
    res = pl.pallas_call(
        functools.partial(_post_kernel, scale, emit_h),
        out_shape=out_shape,
        grid=(m // rows,),
        in_specs=[row_spec, row_spec, vec_spec, vec_spec],
        out_specs=out_specs,
        compiler_params=_cparams("parallel"),
        name="post_norm",
    )(y, x, g_post.reshape(1, d), gn.reshape(1, d))
    return (res[0], res[1]) if emit_h else (res[0], None)


ROW_SCALE_W = 128


def _mm_kernel(scaled, x_ref, *refs):
    w_ref, o_ref = refs[-2:]
    r = _dot(x_ref[...], w_ref[...])
    if scaled:
        r = r * refs[0][:, 0:1]
    o_ref[...] = r.astype(o_ref.dtype)


def _matmul(act, w, layer, out_dtype, bm, bn, name):
    x, rs = act
    m, k = x.shape
    n = w.shape[2]
    bm, bn = min(bm, m), min(bn, n)
    scaled = rs is not None
    in_specs = [pl.BlockSpec((bm, k), lambda i, j: (i, 0))]
    if scaled:
        in_specs.append(pl.BlockSpec((bm, ROW_SCALE_W), lambda i, j: (i, 0)))
    in_specs.append(pl.BlockSpec((None, k, bn), lambda i, j: (layer, 0, j)))
    return pl.pallas_call(
        functools.partial(_mm_kernel, scaled),
        out_shape=jax.ShapeDtypeStruct((m, n), out_dtype),
        grid=(m // bm, n // bn),
        in_specs=in_specs,
        out_specs=pl.BlockSpec((bm, bn), lambda i, j: (i, j)),
        compiler_params=_cparams("parallel", "parallel"),
        name=name,
    )(*((x, rs, w) if scaled else (x, w)))


def _mm_nt_kernel(scaled, x_ref, *refs):
    wt_ref, o_ref = refs[-2:]
    r = _dot_nt(x_ref[...], wt_ref[...])
    if scaled:
        r = r * refs[0][:, 0:1]
    o_ref[...] = r.astype(o_ref.dtype)


def _matmul_nt_small(act, w_t, layer, out_dtype, bm, name):
    x, rs = act
    m, k = x.shape
    n = w_t.shape[1]
    bm = min(bm, m)
    scaled = rs is not None
    in_specs = [pl.BlockSpec((bm, k), lambda i: (i, 0))]
    if scaled:
        in_specs.append(pl.BlockSpec((bm, ROW_SCALE_W), lambda i: (i, 0)))
    in_specs.append(pl.BlockSpec((None, n, k), lambda i: (layer, 0, 0)))
    return pl.pallas_call(
        functools.partial(_mm_nt_kernel, scaled),
        out_shape=jax.ShapeDtypeStruct((m, n), out_dtype),
        grid=(m // bm,),
        in_specs=in_specs,
        out_specs=pl.BlockSpec((bm, n), lambda i: (i, 0)),
        compiler_params=_cparams("parallel"),
        name=name,
    )(*((x, rs, w_t) if scaled else (x, w_t)))


def _mm_wcast_kernel(scaled, x_ref, *refs):
    w_ref, o_ref, w_scr = refs[-3:]

    @pl.when(pl.program_id(1) == 0)
    def _():
        w_scr[...] = w_ref[...].astype(BF16)

    r = _dot(x_ref[...], w_scr[...])
    if scaled:
        r = r * refs[0][:, 0:1]
    o_ref[...] = r.astype(o_ref.dtype)


def _matmul_wcast(act, w, layer, out_dtype, bm, bn, name):
    x, rs = act
    m, k = x.shape
    n = w.shape[2]
    bm, bn = min(bm, m), min(bn, n)
    scaled = rs is not None
    in_specs = [pl.BlockSpec((bm, k), lambda j, i: (i, 0))]
    if scaled:
        in_specs.append(pl.BlockSpec((bm, ROW_SCALE_W), lambda j, i: (i, 0)))
    in_specs.append(pl.BlockSpec((None, k, bn), lambda j, i: (layer, 0, j)))
    return pl.pallas_call(
        functools.partial(_mm_wcast_kernel, scaled),
        out_shape=jax.ShapeDtypeStruct((m, n), out_dtype),
        grid=(n // bn, m // bm),
        in_specs=in_specs,
        out_specs=pl.BlockSpec((bm, bn), lambda j, i: (i, j)),
        scratch_shapes=[pltpu.VMEM((k, bn), BF16)],
        compiler_params=_cparams("arbitrary", "arbitrary"),
        name=name,
    )(*((x, rs, w) if scaled else (x, w)))


def _swiglu_kernel(bn, scaled, x_ref, *refs):
    wg_ref, wu_ref, o_ref, w_scr = refs[-4:]

    @pl.when(pl.program_id(1) == 0)
    def _():
        w_scr[:, 0:bn] = wg_ref[...].astype(BF16)
        w_scr[:, bn:2 * bn] = wu_ref[...].astype(BF16)

    r = _dot(x_ref[...], w_scr[...])
    if scaled:
        r = r * refs[0][:, 0:1]
    o_ref[...] = (_silu(r[:, 0:bn]) * r[:, bn:2 * bn]).astype(o_ref.dtype)


def _swiglu_in(act, w, layer, bm=1024, bn=256):
    x, rs = act
    m, k = x.shape
    f = w.shape[2] // 2
    nb = f // bn
    scaled = rs is not None
    in_specs = [pl.BlockSpec((bm, k), lambda j, i: (i, 0))]
    if scaled:
        in_specs.append(pl.BlockSpec((bm, ROW_SCALE_W), lambda j, i: (i, 0)))
    in_specs += [pl.BlockSpec((None, k, bn), lambda j, i: (layer, 0, j)),
                 pl.BlockSpec((None, k, bn), lambda j, i: (layer, 0, j + nb))]
    return pl.pallas_call(
        functools.partial(_swiglu_kernel, bn, scaled),
        out_shape=jax.ShapeDtypeStruct((m, f), BF16),
        grid=(nb, m // bm),
        in_specs=in_specs,
        out_specs=pl.BlockSpec((bm, bn), lambda j, i: (i, j)),
        scratch_shapes=[pltpu.VMEM((k, 2 * bn), BF16)],
        compiler_params=_cparams("arbitrary", "arbitrary"),
        name="swiglu_in",
    )(*((x, rs, w, w) if scaled else (x, w, w)))


def _out_proj_kernel(nm, scale, emit_h, lhs_ref, w_ref, x_ref, gp_ref, gn_ref, xo_ref, *rest):
    if emit_h:
        ho_ref, rs_ref, y_scr, ssq_scr, ssq2_scr = rest
    else:
        y_scr, ssq_scr = rest
    i, n = pl.program_id(0), pl.program_id(1)
    cur = lax.rem(i, 2)
    prev = 1 - cur
    inv_d = 1.0 / (y_scr.shape[1] * y_scr.shape[3])

    def matmul_part():
        yb = _dot(lhs_ref[...], w_ref[...])
        y_scr[cur, n] = yb.astype(BF16)
        part = jnp.sum(yb * yb, axis=-1, keepdims=True)
        ssq_scr[cur] = jnp.where(n == 0, part, ssq_scr[cur] + part)

    def residual_part():
        r = lax.rsqrt(ssq_scr[prev] * inv_d + EPS)
        xn = x_ref[...] + scale * (y_scr[prev, n].astype(F32) * r * gp_ref[...])
        xo_ref[...] = xn
        if emit_h:
            ho_ref[...] = (xn * gn_ref[...]).astype(BF16)
            part = jnp.sum(xn * xn, axis=-1, keepdims=True)
            acc = jnp.where(n == 0, part, ssq2_scr[...] + part)
            ssq2_scr[...] = acc
            rs_ref[...] = jnp.broadcast_to(lax.rsqrt(acc * inv_d + EPS), rs_ref.shape)

    @pl.when(i == 0)
    def _():
        matmul_part()

    @pl.when((i > 0) & (i < nm))
    def _():
        residual_part()
        matmul_part()

    @pl.when(i == nm)
    def _():
        residual_part()


def _out_proj_post(lhs, w, layer, x, g_post, scale, g_next, name, bm=1024, bn=512):
    m, k = lhs.shape
    d = w.shape[2]
    nm, nn = m // bm, d // bn
    emit_h = g_next is not None
    gn = g_next if emit_h else g_post
    res_map = lambda i, n: (jnp.maximum(i - 1, 0), jnp.where(i == 0, 0, n))
    vec_map = lambda i, n: (0, jnp.where(i == 0, 0, n))
    res_spec = pl.BlockSpec((bm, bn), res_map)
    out_shape = [jax.ShapeDtypeStruct((m, d), F32)]
    out_specs = [res_spec]
    scratch = [pltpu.VMEM((2, nn, bm, bn), BF16), pltpu.VMEM((2, bm, 1), F32)]
    if emit_h:
        out_shape += [jax.ShapeDtypeStruct((m, d), BF16), jax.ShapeDtypeStruct((m, ROW_SCALE_W), F32)]
        out_specs += [res_spec, pl.BlockSpec((bm, ROW_SCALE_W), lambda i, n: (jnp.maximum(i - 1, 0), 0))]
        scratch.append(pltpu.VMEM((bm, 1), F32))
    res = pl.pallas_call(
        functools.partial(_out_proj_kernel, nm, scale, emit_h),
        out_shape=out_shape,
        grid=(nm + 1, nn),
        in_specs=[pl.BlockSpec((bm, k), lambda i, n: (jnp.minimum(i, nm - 1), 0)),
                  pl.BlockSpec((None, k, bn), lambda i, n: (layer, 0, n)),
                  res_spec,
                  pl.BlockSpec((1, bn), vec_map),
                  pl.BlockSpec((1, bn), vec_map)],
        out_specs=out_specs,
        scratch_shapes=scratch,
        compiler_params=_cparams("arbitrary", "arbitrary"),
        name=name,
    )(lhs, w, x, g_post.reshape(1, d), gn.reshape(1, d))
    return (res[0], (res[1], res[2])) if emit_h else (res[0], None)


def _gla_kernel(q_ref, k_ref, v_ref, r_ref, sm_ref, wgk_ref, bgk_ref, ng_ref, o_ref, st_ref, cum_ref):
    @pl.when(pl.program_id(1) == 0)
    def _():
        st_ref[...] = jnp.zeros_like(st_ref)

    rows = q_ref.shape[0]
    a_hi, a_mid, _ = _split3(sm_ref[...])
    w_hi, w_mid, _ = _split3(wgk_ref[...])
    z = _dot(a_hi, w_hi) + _dot(a_hi, w_mid) + _dot(a_mid, w_hi) + bgk_ref[...]
    logf = (jnp.minimum(z, 0.0) - jnp.log1p(jnp.exp(-jnp.abs(z)))) * (1.0 / GLA_TAU)
    tri = _chunk_tril(CUM_BLOCK)
    for b in range(rows // CUM_BLOCK):
        sl = slice(b * CUM_BLOCK, (b + 1) * CUM_BLOCK)
        cum_ref[sl, :] = _sel_dot(tri, logf[sl])

    ri = lax.broadcasted_iota(jnp.int32, (CHUNK, CHUNK), 0)
    ci = lax.broadcasted_iota(jnp.int32, (CHUNK, CHUNK), 1)
    tril = ri >= ci
    for c in range(rows // CHUNK):
        sl = pl.ds(c * CHUNK, CHUNK)
        cum = cum_ref[sl, :]
        cum_end = cum[CHUNK - 1:CHUNK, :]
        up, down = jnp.exp(cum), jnp.exp(-cum)
        q = q_ref[sl, :].astype(F32) * (GLA_DK ** -0.5)
        k = k_ref[sl, :].astype(F32)
        v = v_ref[sl, :]
        q_dec = (q * up).astype(BF16)
        a_past = _dot_nt(q_dec, (k * down).astype(BF16))
        a_future = _dot_nt((q * down).astype(BF16), (k * up).astype(BF16))
        attn = jnp.where(tril, a_past, a_future).astype(BF16)
        state_t = st_ref[...]
        o = _dot(attn, v) + _dot_nt(q_dec, state_t.astype(BF16))
        contrib_t = _dot_tn(v, (k * jnp.exp(cum_end - cum)).astype(BF16))
        st_ref[...] = state_t * jnp.exp(cum_end) + contrib_t
        o = o * lax.rsqrt(jnp.mean(o * o, axis=-1, keepdims=True) + EPS) * ng_ref[...]
        o_ref[sl, :] = (o * _silu(r_ref[sl, :].astype(F32))).astype(o_ref.dtype)


def _gla(proj, small, w_gk_pad, b_gk, norm_g):
    s = proj.shape[0]
    rows = min(MIX_ROWS, s)
    qb, vb = P_GLA_Q // GLA_DK, P_GLA_V // GLA_DV
    kb, rb = P_GLA_K // GLA_DK, P_GLA_R // GLA_DV
    return pl.pallas_call(
        _gla_kernel,
        out_shape=jax.ShapeDtypeStruct((s, GLA_V), BF16),
        grid=(GLA_HEADS, s // rows),
        in_specs=[pl.BlockSpec((rows, GLA_DK), lambda h, i: (i, qb + h)),
                  pl.BlockSpec((rows, GLA_DK), lambda h, i: (i, kb + h)),
                  pl.BlockSpec((rows, GLA_DV), lambda h, i: (i, vb + h)),
                  pl.BlockSpec((rows, GLA_DV), lambda h, i: (i, rb + h)),
                  pl.BlockSpec((rows, SMALL_W), lambda h, i: (i, 0)),
                  pl.BlockSpec((SMALL_W, GLA_DK), lambda h, i: (0, h)),
                  pl.BlockSpec((1, GLA_DK), lambda h, i: (0, h)),
                  pl.BlockSpec((1, GLA_DV), lambda h, i: (0, h))],
        out_specs=pl.BlockSpec((rows, GLA_DV), lambda h, i: (i, h)),
        scratch_shapes=[pltpu.VMEM((GLA_DV, GLA_DK), F32), pltpu.VMEM((rows, GLA_DK), F32)],
        compiler_params=_cparams("parallel", "arbitrary"),
        name="gla",
    )(proj, proj, proj, proj, small, w_gk_pad, b_gk.reshape(1, GLA_QK), norm_g.reshape(1, GLA_V))


def _causal_conv_silu(xp_ref, x_ref, w_ref, b_ref, first):
    rows = x_ref.shape[0]

    @pl.when(first)
    def _():
        xp_ref[pl.ds(0, 8), :] = jnp.zeros((8, xp_ref.shape[1]), F32)

    xp_ref[pl.ds(8, rows), :] = x_ref[...].astype(F32)
    acc = b_ref[...]
    for tap in range(SSM_CONV):
        acc = acc + xp_ref[pl.ds(8 - (SSM_CONV - 1) + tap, rows), :] * w_ref[tap:tap + 1, :]
    xp_ref[pl.ds(0, 8), :] = xp_ref[pl.ds(rows, 8), :]
    return _silu(acc)


def _ssd_kernel(z_ref, xs_ref, b_ref, c_ref, sm_ref, e_ref, cwx_ref, cwb_ref, cwc_ref,
                cbx_ref, cbb_ref, cbc_ref, dtb_ref, alog_ref, dsk_ref, ng_ref,
                o_ref, s_ref, xpx_ref, xpb_ref, xpc_ref, xs_scr, b_scr, c_scr, dt_scr, cum_scr):
    first = pl.program_id(1) == 0

    @pl.when(first)
    def _():
        s_ref[...] = jnp.zeros_like(s_ref)

    rows = xs_ref.shape[0]
    xs_scr[...] = _causal_conv_silu(xpx_ref, xs_ref, cwx_ref, cbx_ref, first)
    b_scr[...] = _causal_conv_silu(xpb_ref, b_ref, cwb_ref, cbb_ref, first).astype(BF16)
    c_scr[...] = _causal_conv_silu(xpc_ref, c_ref, cwc_ref, cbc_ref, first).astype(BF16)

    dt_in = _sel_dot_rhs(sm_ref[...], e_ref[0]) + dtb_ref[...]
    dt = jnp.maximum(dt_in, 0.0) + jnp.log1p(jnp.exp(-jnp.abs(dt_in)))
    dt_scr[...] = dt
    da = dt * (-jnp.exp(alog_ref[...]))
    tri = _chunk_tril(CUM_BLOCK)
    for b in range(rows // CUM_BLOCK):
        sl = slice(b * CUM_BLOCK, (b + 1) * CUM_BLOCK)
        cum_scr[sl, :] = _sel_dot(tri, da[sl])

    gw = SSM_GW
    rw = lax.broadcasted_iota(jnp.int32, (CHUNK, gw), 0)
    cw = lax.broadcasted_iota(jnp.int32, (CHUNK, gw), 1)
    diag_tile = (jnp.bitwise_and(cw, SSM_HEADDIM - 1) == rw).astype(F32)
    half = gw // 2
    rb = lax.broadcasted_iota(jnp.int32, (half, half), 0)
    cb_ = lax.broadcasted_iota(jnp.int32, (half, half), 1)
    head_mask = jnp.right_shift(rb, 6) == jnp.right_shift(cb_, 6)
    reps = half // CHUNK
    for c in range(rows // CHUNK):
        sl = pl.ds(c * CHUNK, CHUNK)
        cum = cum_scr[sl, :]
        cum_row = jnp.sum(cum * diag_tile, axis=0, keepdims=True)
        seg = jnp.exp(-jnp.abs(cum - cum_row))
        xs = xs_scr[sl, :]
        bm = b_scr[sl, :]
        cm = c_scr[sl, :]
        xdt = xs * dt_scr[sl, :]
        cbt = _dot_nt(cm, jnp.concatenate([bm] * SSM_HPG, axis=0))
        m_all = (cbt * seg).astype(BF16)
        intra = []
        for hlf in range(2):
            xh = xdt[:, hlf * half:(hlf + 1) * half]
            bd = jnp.where(head_mask, jnp.concatenate([xh] * reps, axis=0), 0.0).astype(BF16)
            intra.append(_dot(m_all[:, hlf * half:(hlf + 1) * half], bd))
        intra = jnp.concatenate(intra, axis=1)
        cum_end = cum[CHUNK - 1:CHUNK, :]
        state = s_ref[...]
        inter = _dot(cm, state.astype(BF16)) * jnp.exp(cum)
        contrib = _dot_tn(bm, (xdt * jnp.exp(cum_end - cum)).astype(BF16))
        s_ref[...] = state * jnp.exp(cum_end) + contrib
        y = intra + inter + xs * dsk_ref[...]
        y = y * _silu(z_ref[sl, :].astype(F32))
        y = y * lax.rsqrt(jnp.mean(y * y, axis=-1, keepdims=True) + EPS) * ng_ref[...]
        o_ref[sl, :] = y.astype(o_ref.dtype)


def _sel_dot_rhs(x, sel):
    hi, mid, lo = _split3(x)
    return _dot(hi, sel) + _dot(mid, sel) + _dot(lo, sel)


def _ssd(proj, small, expand, conv_w, conv_b, dt_bias, a_log, d_skip, norm_g):
    s = proj.shape[0]
    rows = min(MIX_ROWS, s)
    gw, st = SSM_GW, SSM_STATE
    zb, xb = P_SSM_Z // gw, P_SSM_X // gw
    bb, cb = P_SSM_B // st, P_SSM_C // st
    cwb, cwc = SSM_INNER // st, (SSM_INNER + SSM_BC) // st

    def lane_expand(p):
        return jnp.repeat(p.astype(F32), SSM_HEADDIM).reshape(1, SSM_INNER)

    conv_b2 = conv_b.reshape(1, SSM_CONV_DIM)
    vec = pl.BlockSpec((1, gw), lambda g, i: (0, g))
    return pl.pallas_call(
        _ssd_kernel,
        out_shape=jax.ShapeDtypeStruct((s, SSM_INNER), BF16),
        grid=(SSM_GROUPS, s // rows),
        in_specs=[pl.BlockSpec((rows, gw), lambda g, i: (i, zb + g)),
                  pl.BlockSpec((rows, gw), lambda g, i: (i, xb + g)),
                  pl.BlockSpec((rows, st), lambda g, i: (i, bb + g)),
                  pl.BlockSpec((rows, st), lambda g, i: (i, cb + g)),
                  pl.BlockSpec((rows, SMALL_W), lambda g, i: (i, 0)),
                  pl.BlockSpec((1, SMALL_W, gw), lambda g, i: (g, 0, 0)),
                  pl.BlockSpec((SSM_CONV, gw), lambda g, i: (0, g)),
                  pl.BlockSpec((SSM_CONV, st), lambda g, i: (0, cwb + g)),
                  pl.BlockSpec((SSM_CONV, st), lambda g, i: (0, cwc + g)),
                  pl.BlockSpec((1, gw), lambda g, i: (0, g)),
                  pl.BlockSpec((1, st), lambda g, i: (0, cwb + g)),
                  pl.BlockSpec((1, st), lambda g, i: (0, cwc + g)),
                  vec, vec, vec, vec],
        out_specs=pl.BlockSpec((rows, gw), lambda g, i: (i, g)),
        scratch_shapes=[pltpu.VMEM((st, gw), F32),
                        pltpu.VMEM((rows + 8, gw), F32),
                        pltpu.VMEM((rows + 8, st), F32),
                        pltpu.VMEM((rows + 8, st), F32),
                        pltpu.VMEM((rows, gw), F32),
                        pltpu.VMEM((rows, st), BF16),
                        pltpu.VMEM((rows, st), BF16),
                        pltpu.VMEM((rows, gw), F32),
                        pltpu.VMEM((rows, gw), F32)],
        compiler_params=_cparams("parallel", "arbitrary"),
        name="ssd",
    )(proj, proj, proj, proj, small, expand, conv_w, conv_w, conv_w, conv_b2, conv_b2, conv_b2,
      lane_expand(dt_bias), lane_expand(a_log), lane_expand(d_skip), norm_g.reshape(1, SSM_INNER))


def _ssd_expand_table():
    rows = jnp.arange(SMALL_W, dtype=jnp.int32)[None, :, None]
    cols = jnp.arange(SSM_GW, dtype=jnp.int32)[None, None, :]
    grp = jnp.arange(SSM_GROUPS, dtype=jnp.int32)[:, None, None]
    return (rows == SMALL_DT0 + grp * SSM_HPG + cols // SSM_HEADDIM).astype(BF16)


def _rope_table_kernel(pos_ref, freq_ref, sign_ref, cos_ref, sin_ref):
    ang = pos_ref[...] * freq_ref[...]
    cos_ref[...] = jnp.cos(ang)
    sin_ref[...] = jnp.sin(ang) * sign_ref[...]


def _rope_tables(positions):
    s = positions.shape[-1]
    half = RET_DK // 2
    inv_freq = ROPE_BASE ** (-jnp.arange(half, dtype=F32) / half)
    freq = jnp.concatenate([inv_freq, inv_freq]).reshape(1, RET_DK)
    sign = jnp.concatenate([-jnp.ones((half,), F32), jnp.ones((half,), F32)]).reshape(1, RET_DK)
    pos = positions.astype(F32).reshape(s, 1)
    rows = min(1024, s)
    tab = jax.ShapeDtypeStruct((s, RET_DK), F32)
    return pl.pallas_call(
        _rope_table_kernel,
        out_shape=[tab, tab],
        grid=(s // rows,),
        in_specs=[pl.BlockSpec((rows, 1), lambda i: (i, 0)),
                  pl.BlockSpec((1, RET_DK), lambda i: (0, 0)),
                  pl.BlockSpec((1, RET_DK), lambda i: (0, 0))],
        out_specs=[pl.BlockSpec((rows, RET_DK), lambda i: (i, 0))] * 2,
        compiler_params=_cparams("parallel"),
        name="rope_tables",
    )(pos, freq, sign)


def _ret_kernel(q_ref, k_ref, v_ref, g_ref, cos_ref, sin_ref, dmat_ref, qdec_ref, kdec_ref, sdec_ref,
                ng_ref, o_ref, s_ref):
    @pl.when(pl.program_id(1) == 0)
    def _():
        s_ref[...] = jnp.zeros_like(s_ref)

    cos, sin = cos_ref[...], sin_ref[...]

    def rot(t):
        t = t.astype(F32)
        return t * cos + pltpu.roll(t, RET_DK // 2, axis=1) * sin

    q = rot(q_ref[...])
    k = rot(k_ref[...]) * (RET_DK ** -0.5)
    v = v_ref[...]
    scores = (_dot_nt(q.astype(BF16), k.astype(BF16)) * dmat_ref[0]).astype(BF16)
    state = s_ref[...]
    o = _dot(scores, v) + _dot((q * qdec_ref[0]).astype(BF16), state.astype(BF16))
    s_ref[...] = state * sdec_ref[0] + _dot_tn((k * kdec_ref[0]).astype(BF16), v)
    o = o * lax.rsqrt(jnp.mean(o * o, axis=-1, keepdims=True) + EPS) * ng_ref[...]
    o_ref[...] = (o * _silu(g_ref[...].astype(F32))).astype(o_ref.dtype)


def _ret_tables(rows):
    log_gamma = jnp.log1p(-jnp.exp2(-5.0 - jnp.arange(RET_HEADS, dtype=F32)))[:, None, None]
    i = jnp.arange(rows, dtype=jnp.int32)
    dist = (i[:, None] - i[None, :]).astype(F32)[None]
    ch_i, ch_j = (i // CHUNK)[:, None], (i // CHUNK)[None, :]
    dmat = jnp.where((ch_i == ch_j)[None], jnp.exp(log_gamma * jnp.abs(dist)),
                     jnp.where((ch_j < ch_i)[None], jnp.exp(log_gamma * dist), 0.0))
    fi = i.astype(F32)[None, :, None]
    qdec = jnp.broadcast_to(jnp.exp(log_gamma * (fi + 1.0)), (RET_HEADS, rows, RET_DK))
    kdec = jnp.broadcast_to(jnp.exp(log_gamma * (rows - 1.0 - fi)), (RET_HEADS, rows, RET_DK))
    sdec = jnp.broadcast_to(jnp.exp(log_gamma * float(rows)), (RET_HEADS, 1, RET_DV))
    return dmat, qdec, kdec, sdec


def _retention(proj, cos_t, sin_t, tables, norm_g):
    s = proj.shape[0]
    rows = min(RET_BLOCK, s)
    dmat, qdec, kdec, sdec = tables
    qb, kb = P_RET_Q // RET_DK, P_RET_K // RET_DK
    vb, gb = P_RET_V // RET_DV, P_RET_G // RET_DV
    return pl.pallas_call(
        _ret_kernel,
        out_shape=jax.ShapeDtypeStruct((s, RET_V), BF16),
        grid=(RET_HEADS, s // rows),
        in_specs=[pl.BlockSpec((rows, RET_DK), lambda h, i: (i, qb + h)),
                  pl.BlockSpec((rows, RET_DK), lambda h, i: (i, kb + h)),
                  pl.BlockSpec((rows, RET_DV), lambda h, i: (i, vb + h)),
                  pl.BlockSpec((rows, RET_DV), lambda h, i: (i, gb + h)),
                  pl.BlockSpec((rows, RET_DK), lambda h, i: (i, 0)),
                  pl.BlockSpec((rows, RET_DK), lambda h, i: (i, 0)),
                  pl.BlockSpec((1, rows, rows), lambda h, i: (h, 0, 0)),
                  pl.BlockSpec((1, rows, RET_DK), lambda h, i: (h, 0, 0)),
                  pl.BlockSpec((1, rows, RET_DK), lambda h, i: (h, 0, 0)),
                  pl.BlockSpec((1, 1, RET_DV), lambda h, i: (h, 0, 0)),
                  pl.BlockSpec((1, RET_DV), lambda h, i: (0, h))],
        out_specs=pl.BlockSpec((rows, RET_DV), lambda h, i: (i, h)),
        scratch_shapes=[pltpu.VMEM((RET_DK, RET_DV), F32)],
        compiler_params=_cparams("parallel", "arbitrary"),
        name="retention",
    )(proj, proj, proj, proj, cos_t, sin_t, dmat, qdec, kdec, sdec, norm_g.reshape(1, RET_V))


def _merge_kernel(gl_ref, y0_ref, y1_ref, y2_ref, gu0_ref, gu1_ref, gu2_ref, gb0_ref, gb1_ref, gb2_ref,
                  wo0_ref, wo1_ref, wo2_ref, o_ref):
    gl = gl_ref[...]
    acc = None
    for y_ref, gu_ref, gb_ref, wo_ref in ((y0_ref, gu0_ref, gb0_ref, wo0_ref),
                                          (y1_ref, gu1_ref, gb1_ref, wo1_ref),
                                          (y2_ref, gu2_ref, gb2_ref, wo2_ref)):
        gate = jax.nn.sigmoid(_dot(gl, gu_ref[...]) + gb_ref[...])
        term = gate * _dot(y_ref[...], wo_ref[...])
        acc = term if acc is None else acc + term
    o_ref[...] = acc.astype(o_ref.dtype)


def _merge(proj, ys, gate_w2, gate_b, w_out, layer, bm=1024, bn=512):
    s = proj.shape[0]
    bm = min(bm, s)
    d = D_MODEL
    nb = d // bn
    gate_b4 = gate_b.reshape(DEPTH, N_BRANCH, 1, d)
    y_spec = pl.BlockSpec((bm, BRANCH_W), lambda i, j: (i, 0))
    in_specs = [pl.BlockSpec((bm, GATE_RANK), lambda i, j: (i, P_GATE // GATE_RANK)), y_spec, y_spec, y_spec]
    in_specs += [pl.BlockSpec((None, GATE_RANK, bn),
                              functools.partial(lambda n, i, j: (layer * N_BRANCH + n, 0, j), n))
                 for n in range(N_BRANCH)]
    in_specs += [pl.BlockSpec((None, None, 1, bn), functools.partial(lambda n, i, j: (layer, n, 0, j), n))
                 for n in range(N_BRANCH)]
    in_specs += [pl.BlockSpec((None, BRANCH_W, bn),
                              functools.partial(lambda n, i, j: (layer * N_BRANCH + n, 0, j), n))
                 for n in range(N_BRANCH)]
    return pl.pallas_call(
        _merge_kernel,
        out_shape=jax.ShapeDtypeStruct((s, d), BF16),
        grid=(s // bm, nb),
        in_specs=in_specs,
        out_specs=pl.BlockSpec((bm, bn), lambda i, j: (i, j)),
        compiler_params=_cparams("parallel", "parallel"),
        name="gated_merge",
    )(proj, ys[0], ys[1], ys[2], gate_w2, gate_w2, gate_w2, gate_b4, gate_b4, gate_b4, w_out, w_out, w_out)


def _xattn_kernel(q_ref, kv_ref, o_ref):
    for h in range(XA_HEADS):
        lo, hi = h * XA_DIM, (h + 1) * XA_DIM
        s = _dot_nt(q_ref[:, lo:hi], kv_ref[:, lo:hi]) * (XA_DIM ** -0.5)
        p = jnp.exp(s - jnp.max(s, axis=-1, keepdims=True))
        p = p / jnp.sum(p, axis=-1, keepdims=True)
        o_ref[:, lo:hi] = _dot(p.astype(BF16), kv_ref[:, XA_W + lo:XA_W + hi]).astype(o_ref.dtype)


def _xattn(q, kv, rows=512):
    s = q.shape[0]
    rows = min(rows, s)
    n_mem = kv.shape[0]
    return pl.pallas_call(
        _xattn_kernel,
        out_shape=jax.ShapeDtypeStruct((s, XA_W), BF16),
        grid=(s // rows,),
        in_specs=[pl.BlockSpec((rows, XA_W), lambda i: (i, 0)),
                  pl.BlockSpec((n_mem, 2 * XA_W), lambda i: (0, 0))],
        out_specs=pl.BlockSpec((rows, XA_W), lambda i: (i, 0)),
        compiler_params=_cparams("parallel"),
        name="xattn",
    )(q, kv)


def kernel(x, mem, positions, norm_g, mem_norm_g, ffn1_w_in, ffn1_w_out, mix_w_in, gla_w_gk, gla_b_gk,
           gla_norm_g, ssm_conv_w, ssm_conv_b, ssm_dt_bias, ssm_a_log, ssm_d, ssm_norm_g, ret_norm_g,
           gate_w_up, gate_b, mix_w_out, xattn_w_q, xattn_w_kv, xattn_w_o, ffn2_w_in, ffn2_w_out):
    b, s, d = x.shape
    assert b == 1 and d == D_MODEL and s % max(RET_BLOCK, MIX_ROWS, 1024) == 0
    nl = DEPTH
    xr = x.reshape(s, d)
    memr = mem.reshape(mem.shape[1], d)
    cos_t, sin_t = _rope_tables(positions)
    ret_tabs = _ret_tables(min(RET_BLOCK, s))
    expand = _ssd_expand_table()

    ffn1_out = _cast_bf16(ffn1_w_out.reshape(nl * D_FF, d), 512).reshape(nl, D_FF, d)
    ffn2_out = _cast_bf16(ffn2_w_out.reshape(nl * D_FF, d), 512).reshape(nl, D_FF, d)
    xa_out = _cast_bf16(xattn_w_o.reshape(nl * XA_W, d), 512).reshape(nl, XA_W, d)
    mix_out = _cast_bf16(mix_w_out.reshape(nl * N_BRANCH * BRANCH_W, d), 512).reshape(nl * N_BRANCH, BRANCH_W, d)
    gate_up = _cast_bf16(jnp.transpose(gate_w_up, (0, 2, 1, 3)).reshape(nl * N_BRANCH * GATE_RANK, d),
                         512).reshape(nl * N_BRANCH, GATE_RANK, d)
    mix_w_in_t = jnp.transpose(mix_w_in, (0, 2, 1))
    mix_in_big = _repack_mix_w_in(mix_w_in_t)
    mix_in_side_t = _side_mix_w_in(mix_w_in_t)
    w_gk_pad = jnp.concatenate([gla_w_gk, jnp.zeros((nl, SMALL_W - GLA_RANK, GLA_QK), F32)], axis=1)

    h = (_norm_cast(xr, norm_g[0, 0]), None)
    for l in range(nl):
        ng = norm_g[l]
        a = _swiglu_in(h, ffn1_w_in, l)
        xr, h = _out_proj_post(a, ffn1_out, l, xr, ng[1], 0.5, ng[2], "ffn_out_post")
        proj = _matmul(h, mix_in_big, l, BF16, 1024, 1280, "mix_in")
        small = _matmul_nt_small(h, mix_in_side_t, l, F32, 1024, "mix_in_small")
        y_gla = _gla(proj, small, w_gk_pad[l], gla_b_gk[l], gla_norm_g[l])
        y_ssd = _ssd(proj, small, expand, ssm_conv_w[l], ssm_conv_b[l], ssm_dt_bias[l], ssm_a_log[l],
                     ssm_d[l], ssm_norm_g[l])
        y_ret = _retention(proj, cos_t, sin_t, ret_tabs, ret_norm_g[l])
        mixed = _merge(proj, (y_gla, y_ssd, y_ret), gate_up, gate_b, mix_out, l)
        xr, hn = _post(mixed, xr, ng[3], 1.0, ng[4])
        q = _matmul_wcast((hn, None), xattn_w_q, l, BF16, 1024, 512, "xattn_q")
        mem_n = _norm_cast(memr, mem_norm_g[l])
        kv = _matmul_wcast((mem_n, None), xattn_w_kv, l, BF16, 256, 512, "xattn_kv")
        o = _xattn(q, kv)
        xr, h = _out_proj_post(o, xa_out, l, xr, ng[5], 1.0, ng[6], "xattn_out_post")
        a = _swiglu_in(h, ffn2_w_in, l)
        g_next = norm_g[l + 1, 0] if l + 1 < nl else None
        xr, h = _out_proj_post(a, ffn2_out, l, xr, ng[7], 0.5, g_next, "ffn_out_post")
    return xr.reshape(b, s, d)
```

```python
import functools

import numpy as np
import jax
import jax.numpy as jnp
from jax import lax
from jax.experimental import pallas as pl
from jax.experimental.pallas import tpu as pltpu

F32 = jnp.float32
BF16 = jnp.bfloat16

D_MODEL = 4096
DEPTH = 4
CHUNK = 64
D_FF = 4096
EPS = 1e-6
GLA_HEADS, GLA_DK, GLA_DV, GLA_RANK, GLA_TAU = 4, 256, 512, 16, 16.0
GLA_QK, GLA_V = GLA_HEADS * GLA_DK, GLA_HEADS * GLA_DV
SSM_HEADS, SSM_HEADDIM, SSM_GROUPS, SSM_STATE, SSM_CONV = 32, 64, 4, 128, 4
SSM_INNER = SSM_HEADS * SSM_HEADDIM
SSM_BC = SSM_GROUPS * SSM_STATE
SSM_CONV_DIM = SSM_INNER + 2 * SSM_BC
SSM_HPG = SSM_HEADS // SSM_GROUPS
SSM_GW = SSM_INNER // SSM_GROUPS
RET_HEADS, RET_DK, RET_DV = 8, 128, 256
RET_QK, RET_V = RET_HEADS * RET_DK, RET_HEADS * RET_DV
ROPE_BASE = 10000.0
XA_HEADS, XA_DIM = 4, 256
XA_W = XA_HEADS * XA_DIM
N_BRANCH, BRANCH_W, GATE_RANK = 3, 2048, 512

_IN_WIDTHS = (GLA_QK, GLA_QK, GLA_V, GLA_V, GLA_RANK, SSM_INNER, SSM_CONV_DIM, SSM_HEADS,
              RET_QK, RET_QK, RET_V, RET_V, GATE_RANK)
_IN_OFF = np.concatenate([[0], np.cumsum(_IN_WIDTHS)]).tolist()
IN_COLS = _IN_OFF[-1]
P_GLA_Q, P_GLA_K, P_GLA_V, P_GLA_R = 0, 1024, 2048, 4096
P_SSM_Z, P_SSM_X, P_SSM_B, P_SSM_C = 6144, 8192, 10240, 10752
P_RET_Q, P_RET_K, P_RET_V, P_RET_G = 11264, 12288, 13312, 15360
P_GATE = 17408
P_COLS = 17920
SMALL_W = 128
SMALL_DT0 = GLA_RANK

V7X_VMEM_LIMIT = 60 * 1024 * 1024

RET_BLOCK = 512
MIX_ROWS = 512
CUM_BLOCK = 256
GLA_HEADS_PER_STEP = 4
SSD_GROUPS_PER_STEP = 2
RET_HEADS_PER_STEP = 4


def _cparams(*sem):
    return pltpu.CompilerParams(dimension_semantics=sem, vmem_limit_bytes=V7X_VMEM_LIMIT)


def _silu(x):
    return x * jax.nn.sigmoid(x)


def _dot(a, b):
    return jnp.dot(a, b, preferred_element_type=F32)


def _dot_nt(a, b):
    return lax.dot_general(a, b, (((1,), (1,)), ((), ())), preferred_element_type=F32)


def _dot_tn(a, b):
    return lax.dot_general(a, b, (((0,), (0,)), ((), ())), preferred_element_type=F32)


def _split3(x):
    hi = x.astype(BF16)
    r1 = x - hi.astype(F32)
    mid = r1.astype(BF16)
    lo = (r1 - mid.astype(F32)).astype(BF16)
    return hi, mid, lo


def _sel_dot(sel, x):
    hi, mid, lo = _split3(x)
    return _dot(sel, hi) + _dot(sel, mid) + _dot(sel, lo)


def _chunk_tril(n):
    ri = lax.broadcasted_iota(jnp.int32, (n, n), 0)
    ci = lax.broadcasted_iota(jnp.int32, (n, n), 1)
    same = jnp.right_shift(ri, 6) == jnp.right_shift(ci, 6)
    return jnp.where(same & (ri >= ci), 1.0, 0.0).astype(BF16)


def _cast_kernel(w_ref, o_ref):
    o_ref[...] = w_ref[...].astype(o_ref.dtype)


def _cast_bf16(w, rows):
    r, c = w.shape
    return pl.pallas_call(
        _cast_kernel,
        out_shape=jax.ShapeDtypeStruct((r, c), BF16),
        grid=(r // rows,),
        in_specs=[pl.BlockSpec((rows, c), lambda i: (i, 0))],
        out_specs=pl.BlockSpec((rows, c), lambda i: (i, 0)),
        compiler_params=_cparams("parallel"),
        name="cast_bf16",
    )(w)


REPACK_COLS = 512


def _packed_to_source_col(c):
    skipped = jnp.where(c >= P_SSM_Z, GLA_RANK, 0) + jnp.where(c >= P_RET_Q, SSM_HEADS, 0)
    return pl.multiple_of(c + skipped, GLA_RANK)


def _repack_kernel(w_ref, o_ref):
    o_ref[...] = w_ref[...].T.astype(o_ref.dtype)


def _repack_mix_w_in(w_t):
    nl, _, d = w_t.shape
    return pl.pallas_call(
        _repack_kernel,
        out_shape=jax.ShapeDtypeStruct((nl, d, P_COLS), BF16),
        grid=(nl, P_COLS // REPACK_COLS),
        in_specs=[pl.BlockSpec((pl.Squeezed(), pl.Element(REPACK_COLS), pl.Element(d)),
                               lambda l, j: (l, _packed_to_source_col(j * REPACK_COLS), 0))],
        out_specs=pl.BlockSpec((None, d, REPACK_COLS), lambda l, j: (l, 0, j)),
        compiler_params=_cparams("parallel", "parallel"),
        name="repack_mix_w_in",
    )(w_t)


def _side_w_kernel(gk_ref, dt_ref, o_ref):
    o_ref[...] = jnp.zeros(o_ref.shape, o_ref.dtype)
    o_ref[0:GLA_RANK, :] = gk_ref[...].astype(o_ref.dtype)
    o_ref[GLA_RANK:GLA_RANK + SSM_HEADS, :] = dt_ref[...].astype(o_ref.dtype)


def _side_mix_w_in(w_t):
    nl, _, d = w_t.shape
    o = _IN_OFF
    return pl.pallas_call(
        _side_w_kernel,
        out_shape=jax.ShapeDtypeStruct((nl, SMALL_W, d), BF16),
        grid=(nl,),
        in_specs=[pl.BlockSpec((pl.Squeezed(), pl.Element(GLA_RANK), pl.Element(d)), lambda l: (l, o[4], 0)),
                  pl.BlockSpec((pl.Squeezed(), pl.Element(SSM_HEADS), pl.Element(d)), lambda l: (l, o[7], 0))],
        out_specs=pl.BlockSpec((None, SMALL_W, d), lambda l: (l, 0, 0)),
        compiler_params=_cparams("parallel"),
        name="side_mix_w_in",
    )(w_t, w_t)


def _norm_cast_kernel(x_ref, g_ref, o_ref):
    x = x_ref[...]
    r = lax.rsqrt(jnp.mean(x * x, axis=-1, keepdims=True) + EPS)
    o_ref[...] = (x * r * g_ref[...]).astype(o_ref.dtype)


def _norm_cast(x, g, rows=256):
    m, d = x.shape
    rows = min(rows, m)
    return pl.pallas_call(
        _norm_cast_kernel,
        out_shape=jax.ShapeDtypeStruct((m, d), BF16),
        grid=(m // rows,),
        in_specs=[pl.BlockSpec((rows, d), lambda i: (i, 0)), pl.BlockSpec((1, d), lambda i: (0, 0))],
        out_specs=pl.BlockSpec((rows, d), lambda i: (i, 0)),
        compiler_params=_cparams("parallel"),
        name="norm_cast",
    )(x, g.reshape(1, d))


def _post_kernel(scale, emit_h, y_ref, x_ref, gp_ref, gn_ref, xo_ref, *maybe_h):
    y = y_ref[...].astype(F32)
    r = lax.rsqrt(jnp.mean(y * y, axis=-1, keepdims=True) + EPS)
    xn = x_ref[...] + scale * (y * r * gp_ref[...])
    xo_ref[...] = xn
    if emit_h:
        r2 = lax.rsqrt(jnp.mean(xn * xn, axis=-1, keepdims=True) + EPS)
        maybe_h[0][...] = (xn * r2 * gn_ref[...]).astype(BF16)


def _post(y, x, g_post, scale, g_next, rows=256):
    m, d = x.shape
    emit_h = g_next is not None
    gn = g_next if emit_h else g_post
    row_spec = pl.BlockSpec((rows, d), lambda i: (i, 0))
    vec_spec = pl.BlockSpec((1, d), lambda i: (0, 0))
    out_shape = [jax.ShapeDtypeStruct((m, d), F32)]
    out_specs = [row_spec]
    if emit_h:
        out_shape.append(jax.ShapeDtypeStruct((m, d), BF16))
        out_specs.append(row_spec)
    res = pl.pallas_call(
        functools.partial(_post_kernel, scale, emit_h),
        out_shape=out_shape,
        grid=(m // rows,),
        in_specs=[row_spec, row_spec, vec_spec, vec_spec],
        out_specs=out_specs,
        compiler_params=_cparams("parallel"),
        name="post_norm",
    )(y, x, g_post.reshape(1, d), gn.reshape(1, d))
    return (res[0], res[1]) if emit_h else (res[0], None)


ROW_SCALE_W = 128


def _mm_kernel(scaled, x_ref, *refs):
    w_ref, o_ref = refs[-2:]
    r = _dot(x_ref[...], w_ref[...])
    if scaled:
        r = r * refs[0][:, 0:1]
    o_ref[...] = r.astype(o_ref.dtype)


def _matmul(act, w, layer, out_dtype, bm, bn, name):
    x, rs = act
    m, k = x.shape
    n = w.shape[2]
    bm, bn = min(bm, m), min(bn, n)
    scaled = rs is not None
    in_specs = [pl.BlockSpec((bm, k), lambda i, j: (i, 0))]
    if scaled:
        in_specs.append(pl.BlockSpec((bm, ROW_SCALE_W), lambda i, j: (i, 0)))
    in_specs.append(pl.BlockSpec((None, k, bn), lambda i, j: (layer, 0, j)))
    return pl.pallas_call(
        functools.partial(_mm_kernel, scaled),
        out_shape=jax.ShapeDtypeStruct((m, n), out_dtype),
        grid=(m // bm, n // bn),
        in_specs=in_specs,
        out_specs=pl.BlockSpec((bm, bn), lambda i, j: (i, j)),
        compiler_params=_cparams("parallel", "parallel"),
        name=name,
    )(*((x, rs, w) if scaled else (x, w)))


def _mm_nt_kernel(scaled, x_ref, *refs):
    wt_ref, o_ref = refs[-2:]
    r = _dot_nt(x_ref[...], wt_ref[...])
    if scaled:
        r = r * refs[0][:, 0:1]
    o_ref[...] = r.astype(o_ref.dtype)


def _matmul_nt_small(act, w_t, layer, out_dtype, bm, name):
    x, rs = act
    m, k = x.shape
    n = w_t.shape[1]
    bm = min(bm, m)
    scaled = rs is not None
    in_specs = [pl.BlockSpec((bm, k), lambda i: (i, 0))]
    if scaled:
        in_specs.append(pl.BlockSpec((bm, ROW_SCALE_W), lambda i: (i, 0)))
    in_specs.append(pl.BlockSpec((None, n, k), lambda i: (layer, 0, 0)))
    return pl.pallas_call(
        functools.partial(_mm_nt_kernel, scaled),
        out_shape=jax.ShapeDtypeStruct((m, n), out_dtype),
        grid=(m // bm,),
        in_specs=in_specs,
        out_specs=pl.BlockSpec((bm, n), lambda i: (i, 0)),
        compiler_params=_cparams("parallel"),
        name=name,
    )(*((x, rs, w_t) if scaled else (x, w_t)))


def _mm_wcast_kernel(scaled, x_ref, *refs):
    w_ref, o_ref, w_scr = refs[-3:]

    @pl.when(pl.program_id(1) == 0)
    def _():
        w_scr[...] = w_ref[...].astype(BF16)

    r = _dot(x_ref[...], w_scr[...])
    if scaled:
        r = r * refs[0][:, 0:1]
    o_ref[...] = r.astype(o_ref.dtype)


def _matmul_wcast(act, w, layer, out_dtype, bm, bn, name):
    x, rs = act
    m, k = x.shape
    n = w.shape[2]
    bm, bn = min(bm, m), min(bn, n)
    scaled = rs is not None
    in_specs = [pl.BlockSpec((bm, k), lambda j, i: (i, 0))]
    if scaled:
        in_specs.append(pl.BlockSpec((bm, ROW_SCALE_W), lambda j, i: (i, 0)))
    in_specs.append(pl.BlockSpec((None, k, bn), lambda j, i: (layer, 0, j)))
    return pl.pallas_call(
        functools.partial(_mm_wcast_kernel, scaled),
        out_shape=jax.ShapeDtypeStruct((m, n), out_dtype),
        grid=(n // bn, m // bm),
        in_specs=in_specs,
        out_specs=pl.BlockSpec((bm, bn), lambda j, i: (i, j)),
        scratch_shapes=[pltpu.VMEM((k, bn), BF16)],
        compiler_params=_cparams("arbitrary", "arbitrary"),
        name=name,
    )(*((x, rs, w) if scaled else (x, w)))


def _swiglu_kernel(bn, scaled, x_ref, *refs):
    wg_ref, wu_ref, o_ref, w_scr = refs[-4:]

    @pl.when(pl.program_id(1) == 0)
    def _():
        w_scr[:, 0:bn] = wg_ref[...].astype(BF16)
        w_scr[:, bn:2 * bn] = wu_ref[...].astype(BF16)

    r = _dot(x_ref[...], w_scr[...])
    if scaled:
        r = r * refs[0][:, 0:1]
    o_ref[...] = (_silu(r[:, 0:bn]) * r[:, bn:2 * bn]).astype(o_ref.dtype)


def _swiglu_in(act, w, layer, bm=1024, bn=256):
    x, rs = act
    m, k = x.shape
    f = w.shape[2] // 2
    nb = f // bn
    scaled = rs is not None
    in_specs = [pl.BlockSpec((bm, k), lambda j, i: (i, 0))]
    if scaled:
        in_specs.append(pl.BlockSpec((bm, ROW_SCALE_W), lambda j, i: (i, 0)))
    in_specs += [pl.BlockSpec((None, k, bn), lambda j, i: (layer, 0, j)),
                 pl.BlockSpec((None, k, bn), lambda j, i: (layer, 0, j + nb))]
    return pl.pallas_call(
        functools.partial(_swiglu_kernel, bn, scaled),
        out_shape=jax.ShapeDtypeStruct((m, f), BF16),
        grid=(nb, m // bm),
        in_specs=in_specs,
        out_specs=pl.BlockSpec((bm, bn), lambda j, i: (i, j)),
        scratch_shapes=[pltpu.VMEM((k, 2 * bn), BF16)],
        compiler_params=_cparams("arbitrary", "arbitrary"),
        name="swiglu_in",
    )(*((x, rs, w, w) if scaled else (x, w, w)))


def _out_proj_kernel(nm, scale, emit_h, lhs_ref, w_ref, x_ref, gp_ref, gn_ref, xo_ref, *rest):
    if emit_h:
        ho_ref, rs_ref, y_scr, ssq_scr, ssq2_scr = rest
    else:
        y_scr, ssq_scr = rest
    i, n = pl.program_id(0), pl.program_id(1)
    cur = lax.rem(i, 2)
    prev = 1 - cur
    inv_d = 1.0 / (y_scr.shape[1] * y_scr.shape[3])

    def matmul_part():
        yb = _dot(lhs_ref[...], w_ref[...])
        y_scr[cur, n] = yb.astype(BF16)
        part = jnp.sum(yb * yb, axis=-1, keepdims=True)
        ssq_scr[cur] = jnp.where(n == 0, part, ssq_scr[cur] + part)

    def residual_part():
        r = lax.rsqrt(ssq_scr[prev] * inv_d + EPS)
        xn = x_ref[...] + scale * (y_scr[prev, n].astype(F32) * r * gp_ref[...])
        xo_ref[...] = xn
        if emit_h:
            ho_ref[...] = (xn * gn_ref[...]).astype(BF16)
            part = jnp.sum(xn * xn, axis=-1, keepdims=True)
            acc = jnp.where(n == 0, part, ssq2_scr[...] + part)
            ssq2_scr[...] = acc
            rs_ref[...] = jnp.broadcast_to(lax.rsqrt(acc * inv_d + EPS), rs_ref.shape)

    @pl.when(i == 0)
    def _():
        matmul_part()

    @pl.when((i > 0) & (i < nm))
    def _():
        residual_part()
        matmul_part()

    @pl.when(i == nm)
    def _():
        residual_part()


def _out_proj_post(lhs, w, layer, x, g_post, scale, g_next, name, bm=1024, bn=512):
    m, k = lhs.shape
    d = w.shape[2]
    nm, nn = m // bm, d // bn
    emit_h = g_next is not None
    gn = g_next if emit_h else g_post
    res_map = lambda i, n: (jnp.maximum(i - 1, 0), jnp.where(i == 0, 0, n))
    vec_map = lambda i, n: (0, jnp.where(i == 0, 0, n))
    res_spec = pl.BlockSpec((bm, bn), res_map)
    out_shape = [jax.ShapeDtypeStruct((m, d), F32)]
    out_specs = [res_spec]
    scratch = [pltpu.VMEM((2, nn, bm, bn), BF16), pltpu.VMEM((2, bm, 1), F32)]
    if emit_h:
        out_shape += [jax.ShapeDtypeStruct((m, d), BF16), jax.ShapeDtypeStruct((m, ROW_SCALE_W), F32)]
        out_specs += [res_spec, pl.BlockSpec((bm, ROW_SCALE_W), lambda i, n: (jnp.maximum(i - 1, 0), 0))]
        scratch.append(pltpu.VMEM((bm, 1), F32))
    res = pl.pallas_call(
        functools.partial(_out_proj_kernel, nm, scale, emit_h),
        out_shape=out_shape,
        grid=(nm + 1, nn),
        in_specs=[pl.BlockSpec((bm, k), lambda i, n: (jnp.minimum(i, nm - 1), 0)),
                  pl.BlockSpec((None, k, bn), lambda i, n: (layer, 0, n)),
                  res_spec,
                  pl.BlockSpec((1, bn), vec_map),
                  pl.BlockSpec((1, bn), vec_map)],
        out_specs=out_specs,
        scratch_shapes=scratch,
        compiler_params=_cparams("arbitrary", "arbitrary"),
        name=name,
    )(lhs, w, x, g_post.reshape(1, d), gn.reshape(1, d))
    return (res[0], (res[1], res[2])) if emit_h else (res[0], None)


def _gla_kernel(q_ref, k_ref, v_ref, r_ref, sm_ref, wgk_ref, bgk_ref, ng_ref, o_ref, cum_ref, *st_refs):
    @pl.when(pl.program_id(1) == 0)
    def _():
        for st_ref in st_refs:
            st_ref[...] = jnp.zeros_like(st_ref)

    rows = q_ref.shape[0]
    a_hi, a_mid, _ = _split3(sm_ref[...])
    w_hi, w_mid, _ = _split3(wgk_ref[...])
    z = _dot(a_hi, w_hi) + _dot(a_hi, w_mid) + _dot(a_mid, w_hi) + bgk_ref[...]
    logf = (jnp.minimum(z, 0.0) - jnp.log1p(jnp.exp(-jnp.abs(z)))) * (1.0 / GLA_TAU)
    tri = _chunk_tril(CUM_BLOCK)
    for b in range(rows // CUM_BLOCK):
        sl = slice(b * CUM_BLOCK, (b + 1) * CUM_BLOCK)
        cum_ref[sl, :] = _sel_dot(tri, logf[sl])

    ri = lax.broadcasted_iota(jnp.int32, (CHUNK, CHUNK), 0)
    ci = lax.broadcasted_iota(jnp.int32, (CHUNK, CHUNK), 1)
    tril = ri >= ci
    for c in range(rows // CHUNK):
        sl = pl.ds(c * CHUNK, CHUNK)
        for hh, st_ref in enumerate(st_refs):
            kl = slice(hh * GLA_DK, (hh + 1) * GLA_DK)
            vl = slice(hh * GLA_DV, (hh + 1) * GLA_DV)
            cum = cum_ref[sl, kl]
            cum_end = cum[CHUNK - 1:CHUNK, :]
            up, down = jnp.exp(cum), jnp.exp(-cum)
            q = q_ref[sl, kl].astype(F32) * (GLA_DK ** -0.5)
            k = k_ref[sl, kl].astype(F32)
            v = v_ref[sl, vl]
            q_dec = (q * up).astype(BF16)
            a_past = _dot_nt(q_dec, (k * down).astype(BF16))
            a_future = _dot_nt((q * down).astype(BF16), (k * up).astype(BF16))
            attn = jnp.where(tril, a_past, a_future).astype(BF16)
            state_t = st_ref[...]
            o = _dot(attn, v) + _dot_nt(q_dec, state_t.astype(BF16))
            contrib_t = _dot_tn(v, (k * jnp.exp(cum_end - cum)).astype(BF16))
            st_ref[...] = state_t * jnp.exp(cum_end) + contrib_t
            o = o * lax.rsqrt(jnp.mean(o * o, axis=-1, keepdims=True) + EPS) * ng_ref[:, vl]
            o_ref[sl, vl] = (o * _silu(r_ref[sl, vl].astype(F32))).astype(o_ref.dtype)


def _gla(proj, small, w_gk_pad, b_gk, norm_g):
    s = proj.shape[0]
    rows = min(MIX_ROWS, s)
    hp = GLA_HEADS_PER_STEP
    kw, vw = hp * GLA_DK, hp * GLA_DV
    qb, vb = P_GLA_Q // kw, P_GLA_V // vw
    kb, rb = P_GLA_K // kw, P_GLA_R // vw
    return pl.pallas_call(
        _gla_kernel,
        out_shape=jax.ShapeDtypeStruct((s, GLA_V), BF16),
        grid=(GLA_HEADS // hp, s // rows),
        in_specs=[pl.BlockSpec((rows, kw), lambda h, i: (i, qb + h)),
                  pl.BlockSpec((rows, kw), lambda h, i: (i, kb + h)),
                  pl.BlockSpec((rows, vw), lambda h, i: (i, vb + h)),
                  pl.BlockSpec((rows, vw), lambda h, i: (i, rb + h)),
                  pl.BlockSpec((rows, SMALL_W), lambda h, i: (i, 0)),
                  pl.BlockSpec((SMALL_W, kw), lambda h, i: (0, h)),
                  pl.BlockSpec((1, kw), lambda h, i: (0, h)),
                  pl.BlockSpec((1, vw), lambda h, i: (0, h))],
        out_specs=pl.BlockSpec((rows, vw), lambda h, i: (i, h)),
        scratch_shapes=[pltpu.VMEM((rows, kw), F32)] + [pltpu.VMEM((GLA_DV, GLA_DK), F32)] * hp,
        compiler_params=_cparams("parallel", "arbitrary"),
        name="gla",
    )(proj, proj, proj, proj, small, w_gk_pad, b_gk.reshape(1, GLA_QK), norm_g.reshape(1, GLA_V))


def _causal_conv_silu(xp_ref, x_ref, w_ref, b_ref, first):
    rows = x_ref.shape[0]

    @pl.when(first)
    def _():
        xp_ref[pl.ds(0, 8), :] = jnp.zeros((8, xp_ref.shape[1]), F32)

    xp_ref[pl.ds(8, rows), :] = x_ref[...].astype(F32)
    acc = b_ref[...]
    for tap in range(SSM_CONV):
        acc = acc + xp_ref[pl.ds(8 - (SSM_CONV - 1) + tap, rows), :] * w_ref[tap:tap + 1, :]
    xp_ref[pl.ds(0, 8), :] = xp_ref[pl.ds(rows, 8), :]
    return _silu(acc)


def _ssd_kernel(z_ref, xs_ref, b_ref, c_ref, sm_ref, e_ref, cwx_ref, cwb_ref, cwc_ref,
                cbx_ref, cbb_ref, cbc_ref, dtb_ref, alog_ref, dsk_ref, ng_ref,
                o_ref, xpx_ref, xpb_ref, xpc_ref, xs_scr, b_scr, c_scr, dt_scr, cum_scr, *s_refs):
    first = pl.program_id(1) == 0

    @pl.when(first)
    def _():
        for s_ref in s_refs:
            s_ref[...] = jnp.zeros_like(s_ref)

    rows = xs_ref.shape[0]
    xs_scr[...] = _causal_conv_silu(xpx_ref, xs_ref, cwx_ref, cbx_ref, first)
    b_scr[...] = _causal_conv_silu(xpb_ref, b_ref, cwb_ref, cbb_ref, first).astype(BF16)
    c_scr[...] = _causal_conv_silu(xpc_ref, c_ref, cwc_ref, cbc_ref, first).astype(BF16)

    dt_in = sm_ref[...] + dtb_ref[...]
    dt_c = jnp.maximum(dt_in, 0.0) + jnp.log1p(jnp.exp(-jnp.abs(dt_in)))
    da_c = dt_c * (-jnp.exp(alog_ref[...]))
    tri = _chunk_tril(CUM_BLOCK)
    cum_c = jnp.concatenate([_sel_dot(tri, da_c[b * CUM_BLOCK:(b + 1) * CUM_BLOCK])
                             for b in range(rows // CUM_BLOCK)], axis=0)
    dt_scr[...] = _sel_dot_rhs(dt_c, e_ref[0])
    cum_scr[...] = _sel_dot_rhs(cum_c, e_ref[0])

    gw = SSM_GW
    rw = lax.broadcasted_iota(jnp.int32, (CHUNK, gw), 0)
    cw = lax.broadcasted_iota(jnp.int32, (CHUNK, gw), 1)
    diag_tile = (jnp.bitwise_and(cw, SSM_HEADDIM - 1) == rw).astype(F32)
    half = gw // 2
    rb = lax.broadcasted_iota(jnp.int32, (half, half), 0)
    cb_ = lax.broadcasted_iota(jnp.int32, (half, half), 1)
    head_mask = jnp.right_shift(rb, 6) == jnp.right_shift(cb_, 6)
    reps = half // CHUNK
    for c in range(rows // CHUNK):
        sl = pl.ds(c * CHUNK, CHUNK)
        for gg, s_ref in enumerate(s_refs):
            xl = slice(gg * gw, (gg + 1) * gw)
            nl = slice(gg * SSM_STATE, (gg + 1) * SSM_STATE)
            cum = cum_scr[sl, xl]
            cum_row = jnp.sum(cum * diag_tile, axis=0, keepdims=True)
            seg = jnp.exp(-jnp.abs(cum - cum_row))
            xs = xs_scr[sl, xl]
            bm = b_scr[sl, nl]
            cm = c_scr[sl, nl]
            xdt = xs * dt_scr[sl, xl]
            cbt = _dot_nt(cm, jnp.concatenate([bm] * SSM_HPG, axis=0))
            m_all = (cbt * seg).astype(BF16)
            intra = []
            for hlf in range(2):
                xh = xdt[:, hlf * half:(hlf + 1) * half]
                bd = jnp.where(head_mask, jnp.concatenate([xh] * reps, axis=0), 0.0).astype(BF16)
                intra.append(_dot(m_all[:, hlf * half:(hlf + 1) * half], bd))
            intra = jnp.concatenate(intra, axis=1)
            cum_end = cum[CHUNK - 1:CHUNK, :]
            state = s_ref[...]
            inter = _dot(cm, state.astype(BF16)) * jnp.exp(cum)
            contrib = _dot_tn(bm, (xdt * jnp.exp(cum_end - cum)).astype(BF16))
            s_ref[...] = state * jnp.exp(cum_end) + contrib
            y = intra + inter + xs * dsk_ref[:, xl]
            y = y * _silu(z_ref[sl, xl].astype(F32))
            y = y * lax.rsqrt(jnp.mean(y * y, axis=-1, keepdims=True) + EPS) * ng_ref[:, xl]
            o_ref[sl, xl] = y.astype(o_ref.dtype)


def _sel_dot_rhs(x, sel):
    hi, mid, lo = _split3(x)
    return _dot(hi, sel) + _dot(mid, sel) + _dot(lo, sel)


def _ssd(proj, small, expand, conv_w, conv_b, dt_bias, a_log, d_skip, norm_g):
    s = proj.shape[0]
    rows = min(MIX_ROWS, s)
    gp = SSD_GROUPS_PER_STEP
    xw, nw = gp * SSM_GW, gp * SSM_STATE
    zb, xb = P_SSM_Z // xw, P_SSM_X // xw
    bb, cb = P_SSM_B // nw, P_SSM_C // nw
    cwb, cwc = SSM_INNER // nw, (SSM_INNER + SSM_BC) // nw

    def lane_expand(p):
        return jnp.repeat(p.astype(F32), SSM_HEADDIM).reshape(1, SSM_INNER)

    def side_lanes(p):
        return jnp.pad(p.astype(F32), (SMALL_DT0, SMALL_W - SMALL_DT0 - SSM_HEADS)).reshape(1, SMALL_W)

    conv_b2 = conv_b.reshape(1, SSM_CONV_DIM)
    vec = pl.BlockSpec((1, xw), lambda g, i: (0, g))
    side_vec = pl.BlockSpec((1, SMALL_W), lambda g, i: (0, 0))
    return pl.pallas_call(
        _ssd_kernel,
        out_shape=jax.ShapeDtypeStruct((s, SSM_INNER), BF16),
        grid=(SSM_GROUPS // gp, s // rows),
        in_specs=[pl.BlockSpec((rows, xw), lambda g, i: (i, zb + g)),
                  pl.BlockSpec((rows, xw), lambda g, i: (i, xb + g)),
                  pl.BlockSpec((rows, nw), lambda g, i: (i, bb + g)),
                  pl.BlockSpec((rows, nw), lambda g, i: (i, cb + g)),
                  pl.BlockSpec((rows, SMALL_W), lambda g, i: (i, 0)),
                  pl.BlockSpec((1, SMALL_W, xw), lambda g, i: (g, 0, 0)),
                  pl.BlockSpec((SSM_CONV, xw), lambda g, i: (0, g)),
                  pl.BlockSpec((SSM_CONV, nw), lambda g, i: (0, cwb + g)),
                  pl.BlockSpec((SSM_CONV, nw), lambda g, i: (0, cwc + g)),
                  pl.BlockSpec((1, xw), lambda g, i: (0, g)),
                  pl.BlockSpec((1, nw), lambda g, i: (0, cwb + g)),
                  pl.BlockSpec((1, nw), lambda g, i: (0, cwc + g)),
                  side_vec, side_vec, vec, vec],
        out_specs=pl.BlockSpec((rows, xw), lambda g, i: (i, g)),
        scratch_shapes=[pltpu.VMEM((rows + 8, xw), F32),
                        pltpu.VMEM((rows + 8, nw), F32),
                        pltpu.VMEM((rows + 8, nw), F32),
                        pltpu.VMEM((rows, xw), F32),
                        pltpu.VMEM((rows, nw), BF16),
                        pltpu.VMEM((rows, nw), BF16),
                        pltpu.VMEM((rows, xw), F32),
                        pltpu.VMEM((rows, xw), F32)] + [pltpu.VMEM((SSM_STATE, SSM_GW), F32)] * gp,
        compiler_params=_cparams("parallel", "arbitrary"),
        name="ssd",
    )(proj, proj, proj, proj, small, expand, conv_w, conv_w, conv_w, conv_b2, conv_b2, conv_b2,
      side_lanes(dt_bias), side_lanes(a_log), lane_expand(d_skip), norm_g.reshape(1, SSM_INNER))


def _ssd_expand_table():
    xw = SSD_GROUPS_PER_STEP * SSM_GW
    rows = jnp.arange(SMALL_W, dtype=jnp.int32)[None, :, None]
    cols = jnp.arange(xw, dtype=jnp.int32)[None, None, :]
    step = jnp.arange(SSM_INNER // xw, dtype=jnp.int32)[:, None, None]
    return (rows == SMALL_DT0 + (step * xw + cols) // SSM_HEADDIM).astype(BF16)


def _rope_table_kernel(pos_ref, freq_ref, sign_ref, cos_ref, sin_ref):
    ang = pos_ref[...] * freq_ref[...]
    cos_ref[...] = jnp.cos(ang)
    sin_ref[...] = jnp.sin(ang) * sign_ref[...]


def _rope_tables(positions):
    s = positions.shape[-1]
    half = RET_DK // 2
    inv_freq = ROPE_BASE ** (-jnp.arange(half, dtype=F32) / half)
    freq = jnp.concatenate([inv_freq, inv_freq]).reshape(1, RET_DK)
    sign = jnp.concatenate([-jnp.ones((half,), F32), jnp.ones((half,), F32)]).reshape(1, RET_DK)
    pos = positions.astype(F32).reshape(s, 1)
    rows = min(1024, s)
    tab = jax.ShapeDtypeStruct((s, RET_DK), F32)
    return pl.pallas_call(
        _rope_table_kernel,
        out_shape=[tab, tab],
        grid=(s // rows,),
        in_specs=[pl.BlockSpec((rows, 1), lambda i: (i, 0)),
                  pl.BlockSpec((1, RET_DK), lambda i: (0, 0)),
                  pl.BlockSpec((1, RET_DK), lambda i: (0, 0))],
        out_specs=[pl.BlockSpec((rows, RET_DK), lambda i: (i, 0))] * 2,
        compiler_params=_cparams("parallel"),
        name="rope_tables",
    )(pos, freq, sign)


def _ret_kernel(q_ref, k_ref, v_ref, g_ref, cos_ref, sin_ref, dmat_ref, qdec_ref, kdec_ref, sdec_ref,
                ng_ref, o_ref, *s_refs):
    @pl.when(pl.program_id(1) == 0)
    def _():
        for s_ref in s_refs:
            s_ref[...] = jnp.zeros_like(s_ref)

    cos, sin = cos_ref[...], sin_ref[...]

    def rot(t):
        t = t.astype(F32)
        return t * cos + pltpu.roll(t, RET_DK // 2, axis=1) * sin

    for hh, s_ref in enumerate(s_refs):
        kl = slice(hh * RET_DK, (hh + 1) * RET_DK)
        vl = slice(hh * RET_DV, (hh + 1) * RET_DV)
        q = rot(q_ref[:, kl])
        k = rot(k_ref[:, kl]) * (RET_DK ** -0.5)
        v = v_ref[:, vl]
        scores = (_dot_nt(q.astype(BF16), k.astype(BF16)) * dmat_ref[hh]).astype(BF16)
        state = s_ref[...]
        o = _dot(scores, v) + _dot((q * qdec_ref[hh]).astype(BF16), state.astype(BF16))
        s_ref[...] = state * sdec_ref[hh] + _dot_tn((k * kdec_ref[hh]).astype(BF16), v)
        o = o * lax.rsqrt(jnp.mean(o * o, axis=-1, keepdims=True) + EPS) * ng_ref[:, vl]
        o_ref[:, vl] = (o * _silu(g_ref[:, vl].astype(F32))).astype(o_ref.dtype)


def _ret_tables(rows):
    log_gamma = jnp.log1p(-jnp.exp2(-5.0 - jnp.arange(RET_HEADS, dtype=F32)))[:, None, None]
    i = jnp.arange(rows, dtype=jnp.int32)
    dist = (i[:, None] - i[None, :]).astype(F32)[None]
    ch_i, ch_j = (i // CHUNK)[:, None], (i // CHUNK)[None, :]
    dmat = jnp.where((ch_i == ch_j)[None], jnp.exp(log_gamma * jnp.abs(dist)),
                     jnp.where((ch_j < ch_i)[None], jnp.exp(log_gamma * dist), 0.0))
    fi = i.astype(F32)[None, :, None]
    qdec = jnp.broadcast_to(jnp.exp(log_gamma * (fi + 1.0)), (RET_HEADS, rows, RET_DK))
    kdec = jnp.broadcast_to(jnp.exp(log_gamma * (rows - 1.0 - fi)), (RET_HEADS, rows, RET_DK))
    sdec = jnp.broadcast_to(jnp.exp(log_gamma * float(rows)), (RET_HEADS, 1, RET_DV))
    return dmat, qdec, kdec, sdec


def _retention(proj, cos_t, sin_t, tables, norm_g):
    s = proj.shape[0]
    rows = min(RET_BLOCK, s)
    dmat, qdec, kdec, sdec = tables
    hp = RET_HEADS_PER_STEP
    kw, vw = hp * RET_DK, hp * RET_DV
    qb, kb = P_RET_Q // kw, P_RET_K // kw
    vb, gb = P_RET_V // vw, P_RET_G // vw
    return pl.pallas_call(
        _ret_kernel,
        out_shape=jax.ShapeDtypeStruct((s, RET_V), BF16),
        grid=(RET_HEADS // hp, s // rows),
        in_specs=[pl.BlockSpec((rows, kw), lambda h, i: (i, qb + h)),
                  pl.BlockSpec((rows, kw), lambda h, i: (i, kb + h)),
                  pl.BlockSpec((rows, vw), lambda h, i: (i, vb + h)),
                  pl.BlockSpec((rows, vw), lambda h, i: (i, gb + h)),
                  pl.BlockSpec((rows, RET_DK), lambda h, i: (i, 0)),
                  pl.BlockSpec((rows, RET_DK), lambda h, i: (i, 0)),
                  pl.BlockSpec((hp, rows, rows), lambda h, i: (h, 0, 0)),
                  pl.BlockSpec((hp, rows, RET_DK), lambda h, i: (h, 0, 0)),
                  pl.BlockSpec((hp, rows, RET_DK), lambda h, i: (h, 0, 0)),
                  pl.BlockSpec((hp, 1, RET_DV), lambda h, i: (h, 0, 0)),
                  pl.BlockSpec((1, vw), lambda h, i: (0, h))],
        out_specs=pl.BlockSpec((rows, vw), lambda h, i: (i, h)),
        scratch_shapes=[pltpu.VMEM((RET_DK, RET_DV), F32)] * hp,
        compiler_params=_cparams("parallel", "arbitrary"),
        name="retention",
    )(proj, proj, proj, proj, cos_t, sin_t, dmat, qdec, kdec, sdec, norm_g.reshape(1, RET_V))


def _merge_kernel(gl_ref, y0_ref, y1_ref, y2_ref, gu0_ref, gu1_ref, gu2_ref, gb0_ref, gb1_ref, gb2_ref,
                  wo0_ref, wo1_ref, wo2_ref, o_ref):
    gl = gl_ref[...]
    acc = None
    for y_ref, gu_ref, gb_ref, wo_ref in ((y0_ref, gu0_ref, gb0_ref, wo0_ref),
                                          (y1_ref, gu1_ref, gb1_ref, wo1_ref),
                                          (y2_ref, gu2_ref, gb2_ref, wo2_ref)):
        gate = jax.nn.sigmoid(_dot(gl, gu_ref[...]) + gb_ref[...])
        term = gate * _dot(y_ref[...], wo_ref[...])
        acc = term if acc is None else acc + term
    o_ref[...] = acc.astype(o_ref.dtype)


def _merge(proj, ys, gate_w2, gate_b, w_out, layer, bm=1024, bn=512):
    s = proj.shape[0]
    bm = min(bm, s)
    d = D_MODEL
    nb = d // bn
    gate_b4 = gate_b.reshape(DEPTH, N_BRANCH, 1, d)
    y_spec = pl.BlockSpec((bm, BRANCH_W), lambda i, j: (i, 0))
    in_specs = [pl.BlockSpec((bm, GATE_RANK), lambda i, j: (i, P_GATE // GATE_RANK)), y_spec, y_spec, y_spec]
    in_specs += [pl.BlockSpec((None, GATE_RANK, bn),
                              functools.partial(lambda n, i, j: (layer * N_BRANCH + n, 0, j), n))
                 for n in range(N_BRANCH)]
    in_specs += [pl.BlockSpec((None, None, 1, bn), functools.partial(lambda n, i, j: (layer, n, 0, j), n))
                 for n in range(N_BRANCH)]
    in_specs += [pl.BlockSpec((None, BRANCH_W, bn),
                              functools.partial(lambda n, i, j: (layer * N_BRANCH + n, 0, j), n))
                 for n in range(N_BRANCH)]
    return pl.pallas_call(
        _merge_kernel,
        out_shape=jax.ShapeDtypeStruct((s, d), BF16),
        grid=(s // bm, nb),
        in_specs=in_specs,
        out_specs=pl.BlockSpec((bm, bn), lambda i, j: (i, j)),
        compiler_params=_cparams("parallel", "parallel"),
        name="gated_merge",
    )(proj, ys[0], ys[1], ys[2], gate_w2, gate_w2, gate_w2, gate_b4, gate_b4, gate_b4, w_out, w_out, w_out)


def _xattn_kernel(q_ref, kv_ref, o_ref):
    for h in range(XA_HEADS):
        lo, hi = h * XA_DIM, (h + 1) * XA_DIM
        s = _dot_nt(q_ref[:, lo:hi], kv_ref[:, lo:hi]) * (XA_DIM ** -0.5)
        p = jnp.exp(s - jnp.max(s, axis=-1, keepdims=True))
        p = p / jnp.sum(p, axis=-1, keepdims=True)
        o_ref[:, lo:hi] = _dot(p.astype(BF16), kv_ref[:, XA_W + lo:XA_W + hi]).astype(o_ref.dtype)


def _xattn(q, kv, rows=512):
    s = q.shape[0]
    rows = min(rows, s)
    n_mem = kv.shape[0]
    return pl.pallas_call(
        _xattn_kernel,
        out_shape=jax.ShapeDtypeStruct((s, XA_W), BF16),
        grid=(s // rows,),
        in_specs=[pl.BlockSpec((rows, XA_W), lambda i: (i, 0)),
                  pl.BlockSpec((n_mem, 2 * XA_W), lambda i: (0, 0))],
        out_specs=pl.BlockSpec((rows, XA_W), lambda i: (i, 0)),
        compiler_params=_cparams("parallel"),
        name="xattn",
    )(q, kv)


def kernel(x, mem, positions, norm_g, mem_norm_g, ffn1_w_in, ffn1_w_out, mix_w_in, gla_w_gk, gla_b_gk,
           gla_norm_g, ssm_conv_w, ssm_conv_b, ssm_dt_bias, ssm_a_log, ssm_d, ssm_norm_g, ret_norm_g,
           gate_w_up, gate_b, mix_w_out, xattn_w_q, xattn_w_kv, xattn_w_o, ffn2_w_in, ffn2_w_out):
    b, s, d = x.shape
    assert b == 1 and d == D_MODEL and s % max(RET_BLOCK, MIX_ROWS, 1024) == 0
    nl = DEPTH
    xr = x.reshape(s, d)
    memr = mem.reshape(mem.shape[1], d)
    cos_t, sin_t = _rope_tables(positions)
    ret_tabs = _ret_tables(min(RET_BLOCK, s))
    expand = _ssd_expand_table()

    ffn1_out = _cast_bf16(ffn1_w_out.reshape(nl * D_FF, d), 512).reshape(nl, D_FF, d)
    ffn2_out = _cast_bf16(ffn2_w_out.reshape(nl * D_FF, d), 512).reshape(nl, D_FF, d)
    xa_out = _cast_bf16(xattn_w_o.reshape(nl * XA_W, d), 512).reshape(nl, XA_W, d)
    mix_out = _cast_bf16(mix_w_out.reshape(nl * N_BRANCH * BRANCH_W, d), 512).reshape(nl * N_BRANCH, BRANCH_W, d)
    gate_up = _cast_bf16(jnp.transpose(gate_w_up, (0, 2, 1, 3)).reshape(nl * N_BRANCH * GATE_RANK, d),
                         512).reshape(nl * N_BRANCH, GATE_RANK, d)
    mix_w_in_t = jnp.transpose(mix_w_in, (0, 2, 1))
    mix_in_big = _repack_mix_w_in(mix_w_in_t)
    mix_in_side_t = _side_mix_w_in(mix_w_in_t)
    w_gk_pad = jnp.concatenate([gla_w_gk, jnp.zeros((nl, SMALL_W - GLA_RANK, GLA_QK), F32)], axis=1)

    h = (_norm_cast(xr, norm_g[0, 0]), None)
    for l in range(nl):
        ng = norm_g[l]
        a = _swiglu_in(h, ffn1_w_in, l)
        xr, h = _out_proj_post(a, ffn1_out, l, xr, ng[1], 0.5, ng[2], "ffn_out_post")
        proj = _matmul(h, mix_in_big, l, BF16, 1024, 1280, "mix_in")
        small = _matmul_nt_small(h, mix_in_side_t, l, F32, 1024, "mix_in_small")
        y_gla = _gla(proj, small, w_gk_pad[l], gla_b_gk[l], gla_norm_g[l])
        y_ssd = _ssd(proj, small, expand, ssm_conv_w[l], ssm_conv_b[l], ssm_dt_bias[l], ssm_a_log[l],
                     ssm_d[l], ssm_norm_g[l])
        y_ret = _retention(proj, cos_t, sin_t, ret_tabs, ret_norm_g[l])
        mixed = _merge(proj, (y_gla, y_ssd, y_ret), gate_up, gate_b, mix_out, l)
        xr, hn = _post(mixed, xr, ng[3], 1.0, ng[4])
        q = _matmul_wcast((hn, None), xattn_w_q, l, BF16, 1024, 512, "xattn_q")
        mem_n = _norm_cast(memr, mem_norm_g[l])
        kv = _matmul_wcast((mem_n, None), xattn_w_kv, l, BF16, 256, 512, "xattn_kv")
        o = _xattn(q, kv)
        xr, h = _out_proj_post(o, xa_out, l, xr, ng[5], 1.0, ng[6], "xattn_out_post")
        a = _swiglu_in(h, ffn2_w_in, l)
        g_next = norm_g[l + 1, 0] if l + 1 < nl else None
        xr, h = _out_proj_post(a, ffn2_out, l, xr, ng[7], 0.5, g_next, "ffn_out_post")
    return xr.reshape(b, s, d)
```

```python
import functools

import numpy as np
import jax
import jax.numpy as jnp
from jax import lax
from jax.experimental import pallas as pl
from jax.experimental.pallas import tpu as pltpu

F32 = jnp.float32
BF16 = jnp.bfloat16

D_MODEL = 4096
DEPTH = 4
CHUNK = 64
D_FF = 4096
EPS = 1e-6
GLA_HEADS, GLA_DK, GLA_DV, GLA_RANK, GLA_TAU = 4, 256, 512, 16, 16.0
GLA_QK, GLA_V = GLA_HEADS * GLA_DK, GLA_HEADS * GLA_DV
SSM_HEADS, SSM_HEADDIM, SSM_GROUPS, SSM_STATE, SSM_CONV = 32, 64, 4, 128, 4
SSM_INNER = SSM_HEADS * SSM_HEADDIM
SSM_BC = SSM_GROUPS * SSM_STATE
SSM_CONV_DIM = SSM_INNER + 2 * SSM_BC
SSM_HPG = SSM_HEADS // SSM_GROUPS
SSM_GW = SSM_INNER // SSM_GROUPS
RET_HEADS, RET_DK, RET_DV = 8, 128, 256
RET_QK, RET_V = RET_HEADS * RET_DK, RET_HEADS * RET_DV
ROPE_BASE = 10000.0
XA_HEADS, XA_DIM = 4, 256
XA_W = XA_HEADS * XA_DIM
N_BRANCH, BRANCH_W, GATE_RANK = 3, 2048, 512

_IN_WIDTHS = (GLA_QK, GLA_QK, GLA_V, GLA_V, GLA_RANK, SSM_INNER, SSM_CONV_DIM, SSM_HEADS,
              RET_QK, RET_QK, RET_V, RET_V, GATE_RANK)
_IN_OFF = np.concatenate([[0], np.cumsum(_IN_WIDTHS)]).tolist()
IN_COLS = _IN_OFF[-1]
P_GLA_Q, P_GLA_K, P_GLA_V, P_GLA_R = 0, 1024, 2048, 4096
P_SSM_Z, P_SSM_X, P_SSM_B, P_SSM_C = 6144, 8192, 10240, 10752
P_RET_Q, P_RET_K, P_RET_V, P_RET_G = 11264, 12288, 13312, 15360
P_GATE = 17408
P_COLS = 17920
SMALL_W = 128
SMALL_DT0 = GLA_RANK

V7X_VMEM_LIMIT = 60 * 1024 * 1024
V7X_SUBLANES = 8
CHUNK_SHIFT = CHUNK.bit_length() - 1
HEAD_SHIFT = SSM_HEADDIM.bit_length() - 1

RET_BLOCK = 512
MIX_ROWS = 512
CUM_BLOCK = 256
GLA_HEADS_PER_STEP = 4
SSD_GROUPS_PER_STEP = 2
RET_HEADS_PER_STEP = 4


def _cparams(*sem):
    return pltpu.CompilerParams(dimension_semantics=sem, vmem_limit_bytes=V7X_VMEM_LIMIT)


def _silu(x):
    return x * jax.nn.sigmoid(x)


def _dot(a, b):
    return jnp.dot(a, b, preferred_element_type=F32)


def _dot_nt(a, b):
    return lax.dot_general(a, b, (((1,), (1,)), ((), ())), preferred_element_type=F32)


def _dot_tn(a, b):
    return lax.dot_general(a, b, (((0,), (0,)), ((), ())), preferred_element_type=F32)


def _split3(x):
    hi = x.astype(BF16)
    r1 = x - hi.astype(F32)
    mid = r1.astype(BF16)
    lo = (r1 - mid.astype(F32)).astype(BF16)
    return hi, mid, lo


def _sel_dot(sel, x):
    hi, mid, lo = _split3(x)
    return _dot(sel, hi) + _dot(sel, mid) + _dot(sel, lo)


def _sel_dot_rhs(x, sel):
    hi, mid, lo = _split3(x)
    return _dot(hi, sel) + _dot(mid, sel) + _dot(lo, sel)


def _chunk_tril(n):
    ri = lax.broadcasted_iota(jnp.int32, (n, n), 0)
    ci = lax.broadcasted_iota(jnp.int32, (n, n), 1)
    same = jnp.right_shift(ri, CHUNK_SHIFT) == jnp.right_shift(ci, CHUNK_SHIFT)
    return jnp.where(same & (ri >= ci), 1.0, 0.0).astype(BF16)


def _cast_kernel(w_ref, o_ref):
    o_ref[...] = w_ref[...].astype(o_ref.dtype)


def _cast_bf16(w, rows):
    r, c = w.shape
    return pl.pallas_call(
        _cast_kernel,
        out_shape=jax.ShapeDtypeStruct((r, c), BF16),
        grid=(r // rows,),
        in_specs=[pl.BlockSpec((rows, c), lambda i: (i, 0))],
        out_specs=pl.BlockSpec((rows, c), lambda i: (i, 0)),
        compiler_params=_cparams("parallel"),
        name="cast_bf16",
    )(w)


REPACK_COLS = 512


def _packed_to_source_col(c):
    skipped = jnp.where(c >= P_SSM_Z, GLA_RANK, 0) + jnp.where(c >= P_RET_Q, SSM_HEADS, 0)
    return pl.multiple_of(c + skipped, GLA_RANK)


def _repack_kernel(w_ref, o_ref):
    o_ref[...] = w_ref[...].T.astype(o_ref.dtype)


def _repack_mix_w_in(w_t):
    nl, _, d = w_t.shape
    return pl.pallas_call(
        _repack_kernel,
        out_shape=jax.ShapeDtypeStruct((nl, d, P_COLS), BF16),
        grid=(nl, P_COLS // REPACK_COLS),
        in_specs=[pl.BlockSpec((pl.Squeezed(), pl.Element(REPACK_COLS), pl.Element(d)),
                               lambda l, j: (l, _packed_to_source_col(j * REPACK_COLS), 0))],
        out_specs=pl.BlockSpec((None, d, REPACK_COLS), lambda l, j: (l, 0, j)),
        compiler_params=_cparams("parallel", "parallel"),
        name="repack_mix_w_in",
    )(w_t)


def _side_w_kernel(gk_ref, dt_ref, o_ref):
    o_ref[...] = jnp.zeros(o_ref.shape, o_ref.dtype)
    o_ref[0:GLA_RANK, :] = gk_ref[...].astype(o_ref.dtype)
    o_ref[GLA_RANK:GLA_RANK + SSM_HEADS, :] = dt_ref[...].astype(o_ref.dtype)


def _side_mix_w_in(w_t):
    nl, _, d = w_t.shape
    o = _IN_OFF
    return pl.pallas_call(
        _side_w_kernel,
        out_shape=jax.ShapeDtypeStruct((nl, SMALL_W, d), BF16),
        grid=(nl,),
        in_specs=[pl.BlockSpec((pl.Squeezed(), pl.Element(GLA_RANK), pl.Element(d)), lambda l: (l, o[4], 0)),
                  pl.BlockSpec((pl.Squeezed(), pl.Element(SSM_HEADS), pl.Element(d)), lambda l: (l, o[7], 0))],
        out_specs=pl.BlockSpec((None, SMALL_W, d), lambda l: (l, 0, 0)),
        compiler_params=_cparams("parallel"),
        name="side_mix_w_in",
    )(w_t, w_t)


def _norm_cast_kernel(x_ref, g_ref, o_ref):
    x = x_ref[...]
    r = lax.rsqrt(jnp.mean(x * x, axis=-1, keepdims=True) + EPS)
    o_ref[...] = (x * r * g_ref[...]).astype(o_ref.dtype)


def _norm_cast(x, g, rows=256):
    m, d = x.shape
    rows = min(rows, m)
    return pl.pallas_call(
        _norm_cast_kernel,
        out_shape=jax.ShapeDtypeStruct((m, d), BF16),
        grid=(m // rows,),
        in_specs=[pl.BlockSpec((rows, d), lambda i: (i, 0)), pl.BlockSpec((1, d), lambda i: (0, 0))],
        out_specs=pl.BlockSpec((rows, d), lambda i: (i, 0)),
        compiler_params=_cparams("parallel"),
        name="norm_cast",
    )(x, g.reshape(1, d))


def _post_kernel(scale, emit_h, y_ref, x_ref, gp_ref, gn_ref, xo_ref, *maybe_h):
    y = y_ref[...].astype(F32)
    r = lax.rsqrt(jnp.mean(y * y, axis=-1, keepdims=True) + EPS)
    xn = x_ref[...] + scale * (y * r * gp_ref[...])
    xo_ref[...] = xn
    if emit_h:
        r2 = lax.rsqrt(jnp.mean(xn * xn, axis=-1, keepdims=True) + EPS)
        maybe_h[0][...] = (xn * r2 * gn_ref[...]).astype(BF16)


def _post(y, x, g_post, scale, g_next, rows=256):
    m, d = x.shape
    emit_h = g_next is not None
    gn = g_next if emit_h else g_post
    row_spec = pl.BlockSpec((rows, d), lambda i: (i, 0))
    vec_spec = pl.BlockSpec((1, d), lambda i: (0, 0))
    out_shape = [jax.ShapeDtypeStruct((m, d), F32)]
    out_specs = [row_spec]
    if emit_h:
        out_shape.append(jax.ShapeDtypeStruct((m, d), BF16))
        out_specs.append(row_spec)
    res = pl.pallas_call(
        functools.partial(_post_kernel, scale, emit_h),
        out_shape=out_shape,
        grid=(m // rows,),
        in_specs=[row_spec, row_spec, vec_spec, vec_spec],
        out_specs=out_specs,
        compiler_params=_cparams("parallel"),
        name="post_norm",
    )(y, x, g_post.reshape(1, d), gn.reshape(1, d))
    return (res[0], res[1]) if emit_h else (res[0], None)


ROW_SCALE_W = 128


def _mix_in_kernel(x_ref, rs_ref, w_ref, wside_ref, o_ref, side_ref):
    x = x_ref[...]
    rs = rs_ref[:, 0:1]
    o_ref[...] = (_dot(x, w_ref[...]) * rs).astype(o_ref.dtype)

    @pl.when(pl.program_id(1) == 0)
    def _():
        side_ref[...] = _dot_nt(x, wside_ref[...]) * rs


def _mix_in(act, w, w_side_t, layer, bm=1024, bn=1280):
    x, rs = act
    m, k = x.shape
    n = w.shape[2]
    return pl.pallas_call(
        _mix_in_kernel,
        out_shape=[jax.ShapeDtypeStruct((m, n), BF16), jax.ShapeDtypeStruct((m, SMALL_W), F32)],
        grid=(m // bm, n // bn),
        in_specs=[pl.BlockSpec((bm, k), lambda i, j: (i, 0)),
                  pl.BlockSpec((bm, ROW_SCALE_W), lambda i, j: (i, 0)),
                  pl.BlockSpec((None, k, bn), lambda i, j: (layer, 0, j)),
                  pl.BlockSpec((None, SMALL_W, k), lambda i, j: (layer, 0, 0))],
        out_specs=[pl.BlockSpec((bm, bn), lambda i, j: (i, j)),
                   pl.BlockSpec((bm, SMALL_W), lambda i, j: (i, 0))],
        compiler_params=_cparams("parallel", "arbitrary"),
        name="mix_in",
    )(x, rs, w, w_side_t)


def _mm_wcast_kernel(scaled, x_ref, *refs):
    w_ref, o_ref, w_scr = refs[-3:]

    @pl.when(pl.program_id(1) == 0)
    def _():
        w_scr[...] = w_ref[...].astype(BF16)

    r = _dot(x_ref[...], w_scr[...])
    if scaled:
        r = r * refs[0][:, 0:1]
    o_ref[...] = r.astype(o_ref.dtype)


def _matmul_wcast(act, w, layer, out_dtype, bm, bn, name):
    x, rs = act
    m, k = x.shape
    n = w.shape[2]
    bm, bn = min(bm, m), min(bn, n)
    scaled = rs is not None
    in_specs = [pl.BlockSpec((bm, k), lambda j, i: (i, 0))]
    if scaled:
        in_specs.append(pl.BlockSpec((bm, ROW_SCALE_W), lambda j, i: (i, 0)))
    in_specs.append(pl.BlockSpec((None, k, bn), lambda j, i: (layer, 0, j)))
    return pl.pallas_call(
        functools.partial(_mm_wcast_kernel, scaled),
        out_shape=jax.ShapeDtypeStruct((m, n), out_dtype),
        grid=(n // bn, m // bm),
        in_specs=in_specs,
        out_specs=pl.BlockSpec((bm, bn), lambda j, i: (i, j)),
        scratch_shapes=[pltpu.VMEM((k, bn), BF16)],
        compiler_params=_cparams("arbitrary", "arbitrary"),
        name=name,
    )(*((x, rs, w) if scaled else (x, w)))


def _swiglu_kernel(bn, scaled, x_ref, *refs):
    wg_ref, wu_ref, o_ref, w_scr = refs[-4:]

    @pl.when(pl.program_id(1) == 0)
    def _():
        w_scr[:, 0:bn] = wg_ref[...].astype(BF16)
        w_scr[:, bn:2 * bn] = wu_ref[...].astype(BF16)

    r = _dot(x_ref[...], w_scr[...])
    if scaled:
        r = r * refs[0][:, 0:1]
    o_ref[...] = (_silu(r[:, 0:bn]) * r[:, bn:2 * bn]).astype(o_ref.dtype)


def _swiglu_in(act, w, layer, bm=512, bn=512):
    x, rs = act
    m, k = x.shape
    f = w.shape[2] // 2
    nb = f // bn
    scaled = rs is not None
    in_specs = [pl.BlockSpec((bm, k), lambda j, i: (i, 0))]
    if scaled:
        in_specs.append(pl.BlockSpec((bm, ROW_SCALE_W), lambda j, i: (i, 0)))
    in_specs += [pl.BlockSpec((None, k, bn), lambda j, i: (layer, 0, j)),
                 pl.BlockSpec((None, k, bn), lambda j, i: (layer, 0, j + nb))]
    return pl.pallas_call(
        functools.partial(_swiglu_kernel, bn, scaled),
        out_shape=jax.ShapeDtypeStruct((m, f), BF16),
        grid=(nb, m // bm),
        in_specs=in_specs,
        out_specs=pl.BlockSpec((bm, bn), lambda j, i: (i, j)),
        scratch_shapes=[pltpu.VMEM((k, 2 * bn), BF16)],
        compiler_params=_cparams("arbitrary", "arbitrary"),
        name="swiglu_in",
    )(*((x, rs, w, w) if scaled else (x, w, w)))


def _out_proj_kernel(nm, scale, emit_h, lhs_ref, w_ref, x_ref, gp_ref, gn_ref, xo_ref, *rest):
    if emit_h:
        ho_ref, rs_ref, y_scr, ssq_scr, ssq2_scr = rest
    else:
        y_scr, ssq_scr = rest
    i, n = pl.program_id(0), pl.program_id(1)
    cur = lax.rem(i, 2)
    prev = 1 - cur
    inv_d = 1.0 / (y_scr.shape[0] * y_scr.shape[2])

    def matmul_part():
        yb = _dot(lhs_ref[...], w_ref[...])
        y_scr[n] = yb.astype(BF16)
        part = jnp.sum(yb * yb, axis=-1, keepdims=True)
        ssq_scr[cur] = jnp.where(n == 0, part, ssq_scr[cur] + part)

    def residual_part():
        r = lax.rsqrt(ssq_scr[prev] * inv_d + EPS)
        xn = x_ref[...] + scale * (y_scr[n].astype(F32) * r * gp_ref[...])
        xo_ref[...] = xn
        if emit_h:
            ho_ref[...] = (xn * gn_ref[...]).astype(BF16)
            part = jnp.sum(xn * xn, axis=-1, keepdims=True)
            acc = jnp.where(n == 0, part, ssq2_scr[...] + part)
            ssq2_scr[...] = acc
            rs_ref[...] = jnp.broadcast_to(lax.rsqrt(acc * inv_d + EPS), rs_ref.shape)

    @pl.when(i == 0)
    def _():
        matmul_part()

    @pl.when((i > 0) & (i < nm))
    def _():
        residual_part()
        matmul_part()

    @pl.when(i == nm)
    def _():
        residual_part()


def _out_proj_post(lhs, w, layer, x, g_post, scale, g_next, name, bm=1024, bn=512):
    m, k = lhs.shape
    d = w.shape[2]
    nm, nn = m // bm, d // bn
    emit_h = g_next is not None
    gn = g_next if emit_h else g_post
    res_map = lambda i, n: (jnp.maximum(i - 1, 0), jnp.where(i == 0, 0, n))
    vec_map = lambda i, n: (0, jnp.where(i == 0, 0, n))
    res_spec = pl.BlockSpec((bm, bn), res_map)
    out_shape = [jax.ShapeDtypeStruct((m, d), F32)]
    out_specs = [res_spec]
    scratch = [pltpu.VMEM((nn, bm, bn), BF16), pltpu.VMEM((2, bm, 1), F32)]
    if emit_h:
        out_shape += [jax.ShapeDtypeStruct((m, d), BF16), jax.ShapeDtypeStruct((m, ROW_SCALE_W), F32)]
        out_specs += [res_spec, pl.BlockSpec((bm, ROW_SCALE_W), lambda i, n: (jnp.maximum(i - 1, 0), 0))]
        scratch.append(pltpu.VMEM((bm, 1), F32))
    res = pl.pallas_call(
        functools.partial(_out_proj_kernel, nm, scale, emit_h),
        out_shape=out_shape,
        grid=(nm + 1, nn),
        in_specs=[pl.BlockSpec((bm, k), lambda i, n: (jnp.minimum(i, nm - 1), 0)),
                  pl.BlockSpec((None, k, bn), lambda i, n: (layer, 0, n)),
                  res_spec,
                  pl.BlockSpec((1, bn), vec_map),
                  pl.BlockSpec((1, bn), vec_map)],
        out_specs=out_specs,
        scratch_shapes=scratch,
        compiler_params=_cparams("arbitrary", "arbitrary"),
        name=name,
    )(lhs, w, x, g_post.reshape(1, d), gn.reshape(1, d))
    return (res[0], (res[1], res[2])) if emit_h else (res[0], None)


def _gla_kernel(q_ref, k_ref, v_ref, r_ref, sm_ref, wgk_ref, bgk_ref, ng_ref, o_ref, cum_ref, *st_refs):
    @pl.when(pl.program_id(1) == 0)
    def _():
        for st_ref in st_refs:
            st_ref[...] = jnp.zeros_like(st_ref)

    rows = q_ref.shape[0]
    a_hi, a_mid, _ = _split3(sm_ref[...])
    w_hi, w_mid, _ = _split3(wgk_ref[...])
    z = _dot(a_hi, w_hi) + _dot(a_hi, w_mid) + _dot(a_mid, w_hi) + bgk_ref[...]
    logf = (jnp.minimum(z, 0.0) - jnp.log1p(jnp.exp(-jnp.abs(z)))) * (1.0 / GLA_TAU)
    tri = _chunk_tril(CUM_BLOCK)
    for b in range(rows // CUM_BLOCK):
        sl = slice(b * CUM_BLOCK, (b + 1) * CUM_BLOCK)
        cum_ref[sl, :] = _sel_dot(tri, logf[sl])

    ri = lax.broadcasted_iota(jnp.int32, (CHUNK, CHUNK), 0)
    ci = lax.broadcasted_iota(jnp.int32, (CHUNK, CHUNK), 1)
    tril = ri >= ci
    for c in range(rows // CHUNK):
        sl = pl.ds(c * CHUNK, CHUNK)
        for hh, st_ref in enumerate(st_refs):
            kl = slice(hh * GLA_DK, (hh + 1) * GLA_DK)
            vl = slice(hh * GLA_DV, (hh + 1) * GLA_DV)
            cum = cum_ref[sl, kl]
            cum_end = cum[CHUNK - 1:CHUNK, :]
            up, down = jnp.exp(cum), jnp.exp(-cum)
            q = q_ref[sl, kl].astype(F32) * (GLA_DK ** -0.5)
            k = k_ref[sl, kl].astype(F32)
            v = v_ref[sl, vl]
            q_dec = (q * up).astype(BF16)
            a_past = _dot_nt(q_dec, (k * down).astype(BF16))
            a_future = _dot_nt((q * down).astype(BF16), (k * up).astype(BF16))
            attn = jnp.where(tril, a_past, a_future).astype(BF16)
            state_t = st_ref[...]
            o = _dot(attn, v) + _dot_nt(q_dec, state_t.astype(BF16))
            contrib_t = _dot_tn(v, (k * jnp.exp(cum_end - cum)).astype(BF16))
            st_ref[...] = state_t * jnp.exp(cum_end) + contrib_t
            o = o * lax.rsqrt(jnp.mean(o * o, axis=-1, keepdims=True) + EPS) * ng_ref[:, vl]
            o_ref[sl, vl] = (o * _silu(r_ref[sl, vl].astype(F32))).astype(o_ref.dtype)


def _gla(proj, small, w_gk_pad, b_gk, norm_g):
    s = proj.shape[0]
    rows = min(MIX_ROWS, s)
    hp = GLA_HEADS_PER_STEP
    kw, vw = hp * GLA_DK, hp * GLA_DV
    qb, vb = P_GLA_Q // kw, P_GLA_V // vw
    kb, rb = P_GLA_K // kw, P_GLA_R // vw
    return pl.pallas_call(
        _gla_kernel,
        out_shape=jax.ShapeDtypeStruct((s, GLA_V), BF16),
        grid=(GLA_HEADS // hp, s // rows),
        in_specs=[pl.BlockSpec((rows, kw), lambda h, i: (i, qb + h)),
                  pl.BlockSpec((rows, kw), lambda h, i: (i, kb + h)),
                  pl.BlockSpec((rows, vw), lambda h, i: (i, vb + h)),
                  pl.BlockSpec((rows, vw), lambda h, i: (i, rb + h)),
                  pl.BlockSpec((rows, SMALL_W), lambda h, i: (i, 0)),
                  pl.BlockSpec((SMALL_W, kw), lambda h, i: (0, h)),
                  pl.BlockSpec((1, kw), lambda h, i: (0, h)),
                  pl.BlockSpec((1, vw), lambda h, i: (0, h))],
        out_specs=pl.BlockSpec((rows, vw), lambda h, i: (i, h)),
        scratch_shapes=[pltpu.VMEM((rows, kw), F32)] + [pltpu.VMEM((GLA_DV, GLA_DK), F32)] * hp,
        compiler_params=_cparams("parallel", "arbitrary"),
        name="gla",
    )(proj, proj, proj, proj, small, w_gk_pad, b_gk.reshape(1, GLA_QK), norm_g.reshape(1, GLA_V))


def _causal_conv_silu(xp_ref, x_ref, w_ref, b_ref, first):
    rows = x_ref.shape[0]
    hist = V7X_SUBLANES

    @pl.when(first)
    def _():
        xp_ref[pl.ds(0, hist), :] = jnp.zeros((hist, xp_ref.shape[1]), F32)

    xp_ref[pl.ds(hist, rows), :] = x_ref[...].astype(F32)
    acc = b_ref[...]
    for tap in range(SSM_CONV):
        acc = acc + xp_ref[pl.ds(hist - (SSM_CONV - 1) + tap, rows), :] * w_ref[tap:tap + 1, :]
    xp_ref[pl.ds(0, hist), :] = xp_ref[pl.ds(rows, hist), :]
    return _silu(acc)


def _ssd_kernel(z_ref, xs_ref, b_ref, c_ref, sm_ref, e_ref, cwx_ref, cwb_ref, cwc_ref,
                cbx_ref, cbb_ref, cbc_ref, dtb_ref, alog_ref, dsk_ref, ng_ref,
                o_ref, xpx_ref, xpb_ref, xpc_ref, xs_scr, b_scr, c_scr, dt_scr, cum_scr, *s_refs):
    first = pl.program_id(1) == 0

    @pl.when(first)
    def _():
        for s_ref in s_refs:
            s_ref[...] = jnp.zeros_like(s_ref)

    rows = xs_ref.shape[0]
    xs_scr[...] = _causal_conv_silu(xpx_ref, xs_ref, cwx_ref, cbx_ref, first)
    b_scr[...] = _causal_conv_silu(xpb_ref, b_ref, cwb_ref, cbb_ref, first).astype(BF16)
    c_scr[...] = _causal_conv_silu(xpc_ref, c_ref, cwc_ref, cbc_ref, first).astype(BF16)

    dt_in = sm_ref[...] + dtb_ref[...]
    dt_c = jnp.maximum(dt_in, 0.0) + jnp.log1p(jnp.exp(-jnp.abs(dt_in)))
    da_c = dt_c * (-jnp.exp(alog_ref[...]))
    tri = _chunk_tril(CUM_BLOCK)
    cum_c = jnp.concatenate([_sel_dot(tri, da_c[b * CUM_BLOCK:(b + 1) * CUM_BLOCK])
                             for b in range(rows // CUM_BLOCK)], axis=0)
    dt_scr[...] = _sel_dot_rhs(dt_c, e_ref[0])
    cum_scr[...] = _sel_dot_rhs(cum_c, e_ref[0])

    gw = SSM_GW
    rw = lax.broadcasted_iota(jnp.int32, (CHUNK, gw), 0)
    cw = lax.broadcasted_iota(jnp.int32, (CHUNK, gw), 1)
    diag_tile = (jnp.bitwise_and(cw, SSM_HEADDIM - 1) == rw).astype(F32)
    half = gw // 2
    rb = lax.broadcasted_iota(jnp.int32, (half, half), 0)
    cb_ = lax.broadcasted_iota(jnp.int32, (half, half), 1)
    head_mask = jnp.right_shift(rb, HEAD_SHIFT) == jnp.right_shift(cb_, HEAD_SHIFT)
    reps = half // CHUNK
    for c in range(rows // CHUNK):
        sl = pl.ds(c * CHUNK, CHUNK)
        for gg, s_ref in enumerate(s_refs):
            xl = slice(gg * gw, (gg + 1) * gw)
            nl = slice(gg * SSM_STATE, (gg + 1) * SSM_STATE)
            cum = cum_scr[sl, xl]
            cum_row = jnp.sum(cum * diag_tile, axis=0, keepdims=True)
            seg = jnp.exp(-jnp.abs(cum - cum_row))
            xs = xs_scr[sl, xl]
            bm = b_scr[sl, nl]
            cm = c_scr[sl, nl]
            xdt = xs * dt_scr[sl, xl]
            cbt = _dot_nt(cm, jnp.concatenate([bm] * SSM_HPG, axis=0))
            m_all = (cbt * seg).astype(BF16)
            intra = []
            for hlf in range(2):
                xh = xdt[:, hlf * half:(hlf + 1) * half]
                bd = jnp.where(head_mask, jnp.concatenate([xh] * reps, axis=0), 0.0).astype(BF16)
                intra.append(_dot(m_all[:, hlf * half:(hlf + 1) * half], bd))
            intra = jnp.concatenate(intra, axis=1)
            cum_end = cum[CHUNK - 1:CHUNK, :]
            state = s_ref[...]
            inter = _dot(cm, state.astype(BF16)) * jnp.exp(cum)
            contrib = _dot_tn(bm, (xdt * jnp.exp(cum_end - cum)).astype(BF16))
            s_ref[...] = state * jnp.exp(cum_end) + contrib
            y = intra + inter + xs * dsk_ref[:, xl]
            y = y * _silu(z_ref[sl, xl].astype(F32))
            y = y * lax.rsqrt(jnp.mean(y * y, axis=-1, keepdims=True) + EPS) * ng_ref[:, xl]
            o_ref[sl, xl] = y.astype(o_ref.dtype)


def _ssd(proj, small, expand, conv_w, conv_b, dt_bias, a_log, d_skip, norm_g):
    s = proj.shape[0]
    rows = min(MIX_ROWS, s)
    gp = SSD_GROUPS_PER_STEP
    xw, nw = gp * SSM_GW, gp * SSM_STATE
    zb, xb = P_SSM_Z // xw, P_SSM_X // xw
    bb, cb = P_SSM_B // nw, P_SSM_C // nw
    cwb, cwc = SSM_INNER // nw, (SSM_INNER + SSM_BC) // nw

    def lane_expand(p):
        return jnp.repeat(p.astype(F32), SSM_HEADDIM).reshape(1, SSM_INNER)

    def side_lanes(p):
        return jnp.pad(p.astype(F32), (SMALL_DT0, SMALL_W - SMALL_DT0 - SSM_HEADS)).reshape(1, SMALL_W)

    conv_b2 = conv_b.reshape(1, SSM_CONV_DIM)
    vec = pl.BlockSpec((1, xw), lambda g, i: (0, g))
    side_vec = pl.BlockSpec((1, SMALL_W), lambda g, i: (0, 0))
    return pl.pallas_call(
        _ssd_kernel,
        out_shape=jax.ShapeDtypeStruct((s, SSM_INNER), BF16),
        grid=(SSM_GROUPS // gp, s // rows),
        in_specs=[pl.BlockSpec((rows, xw), lambda g, i: (i, zb + g)),
                  pl.BlockSpec((rows, xw), lambda g, i: (i, xb + g)),
                  pl.BlockSpec((rows, nw), lambda g, i: (i, bb + g)),
                  pl.BlockSpec((rows, nw), lambda g, i: (i, cb + g)),
                  pl.BlockSpec((rows, SMALL_W), lambda g, i: (i, 0)),
                  pl.BlockSpec((1, SMALL_W, xw), lambda g, i: (g, 0, 0)),
                  pl.BlockSpec((SSM_CONV, xw), lambda g, i: (0, g)),
                  pl.BlockSpec((SSM_CONV, nw), lambda g, i: (0, cwb + g)),
                  pl.BlockSpec((SSM_CONV, nw), lambda g, i: (0, cwc + g)),
                  pl.BlockSpec((1, xw), lambda g, i: (0, g)),
                  pl.BlockSpec((1, nw), lambda g, i: (0, cwb + g)),
                  pl.BlockSpec((1, nw), lambda g, i: (0, cwc + g)),
                  side_vec, side_vec, vec, vec],
        out_specs=pl.BlockSpec((rows, xw), lambda g, i: (i, g)),
        scratch_shapes=[pltpu.VMEM((rows + V7X_SUBLANES, xw), F32),
                        pltpu.VMEM((rows + V7X_SUBLANES, nw), F32),
                        pltpu.VMEM((rows + V7X_SUBLANES, nw), F32),
                        pltpu.VMEM((rows, xw), F32),
                        pltpu.VMEM((rows, nw), BF16),
                        pltpu.VMEM((rows, nw), BF16),
                        pltpu.VMEM((rows, xw), F32),
                        pltpu.VMEM((rows, xw), F32)] + [pltpu.VMEM((SSM_STATE, SSM_GW), F32)] * gp,
        compiler_params=_cparams("parallel", "arbitrary"),
        name="ssd",
    )(proj, proj, proj, proj, small, expand, conv_w, conv_w, conv_w, conv_b2, conv_b2, conv_b2,
      side_lanes(dt_bias), side_lanes(a_log), lane_expand(d_skip), norm_g.reshape(1, SSM_INNER))


def _ssd_expand_table():
    xw = SSD_GROUPS_PER_STEP * SSM_GW
    rows = jnp.arange(SMALL_W, dtype=jnp.int32)[None, :, None]
    cols = jnp.arange(xw, dtype=jnp.int32)[None, None, :]
    step = jnp.arange(SSM_INNER // xw, dtype=jnp.int32)[:, None, None]
    return (rows == SMALL_DT0 + (step * xw + cols) // SSM_HEADDIM).astype(BF16)


def _rope_table_kernel(pos_ref, freq_ref, sign_ref, cos_ref, sin_ref):
    ang = pos_ref[...] * freq_ref[...]
    cos_ref[...] = jnp.cos(ang)
    sin_ref[...] = jnp.sin(ang) * sign_ref[...]


def _rope_tables(positions):
    s = positions.shape[-1]
    half = RET_DK // 2
    inv_freq = ROPE_BASE ** (-jnp.arange(half, dtype=F32) / half)
    freq = jnp.concatenate([inv_freq, inv_freq]).reshape(1, RET_DK)
    sign = jnp.concatenate([-jnp.ones((half,), F32), jnp.ones((half,), F32)]).reshape(1, RET_DK)
    pos = positions.astype(F32).reshape(s, 1)
    rows = min(1024, s)
    tab = jax.ShapeDtypeStruct((s, RET_DK), F32)
    return pl.pallas_call(
        _rope_table_kernel,
        out_shape=[tab, tab],
        grid=(s // rows,),
        in_specs=[pl.BlockSpec((rows, 1), lambda i: (i, 0)),
                  pl.BlockSpec((1, RET_DK), lambda i: (0, 0)),
                  pl.BlockSpec((1, RET_DK), lambda i: (0, 0))],
        out_specs=[pl.BlockSpec((rows, RET_DK), lambda i: (i, 0))] * 2,
        compiler_params=_cparams("parallel"),
        name="rope_tables",
    )(pos, freq, sign)


def _ret_kernel(q_ref, k_ref, v_ref, g_ref, cos_ref, sin_ref, dmat_ref, qdec_ref, kdec_ref, sdec_ref,
                ng_ref, o_ref, *s_refs):
    @pl.when(pl.program_id(1) == 0)
    def _():
        for s_ref in s_refs:
            s_ref[...] = jnp.zeros_like(s_ref)

    cos, sin = cos_ref[...], sin_ref[...]

    def rot(t):
        t = t.astype(F32)
        return t * cos + pltpu.roll(t, RET_DK // 2, axis=1) * sin

    for hh, s_ref in enumerate(s_refs):
        kl = slice(hh * RET_DK, (hh + 1) * RET_DK)
        vl = slice(hh * RET_DV, (hh + 1) * RET_DV)
        q = rot(q_ref[:, kl])
        k = rot(k_ref[:, kl]) * (RET_DK ** -0.5)
        v = v_ref[:, vl]
        scores = (_dot_nt(q.astype(BF16), k.astype(BF16)) * dmat_ref[hh]).astype(BF16)
        state = s_ref[...]
        o = _dot(scores, v) + _dot((q * qdec_ref[hh]).astype(BF16), state.astype(BF16))
        s_ref[...] = state * sdec_ref[hh] + _dot_tn((k * kdec_ref[hh]).astype(BF16), v)
        o = o * lax.rsqrt(jnp.mean(o * o, axis=-1, keepdims=True) + EPS) * ng_ref[:, vl]
        o_ref[:, vl] = (o * _silu(g_ref[:, vl].astype(F32))).astype(o_ref.dtype)


def _ret_tables(rows):
    log_gamma = jnp.log1p(-jnp.exp2(-5.0 - jnp.arange(RET_HEADS, dtype=F32)))[:, None, None]
    i = jnp.arange(rows, dtype=jnp.int32)
    dist = (i[:, None] - i[None, :]).astype(F32)[None]
    ch_i, ch_j = (i // CHUNK)[:, None], (i // CHUNK)[None, :]
    dmat = jnp.where((ch_i == ch_j)[None], jnp.exp(log_gamma * jnp.abs(dist)),
                     jnp.where((ch_j < ch_i)[None], jnp.exp(log_gamma * dist), 0.0))
    fi = i.astype(F32)[None, :, None]
    qdec = jnp.broadcast_to(jnp.exp(log_gamma * (fi + 1.0)), (RET_HEADS, rows, RET_DK))
    kdec = jnp.broadcast_to(jnp.exp(log_gamma * (rows - 1.0 - fi)), (RET_HEADS, rows, RET_DK))
    sdec = jnp.broadcast_to(jnp.exp(log_gamma * float(rows)), (RET_HEADS, 1, RET_DV))
    return dmat, qdec, kdec, sdec


def _retention(proj, cos_t, sin_t, tables, norm_g):
    s = proj.shape[0]
    rows = min(RET_BLOCK, s)
    dmat, qdec, kdec, sdec = tables
    hp = RET_HEADS_PER_STEP
    kw, vw = hp * RET_DK, hp * RET_DV
    qb, kb = P_RET_Q // kw, P_RET_K // kw
    vb, gb = P_RET_V // vw, P_RET_G // vw
    return pl.pallas_call(
        _ret_kernel,
        out_shape=jax.ShapeDtypeStruct((s, RET_V), BF16),
        grid=(RET_HEADS // hp, s // rows),
        in_specs=[pl.BlockSpec((rows, kw), lambda h, i: (i, qb + h)),
                  pl.BlockSpec((rows, kw), lambda h, i: (i, kb + h)),
                  pl.BlockSpec((rows, vw), lambda h, i: (i, vb + h)),
                  pl.BlockSpec((rows, vw), lambda h, i: (i, gb + h)),
                  pl.BlockSpec((rows, RET_DK), lambda h, i: (i, 0)),
                  pl.BlockSpec((rows, RET_DK), lambda h, i: (i, 0)),
                  pl.BlockSpec((hp, rows, rows), lambda h, i: (h, 0, 0)),
                  pl.BlockSpec((hp, rows, RET_DK), lambda h, i: (h, 0, 0)),
                  pl.BlockSpec((hp, rows, RET_DK), lambda h, i: (h, 0, 0)),
                  pl.BlockSpec((hp, 1, RET_DV), lambda h, i: (h, 0, 0)),
                  pl.BlockSpec((1, vw), lambda h, i: (0, h))],
        out_specs=pl.BlockSpec((rows, vw), lambda h, i: (i, h)),
        scratch_shapes=[pltpu.VMEM((RET_DK, RET_DV), F32)] * hp,
        compiler_params=_cparams("parallel", "arbitrary"),
        name="retention",
    )(proj, proj, proj, proj, cos_t, sin_t, dmat, qdec, kdec, sdec, norm_g.reshape(1, RET_V))


def _merge_kernel(gl_ref, y0_ref, y1_ref, y2_ref, gu0_ref, gu1_ref, gu2_ref, gb0_ref, gb1_ref, gb2_ref,
                  wo0_ref, wo1_ref, wo2_ref, o_ref):
    gl = gl_ref[...]
    acc = None
    for y_ref, gu_ref, gb_ref, wo_ref in ((y0_ref, gu0_ref, gb0_ref, wo0_ref),
                                          (y1_ref, gu1_ref, gb1_ref, wo1_ref),
                                          (y2_ref, gu2_ref, gb2_ref, wo2_ref)):
        gate = jax.nn.sigmoid(_dot(gl, gu_ref[...]) + gb_ref[...])
        term = gate * _dot(y_ref[...], wo_ref[...])
        acc = term if acc is None else acc + term
    o_ref[...] = acc.astype(o_ref.dtype)


def _merge(proj, ys, gate_w2, gate_b, w_out, layer, bm=1024, bn=512):
    s = proj.shape[0]
    bm = min(bm, s)
    d = D_MODEL
    nb = d // bn
    gate_b4 = gate_b.reshape(DEPTH, N_BRANCH, 1, d)
    y_spec = pl.BlockSpec((bm, BRANCH_W), lambda i, j: (i, 0))
    in_specs = [pl.BlockSpec((bm, GATE_RANK), lambda i, j: (i, P_GATE // GATE_RANK)), y_spec, y_spec, y_spec]
    in_specs += [pl.BlockSpec((None, GATE_RANK, bn),
                              functools.partial(lambda n, i, j: (layer * N_BRANCH + n, 0, j), n))
                 for n in range(N_BRANCH)]
    in_specs += [pl.BlockSpec((None, None, 1, bn), functools.partial(lambda n, i, j: (layer, n, 0, j), n))
                 for n in range(N_BRANCH)]
    in_specs += [pl.BlockSpec((None, BRANCH_W, bn),
                              functools.partial(lambda n, i, j: (layer * N_BRANCH + n, 0, j), n))
                 for n in range(N_BRANCH)]
    return pl.pallas_call(
        _merge_kernel,
        out_shape=jax.ShapeDtypeStruct((s, d), BF16),
        grid=(s // bm, nb),
        in_specs=in_specs,
        out_specs=pl.BlockSpec((bm, bn), lambda i, j: (i, j)),
        compiler_params=_cparams("parallel", "parallel"),
        name="gated_merge",
    )(proj, ys[0], ys[1], ys[2], gate_w2, gate_w2, gate_w2, gate_b4, gate_b4, gate_b4, w_out, w_out, w_out)


def _xattn_kernel(q_ref, kv_ref, o_ref):
    for h in range(XA_HEADS):
        lo, hi = h * XA_DIM, (h + 1) * XA_DIM
        s = _dot_nt(q_ref[:, lo:hi], kv_ref[:, lo:hi]) * (XA_DIM ** -0.5)
        p = jnp.exp(s - jnp.max(s, axis=-1, keepdims=True))
        p = p / jnp.sum(p, axis=-1, keepdims=True)
        o_ref[:, lo:hi] = _dot(p.astype(BF16), kv_ref[:, XA_W + lo:XA_W + hi]).astype(o_ref.dtype)


def _xattn(q, kv, rows=512):
    s = q.shape[0]
    rows = min(rows, s)
    n_mem = kv.shape[0]
    return pl.pallas_call(
        _xattn_kernel,
        out_shape=jax.ShapeDtypeStruct((s, XA_W), BF16),
        grid=(s // rows,),
        in_specs=[pl.BlockSpec((rows, XA_W), lambda i: (i, 0)),
                  pl.BlockSpec((n_mem, 2 * XA_W), lambda i: (0, 0))],
        out_specs=pl.BlockSpec((rows, XA_W), lambda i: (i, 0)),
        compiler_params=_cparams("parallel"),
        name="xattn",
    )(q, kv)


def kernel(x, mem, positions, norm_g, mem_norm_g, ffn1_w_in, ffn1_w_out, mix_w_in, gla_w_gk, gla_b_gk,
           gla_norm_g, ssm_conv_w, ssm_conv_b, ssm_dt_bias, ssm_a_log, ssm_d, ssm_norm_g, ret_norm_g,
           gate_w_up, gate_b, mix_w_out, xattn_w_q, xattn_w_kv, xattn_w_o, ffn2_w_in, ffn2_w_out):
    b, s, d = x.shape
    assert b == 1 and d == D_MODEL and s % max(RET_BLOCK, MIX_ROWS, 1024) == 0
    nl = DEPTH
    xr = x.reshape(s, d)
    memr = mem.reshape(mem.shape[1], d)
    cos_t, sin_t = _rope_tables(positions)
    ret_tabs = _ret_tables(min(RET_BLOCK, s))
    expand = _ssd_expand_table()

    ffn1_out = _cast_bf16(ffn1_w_out.reshape(nl * D_FF, d), 512).reshape(nl, D_FF, d)
    ffn2_out = _cast_bf16(ffn2_w_out.reshape(nl * D_FF, d), 512).reshape(nl, D_FF, d)
    xa_out = _cast_bf16(xattn_w_o.reshape(nl * XA_W, d), 512).reshape(nl, XA_W, d)
    mix_out = _cast_bf16(mix_w_out.reshape(nl * N_BRANCH * BRANCH_W, d), 512).reshape(nl * N_BRANCH, BRANCH_W, d)
    gate_up = _cast_bf16(jnp.transpose(gate_w_up, (0, 2, 1, 3)).reshape(nl * N_BRANCH * GATE_RANK, d),
                         512).reshape(nl * N_BRANCH, GATE_RANK, d)
    mix_w_in_t = jnp.transpose(mix_w_in, (0, 2, 1))
    mix_in_big = _repack_mix_w_in(mix_w_in_t)
    mix_in_side_t = _side_mix_w_in(mix_w_in_t)
    w_gk_pad = jnp.concatenate([gla_w_gk, jnp.zeros((nl, SMALL_W - GLA_RANK, GLA_QK), F32)], axis=1)

    h = (_norm_cast(xr, norm_g[0, 0]), None)
    for l in range(nl):
        ng = norm_g[l]
        a = _swiglu_in(h, ffn1_w_in, l)
        xr, h = _out_proj_post(a, ffn1_out, l, xr, ng[1], 0.5, ng[2], "ffn_out_post")
        proj, small = _mix_in(h, mix_in_big, mix_in_side_t, l)
        y_gla = _gla(proj, small, w_gk_pad[l], gla_b_gk[l], gla_norm_g[l])
        y_ssd = _ssd(proj, small, expand, ssm_conv_w[l], ssm_conv_b[l], ssm_dt_bias[l], ssm_a_log[l],
                     ssm_d[l], ssm_norm_g[l])
        y_ret = _retention(proj, cos_t, sin_t, ret_tabs, ret_norm_g[l])
        mixed = _merge(proj, (y_gla, y_ssd, y_ret), gate_up, gate_b, mix_out, l)
        xr, hn = _post(mixed, xr, ng[3], 1.0, ng[4])
        q = _matmul_wcast((hn, None), xattn_w_q, l, BF16, 1024, 512, "xattn_q")
        mem_n = _norm_cast(memr, mem_norm_g[l])
        kv = _matmul_wcast((mem_n, None), xattn_w_kv, l, BF16, 256, 512, "xattn_kv")
        o = _xattn(q, kv)
        xr, h = _out_proj_post(o, xa_out, l, xr, ng[5], 1.0, ng[6], "xattn_out_post", bn=1024)
        a = _swiglu_in(h, ffn2_w_in, l)
        g_next = norm_g[l + 1, 0] if l + 1 < nl else None
        xr, h = _out_proj_post(a, ffn2_out, l, xr, ng[7], 0.5, g_next, "ffn_out_post")
    return xr.reshape(b, s, d)
```

```python
import functools

import numpy as np
import jax
import jax.numpy as jnp
from jax import lax
from jax.experimental import pallas as pl
from jax.experimental.pallas import tpu as pltpu

F32 = jnp.float32
BF16 = jnp.bfloat16

D_MODEL = 4096
DEPTH = 4
CHUNK = 64
D_FF = 4096
EPS = 1e-6
GLA_HEADS, GLA_DK, GLA_DV, GLA_RANK, GLA_TAU = 4, 256, 512, 16, 16.0
GLA_QK, GLA_V = GLA_HEADS * GLA_DK, GLA_HEADS * GLA_DV
SSM_HEADS, SSM_HEADDIM, SSM_GROUPS, SSM_STATE, SSM_CONV = 32, 64, 4, 128, 4
SSM_INNER = SSM_HEADS * SSM_HEADDIM
SSM_BC = SSM_GROUPS * SSM_STATE
SSM_CONV_DIM = SSM_INNER + 2 * SSM_BC
SSM_HPG = SSM_HEADS // SSM_GROUPS
SSM_GW = SSM_INNER // SSM_GROUPS
RET_HEADS, RET_DK, RET_DV = 8, 128, 256
RET_QK, RET_V = RET_HEADS * RET_DK, RET_HEADS * RET_DV
ROPE_BASE = 10000.0
XA_HEADS, XA_DIM = 4, 256
XA_W = XA_HEADS * XA_DIM
N_BRANCH, BRANCH_W, GATE_RANK = 3, 2048, 512

_IN_WIDTHS = (GLA_QK, GLA_QK, GLA_V, GLA_V, GLA_RANK, SSM_INNER, SSM_CONV_DIM, SSM_HEADS,
              RET_QK, RET_QK, RET_V, RET_V, GATE_RANK)
_IN_OFF = np.concatenate([[0], np.cumsum(_IN_WIDTHS)]).tolist()
IN_COLS = _IN_OFF[-1]
P_GLA_Q, P_GLA_K, P_GLA_V, P_GLA_R = 0, 1024, 2048, 4096
P_SSM_Z, P_SSM_X, P_SSM_B, P_SSM_C = 6144, 8192, 10240, 10752
P_RET_Q, P_RET_K, P_RET_V, P_RET_G = 11264, 12288, 13312, 15360
P_GATE = 17408
P_COLS = 17920
SMALL_W = 128
SMALL_DT0 = GLA_RANK

V7X_VMEM_LIMIT = 60 * 1024 * 1024
V7X_SUBLANES = 8
CHUNK_SHIFT = CHUNK.bit_length() - 1
HEAD_SHIFT = SSM_HEADDIM.bit_length() - 1

RET_BLOCK = 512
MIX_ROWS = 512
CUM_BLOCK = 256
GLA_HEADS_PER_STEP = 4
SSD_GROUPS_PER_STEP = 2
RET_HEADS_PER_STEP = 4


def _cparams(*sem):
    return pltpu.CompilerParams(dimension_semantics=sem, vmem_limit_bytes=V7X_VMEM_LIMIT)


def _silu(x):
    return x * jax.nn.sigmoid(x)


def _dot(a, b):
    return jnp.dot(a, b, preferred_element_type=F32)


def _dot_nt(a, b):
    return lax.dot_general(a, b, (((1,), (1,)), ((), ())), preferred_element_type=F32)


def _dot_tn(a, b):
    return lax.dot_general(a, b, (((0,), (0,)), ((), ())), preferred_element_type=F32)


def _split3(x):
    hi = x.astype(BF16)
    r1 = x - hi.astype(F32)
    mid = r1.astype(BF16)
    lo = (r1 - mid.astype(F32)).astype(BF16)
    return hi, mid, lo


def _sel_dot(sel, x):
    hi, mid, lo = _split3(x)
    return _dot(sel, hi) + _dot(sel, mid) + _dot(sel, lo)


def _sel_dot_rhs(x, sel):
    hi, mid, lo = _split3(x)
    return _dot(hi, sel) + _dot(mid, sel) + _dot(lo, sel)


def _chunk_tril(n):
    ri = lax.broadcasted_iota(jnp.int32, (n, n), 0)
    ci = lax.broadcasted_iota(jnp.int32, (n, n), 1)
    same = jnp.right_shift(ri, CHUNK_SHIFT) == jnp.right_shift(ci, CHUNK_SHIFT)
    return jnp.where(same & (ri >= ci), 1.0, 0.0).astype(BF16)


def _cast_kernel(w_ref, o_ref):
    o_ref[...] = w_ref[...].astype(o_ref.dtype)


def _cast_bf16(w, rows):
    r, c = w.shape
    return pl.pallas_call(
        _cast_kernel,
        out_shape=jax.ShapeDtypeStruct((r, c), BF16),
        grid=(r // rows,),
        in_specs=[pl.BlockSpec((rows, c), lambda i: (i, 0))],
        out_specs=pl.BlockSpec((rows, c), lambda i: (i, 0)),
        compiler_params=_cparams("parallel"),
        name="cast_bf16",
    )(w)


REPACK_COLS = 512


def _packed_to_source_col(c):
    skipped = jnp.where(c >= P_SSM_Z, GLA_RANK, 0) + jnp.where(c >= P_RET_Q, SSM_HEADS, 0)
    return pl.multiple_of(c + skipped, GLA_RANK)


def _repack_kernel(w_ref, o_ref):
    o_ref[...] = w_ref[...].T.astype(o_ref.dtype)


def _repack_mix_w_in(w_t):
    nl, _, d = w_t.shape
    return pl.pallas_call(
        _repack_kernel,
        out_shape=jax.ShapeDtypeStruct((nl, d, P_COLS), BF16),
        grid=(nl, P_COLS // REPACK_COLS),
        in_specs=[pl.BlockSpec((pl.Squeezed(), pl.Element(REPACK_COLS), pl.Element(d)),
                               lambda l, j: (l, _packed_to_source_col(j * REPACK_COLS), 0))],
        out_specs=pl.BlockSpec((None, d, REPACK_COLS), lambda l, j: (l, 0, j)),
        compiler_params=_cparams("parallel", "parallel"),
        name="repack_mix_w_in",
    )(w_t)


def _side_w_kernel(gk_ref, dt_ref, o_ref):
    o_ref[...] = jnp.zeros(o_ref.shape, o_ref.dtype)
    o_ref[0:GLA_RANK, :] = gk_ref[...].astype(o_ref.dtype)
    o_ref[GLA_RANK:GLA_RANK + SSM_HEADS, :] = dt_ref[...].astype(o_ref.dtype)


def _side_mix_w_in(w_t):
    nl, _, d = w_t.shape
    o = _IN_OFF
    return pl.pallas_call(
        _side_w_kernel,
        out_shape=jax.ShapeDtypeStruct((nl, SMALL_W, d), BF16),
        grid=(nl,),
        in_specs=[pl.BlockSpec((pl.Squeezed(), pl.Element(GLA_RANK), pl.Element(d)), lambda l: (l, o[4], 0)),
                  pl.BlockSpec((pl.Squeezed(), pl.Element(SSM_HEADS), pl.Element(d)), lambda l: (l, o[7], 0))],
        out_specs=pl.BlockSpec((None, SMALL_W, d), lambda l: (l, 0, 0)),
        compiler_params=_cparams("parallel"),
        name="side_mix_w_in",
    )(w_t, w_t)


def _norm_cast_kernel(x_ref, g_ref, o_ref):
    x = x_ref[...]
    r = lax.rsqrt(jnp.mean(x * x, axis=-1, keepdims=True) + EPS)
    o_ref[...] = (x * r * g_ref[...]).astype(o_ref.dtype)


def _norm_cast(x, g, rows=256):
    m, d = x.shape
    rows = min(rows, m)
    return pl.pallas_call(
        _norm_cast_kernel,
        out_shape=jax.ShapeDtypeStruct((m, d), BF16),
        grid=(m // rows,),
        in_specs=[pl.BlockSpec((rows, d), lambda i: (i, 0)), pl.BlockSpec((1, d), lambda i: (0, 0))],
        out_specs=pl.BlockSpec((rows, d), lambda i: (i, 0)),
        compiler_params=_cparams("parallel"),
        name="norm_cast",
    )(x, g.reshape(1, d))


ROW_SCALE_W = 128


def _mix_in_kernel(x_ref, rs_ref, w_ref, wside_ref, o_ref, side_ref):
    x = x_ref[...]
    rs = rs_ref[:, 0:1]
    o_ref[...] = (_dot(x, w_ref[...]) * rs).astype(o_ref.dtype)

    @pl.when(pl.program_id(1) == 0)
    def _():
        side_ref[...] = _dot_nt(x, wside_ref[...]) * rs


def _mix_in(act, w, w_side_t, layer, bm=1024, bn=1280):
    x, rs = act
    m, k = x.shape
    n = w.shape[2]
    return pl.pallas_call(
        _mix_in_kernel,
        out_shape=[jax.ShapeDtypeStruct((m, n), BF16), jax.ShapeDtypeStruct((m, SMALL_W), F32)],
        grid=(m // bm, n // bn),
        in_specs=[pl.BlockSpec((bm, k), lambda i, j: (i, 0)),
                  pl.BlockSpec((bm, ROW_SCALE_W), lambda i, j: (i, 0)),
                  pl.BlockSpec((None, k, bn), lambda i, j: (layer, 0, j)),
                  pl.BlockSpec((None, SMALL_W, k), lambda i, j: (layer, 0, 0))],
        out_specs=[pl.BlockSpec((bm, bn), lambda i, j: (i, j)),
                   pl.BlockSpec((bm, SMALL_W), lambda i, j: (i, 0))],
        compiler_params=_cparams("parallel", "arbitrary"),
        name="mix_in",
    )(x, rs, w, w_side_t)


def _mm_wcast_kernel(scaled, x_ref, *refs):
    w_ref, o_ref, w_scr = refs[-3:]

    @pl.when(pl.program_id(1) == 0)
    def _():
        w_scr[...] = w_ref[...].astype(BF16)

    r = _dot(x_ref[...], w_scr[...])
    if scaled:
        r = r * refs[0][:, 0:1]
    o_ref[...] = r.astype(o_ref.dtype)


def _matmul_wcast(act, w, layer, out_dtype, bm, bn, name):
    x, rs = act
    m, k = x.shape
    n = w.shape[2]
    bm, bn = min(bm, m), min(bn, n)
    scaled = rs is not None
    in_specs = [pl.BlockSpec((bm, k), lambda j, i: (i, 0))]
    if scaled:
        in_specs.append(pl.BlockSpec((bm, ROW_SCALE_W), lambda j, i: (i, 0)))
    in_specs.append(pl.BlockSpec((None, k, bn), lambda j, i: (layer, 0, j)))
    return pl.pallas_call(
        functools.partial(_mm_wcast_kernel, scaled),
        out_shape=jax.ShapeDtypeStruct((m, n), out_dtype),
        grid=(n // bn, m // bm),
        in_specs=in_specs,
        out_specs=pl.BlockSpec((bm, bn), lambda j, i: (i, j)),
        scratch_shapes=[pltpu.VMEM((k, bn), BF16)],
        compiler_params=_cparams("arbitrary", "arbitrary"),
        name=name,
    )(*((x, rs, w) if scaled else (x, w)))


def _swiglu_kernel(bn, scaled, x_ref, *refs):
    wg_ref, wu_ref, o_ref, w_scr = refs[-4:]

    @pl.when(pl.program_id(1) == 0)
    def _():
        w_scr[:, 0:bn] = wg_ref[...].astype(BF16)
        w_scr[:, bn:2 * bn] = wu_ref[...].astype(BF16)

    r = _dot(x_ref[...], w_scr[...])
    if scaled:
        r = r * refs[0][:, 0:1]
    o_ref[...] = (_silu(r[:, 0:bn]) * r[:, bn:2 * bn]).astype(o_ref.dtype)


def _swiglu_in(act, w, layer, bm=512, bn=512):
    x, rs = act
    m, k = x.shape
    f = w.shape[2] // 2
    nb = f // bn
    scaled = rs is not None
    in_specs = [pl.BlockSpec((bm, k), lambda j, i: (i, 0))]
    if scaled:
        in_specs.append(pl.BlockSpec((bm, ROW_SCALE_W), lambda j, i: (i, 0)))
    in_specs += [pl.BlockSpec((None, k, bn), lambda j, i: (layer, 0, j)),
                 pl.BlockSpec((None, k, bn), lambda j, i: (layer, 0, j + nb))]
    return pl.pallas_call(
        functools.partial(_swiglu_kernel, bn, scaled),
        out_shape=jax.ShapeDtypeStruct((m, f), BF16),
        grid=(nb, m // bm),
        in_specs=in_specs,
        out_specs=pl.BlockSpec((bm, bn), lambda j, i: (i, j)),
        scratch_shapes=[pltpu.VMEM((k, 2 * bn), BF16)],
        compiler_params=_cparams("arbitrary", "arbitrary"),
        name="swiglu_in",
    )(*((x, rs, w, w) if scaled else (x, w, w)))


def _plain_out_proj(lhs_ref, w_ref):
    return _dot(lhs_ref[...], w_ref[...])


def _out_proj_kernel(nm, scale, emit_h, proj_fn, n_proj, *refs):
    proj_refs = refs[:n_proj]
    x_ref, gp_ref, gn_ref, xo_ref = refs[n_proj:n_proj + 4]
    rest = refs[n_proj + 4:]
    if emit_h:
        ho_ref, rs_ref, y_scr, ssq_scr, ssq2_scr = rest
    else:
        y_scr, ssq_scr = rest
    i, n = pl.program_id(0), pl.program_id(1)
    cur = lax.rem(i, 2)
    prev = 1 - cur
    inv_d = 1.0 / (y_scr.shape[0] * y_scr.shape[2])

    def matmul_part():
        yb = proj_fn(*proj_refs)
        y_scr[n] = yb.astype(BF16)
        part = jnp.sum(yb * yb, axis=-1, keepdims=True)
        ssq_scr[cur] = jnp.where(n == 0, part, ssq_scr[cur] + part)

    def residual_part():
        r = lax.rsqrt(ssq_scr[prev] * inv_d + EPS)
        xn = x_ref[...] + scale * (y_scr[n].astype(F32) * r * gp_ref[...])
        xo_ref[...] = xn
        if emit_h:
            ho_ref[...] = (xn * gn_ref[...]).astype(BF16)
            part = jnp.sum(xn * xn, axis=-1, keepdims=True)
            acc = jnp.where(n == 0, part, ssq2_scr[...] + part)
            ssq2_scr[...] = acc
            rs_ref[...] = jnp.broadcast_to(lax.rsqrt(acc * inv_d + EPS), rs_ref.shape)

    @pl.when(i == 0)
    def _():
        matmul_part()

    @pl.when((i > 0) & (i < nm))
    def _():
        residual_part()
        matmul_part()

    @pl.when(i == nm)
    def _():
        residual_part()


def _out_proj_post(lhs, w, layer, x, g_post, scale, g_next, name, bm=1024, bn=512):
    m, k = lhs.shape
    nm = m // bm
    proj_specs = [pl.BlockSpec((bm, k), lambda i, n: (jnp.minimum(i, nm - 1), 0)),
                  pl.BlockSpec((None, k, bn), lambda i, n: (layer, 0, n))]
    return _fused_proj_post(_plain_out_proj, (lhs, w), proj_specs, x, g_post, scale, g_next, name, bm, bn)


def _fused_proj_post(proj_fn, proj_args, proj_specs, x, g_post, scale, g_next, name, bm, bn):
    m, d = x.shape
    nm, nn = m // bm, d // bn
    emit_h = g_next is not None
    gn = g_next if emit_h else g_post
    res_map = lambda i, n: (jnp.maximum(i - 1, 0), jnp.where(i == 0, 0, n))
    vec_map = lambda i, n: (0, jnp.where(i == 0, 0, n))
    res_spec = pl.BlockSpec((bm, bn), res_map)
    out_shape = [jax.ShapeDtypeStruct((m, d), F32)]
    out_specs = [res_spec]
    scratch = [pltpu.VMEM((nn, bm, bn), BF16), pltpu.VMEM((2, bm, 1), F32)]
    if emit_h:
        out_shape += [jax.ShapeDtypeStruct((m, d), BF16), jax.ShapeDtypeStruct((m, ROW_SCALE_W), F32)]
        out_specs += [res_spec, pl.BlockSpec((bm, ROW_SCALE_W), lambda i, n: (jnp.maximum(i - 1, 0), 0))]
        scratch.append(pltpu.VMEM((bm, 1), F32))
    res = pl.pallas_call(
        functools.partial(_out_proj_kernel, nm, scale, emit_h, proj_fn, len(proj_args)),
        out_shape=out_shape,
        grid=(nm + 1, nn),
        in_specs=list(proj_specs) + [res_spec, pl.BlockSpec((1, bn), vec_map), pl.BlockSpec((1, bn), vec_map)],
        out_specs=out_specs,
        scratch_shapes=scratch,
        compiler_params=_cparams("arbitrary", "arbitrary"),
        name=name,
    )(*proj_args, x, g_post.reshape(1, d), gn.reshape(1, d))
    return (res[0], (res[1], res[2])) if emit_h else (res[0], None)


def _gla_kernel(q_ref, k_ref, v_ref, r_ref, sm_ref, wgk_ref, bgk_ref, ng_ref, o_ref, cum_ref, *st_refs):
    @pl.when(pl.program_id(1) == 0)
    def _():
        for st_ref in st_refs:
            st_ref[...] = jnp.zeros_like(st_ref)

    rows = q_ref.shape[0]
    a_hi, a_mid, _ = _split3(sm_ref[...])
    w_hi, w_mid, _ = _split3(wgk_ref[...])
    z = _dot(a_hi, w_hi) + _dot(a_hi, w_mid) + _dot(a_mid, w_hi) + bgk_ref[...]
    logf = (jnp.minimum(z, 0.0) - jnp.log1p(jnp.exp(-jnp.abs(z)))) * (1.0 / GLA_TAU)
    tri = _chunk_tril(CUM_BLOCK)
    for b in range(rows // CUM_BLOCK):
        sl = slice(b * CUM_BLOCK, (b + 1) * CUM_BLOCK)
        cum_ref[sl, :] = _sel_dot(tri, logf[sl])

    ri = lax.broadcasted_iota(jnp.int32, (CHUNK, CHUNK), 0)
    ci = lax.broadcasted_iota(jnp.int32, (CHUNK, CHUNK), 1)
    tril = ri >= ci
    for c in range(rows // CHUNK):
        sl = pl.ds(c * CHUNK, CHUNK)
        for hh, st_ref in enumerate(st_refs):
            kl = slice(hh * GLA_DK, (hh + 1) * GLA_DK)
            vl = slice(hh * GLA_DV, (hh + 1) * GLA_DV)
            cum = cum_ref[sl, kl]
            cum_end = cum[CHUNK - 1:CHUNK, :]
            up, down = jnp.exp(cum), jnp.exp(-cum)
            q = q_ref[sl, kl].astype(F32) * (GLA_DK ** -0.5)
            k = k_ref[sl, kl].astype(F32)
            v = v_ref[sl, vl]
            q_dec = (q * up).astype(BF16)
            a_past = _dot_nt(q_dec, (k * down).astype(BF16))
            a_future = _dot_nt((q * down).astype(BF16), (k * up).astype(BF16))
            attn = jnp.where(tril, a_past, a_future).astype(BF16)
            state_t = st_ref[...]
            o = _dot(attn, v) + _dot_nt(q_dec, state_t.astype(BF16))
            contrib_t = _dot_tn(v, (k * jnp.exp(cum_end - cum)).astype(BF16))
            st_ref[...] = state_t * jnp.exp(cum_end) + contrib_t
            o = o * lax.rsqrt(jnp.mean(o * o, axis=-1, keepdims=True) + EPS) * ng_ref[:, vl]
            o_ref[sl, vl] = (o * _silu(r_ref[sl, vl].astype(F32))).astype(o_ref.dtype)


def _gla(proj, small, w_gk_pad, b_gk, norm_g):
    s = proj.shape[0]
    rows = min(MIX_ROWS, s)
    hp = GLA_HEADS_PER_STEP
    kw, vw = hp * GLA_DK, hp * GLA_DV
    qb, vb = P_GLA_Q // kw, P_GLA_V // vw
    kb, rb = P_GLA_K // kw, P_GLA_R // vw
    return pl.pallas_call(
        _gla_kernel,
        out_shape=jax.ShapeDtypeStruct((s, GLA_V), BF16),
        grid=(GLA_HEADS // hp, s // rows),
        in_specs=[pl.BlockSpec((rows, kw), lambda h, i: (i, qb + h)),
                  pl.BlockSpec((rows, kw), lambda h, i: (i, kb + h)),
                  pl.BlockSpec((rows, vw), lambda h, i: (i, vb + h)),
                  pl.BlockSpec((rows, vw), lambda h, i: (i, rb + h)),
                  pl.BlockSpec((rows, SMALL_W), lambda h, i: (i, 0)),
                  pl.BlockSpec((SMALL_W, kw), lambda h, i: (0, h)),
                  pl.BlockSpec((1, kw), lambda h, i: (0, h)),
                  pl.BlockSpec((1, vw), lambda h, i: (0, h))],
        out_specs=pl.BlockSpec((rows, vw), lambda h, i: (i, h)),
        scratch_shapes=[pltpu.VMEM((rows, kw), F32)] + [pltpu.VMEM((GLA_DV, GLA_DK), F32)] * hp,
        compiler_params=_cparams("parallel", "arbitrary"),
        name="gla",
    )(proj, proj, proj, proj, small, w_gk_pad, b_gk.reshape(1, GLA_QK), norm_g.reshape(1, GLA_V))


def _causal_conv_silu(xp_ref, x_ref, w_ref, b_ref, first):
    rows = x_ref.shape[0]
    hist = V7X_SUBLANES

    @pl.when(first)
    def _():
        xp_ref[pl.ds(0, hist), :] = jnp.zeros((hist, xp_ref.shape[1]), F32)

    xp_ref[pl.ds(hist, rows), :] = x_ref[...].astype(F32)
    acc = b_ref[...]
    for tap in range(SSM_CONV):
        acc = acc + xp_ref[pl.ds(hist - (SSM_CONV - 1) + tap, rows), :] * w_ref[tap:tap + 1, :]
    xp_ref[pl.ds(0, hist), :] = xp_ref[pl.ds(rows, hist), :]
    return _silu(acc)


def _ssd_kernel(z_ref, xs_ref, b_ref, c_ref, sm_ref, e_ref, cwx_ref, cwb_ref, cwc_ref,
                cbx_ref, cbb_ref, cbc_ref, dtb_ref, alog_ref, dsk_ref, ng_ref,
                o_ref, xpx_ref, xpb_ref, xpc_ref, xs_scr, b_scr, c_scr, dt_scr, cum_scr, *s_refs):
    first = pl.program_id(1) == 0

    @pl.when(first)
    def _():
        for s_ref in s_refs:
            s_ref[...] = jnp.zeros_like(s_ref)

    rows = xs_ref.shape[0]
    xs_scr[...] = _causal_conv_silu(xpx_ref, xs_ref, cwx_ref, cbx_ref, first)
    b_scr[...] = _causal_conv_silu(xpb_ref, b_ref, cwb_ref, cbb_ref, first).astype(BF16)
    c_scr[...] = _causal_conv_silu(xpc_ref, c_ref, cwc_ref, cbc_ref, first).astype(BF16)

    dt_in = sm_ref[...] + dtb_ref[...]
    dt_c = jnp.maximum(dt_in, 0.0) + jnp.log1p(jnp.exp(-jnp.abs(dt_in)))
    da_c = dt_c * (-jnp.exp(alog_ref[...]))
    tri = _chunk_tril(CUM_BLOCK)
    cum_c = jnp.concatenate([_sel_dot(tri, da_c[b * CUM_BLOCK:(b + 1) * CUM_BLOCK])
                             for b in range(rows // CUM_BLOCK)], axis=0)
    dt_scr[...] = _sel_dot_rhs(dt_c, e_ref[0])
    cum_scr[...] = _sel_dot_rhs(cum_c, e_ref[0])

    gw = SSM_GW
    rw = lax.broadcasted_iota(jnp.int32, (CHUNK, gw), 0)
    cw = lax.broadcasted_iota(jnp.int32, (CHUNK, gw), 1)
    diag_tile = (jnp.bitwise_and(cw, SSM_HEADDIM - 1) == rw).astype(F32)
    half = gw // 2
    rb = lax.broadcasted_iota(jnp.int32, (half, half), 0)
    cb_ = lax.broadcasted_iota(jnp.int32, (half, half), 1)
    head_mask = jnp.right_shift(rb, HEAD_SHIFT) == jnp.right_shift(cb_, HEAD_SHIFT)
    reps = half // CHUNK
    for c in range(rows // CHUNK):
        sl = pl.ds(c * CHUNK, CHUNK)
        for gg, s_ref in enumerate(s_refs):
            xl = slice(gg * gw, (gg + 1) * gw)
            nl = slice(gg * SSM_STATE, (gg + 1) * SSM_STATE)
            cum = cum_scr[sl, xl]
            cum_row = jnp.sum(cum * diag_tile, axis=0, keepdims=True)
            seg = jnp.exp(-jnp.abs(cum - cum_row))
            xs = xs_scr[sl, xl]
            bm = b_scr[sl, nl]
            cm = c_scr[sl, nl]
            xdt = xs * dt_scr[sl, xl]
            cbt = _dot_nt(cm, jnp.concatenate([bm] * SSM_HPG, axis=0))
            m_all = (cbt * seg).astype(BF16)
            intra = []
            for hlf in range(2):
                xh = xdt[:, hlf * half:(hlf + 1) * half]
                bd = jnp.where(head_mask, jnp.concatenate([xh] * reps, axis=0), 0.0).astype(BF16)
                intra.append(_dot(m_all[:, hlf * half:(hlf + 1) * half], bd))
            intra = jnp.concatenate(intra, axis=1)
            cum_end = cum[CHUNK - 1:CHUNK, :]
            state = s_ref[...]
            inter = _dot(cm, state.astype(BF16)) * jnp.exp(cum)
            contrib = _dot_tn(bm, (xdt * jnp.exp(cum_end - cum)).astype(BF16))
            s_ref[...] = state * jnp.exp(cum_end) + contrib
            y = intra + inter + xs * dsk_ref[:, xl]
            y = y * _silu(z_ref[sl, xl].astype(F32))
            y = y * lax.rsqrt(jnp.mean(y * y, axis=-1, keepdims=True) + EPS) * ng_ref[:, xl]
            o_ref[sl, xl] = y.astype(o_ref.dtype)


def _ssd(proj, small, expand, conv_w, conv_b, dt_bias, a_log, d_skip, norm_g):
    s = proj.shape[0]
    rows = min(MIX_ROWS, s)
    gp = SSD_GROUPS_PER_STEP
    xw, nw = gp * SSM_GW, gp * SSM_STATE
    zb, xb = P_SSM_Z // xw, P_SSM_X // xw
    bb, cb = P_SSM_B // nw, P_SSM_C // nw
    cwb, cwc = SSM_INNER // nw, (SSM_INNER + SSM_BC) // nw

    def lane_expand(p):
        return jnp.repeat(p.astype(F32), SSM_HEADDIM).reshape(1, SSM_INNER)

    def side_lanes(p):
        return jnp.pad(p.astype(F32), (SMALL_DT0, SMALL_W - SMALL_DT0 - SSM_HEADS)).reshape(1, SMALL_W)

    conv_b2 = conv_b.reshape(1, SSM_CONV_DIM)
    vec = pl.BlockSpec((1, xw), lambda g, i: (0, g))
    side_vec = pl.BlockSpec((1, SMALL_W), lambda g, i: (0, 0))
    return pl.pallas_call(
        _ssd_kernel,
        out_shape=jax.ShapeDtypeStruct((s, SSM_INNER), BF16),
        grid=(SSM_GROUPS // gp, s // rows),
        in_specs=[pl.BlockSpec((rows, xw), lambda g, i: (i, zb + g)),
                  pl.BlockSpec((rows, xw), lambda g, i: (i, xb + g)),
                  pl.BlockSpec((rows, nw), lambda g, i: (i, bb + g)),
                  pl.BlockSpec((rows, nw), lambda g, i: (i, cb + g)),
                  pl.BlockSpec((rows, SMALL_W), lambda g, i: (i, 0)),
                  pl.BlockSpec((1, SMALL_W, xw), lambda g, i: (g, 0, 0)),
                  pl.BlockSpec((SSM_CONV, xw), lambda g, i: (0, g)),
                  pl.BlockSpec((SSM_CONV, nw), lambda g, i: (0, cwb + g)),
                  pl.BlockSpec((SSM_CONV, nw), lambda g, i: (0, cwc + g)),
                  pl.BlockSpec((1, xw), lambda g, i: (0, g)),
                  pl.BlockSpec((1, nw), lambda g, i: (0, cwb + g)),
                  pl.BlockSpec((1, nw), lambda g, i: (0, cwc + g)),
                  side_vec, side_vec, vec, vec],
        out_specs=pl.BlockSpec((rows, xw), lambda g, i: (i, g)),
        scratch_shapes=[pltpu.VMEM((rows + V7X_SUBLANES, xw), F32),
                        pltpu.VMEM((rows + V7X_SUBLANES, nw), F32),
                        pltpu.VMEM((rows + V7X_SUBLANES, nw), F32),
                        pltpu.VMEM((rows, xw), F32),
                        pltpu.VMEM((rows, nw), BF16),
                        pltpu.VMEM((rows, nw), BF16),
                        pltpu.VMEM((rows, xw), F32),
                        pltpu.VMEM((rows, xw), F32)] + [pltpu.VMEM((SSM_STATE, SSM_GW), F32)] * gp,
        compiler_params=_cparams("parallel", "arbitrary"),
        name="ssd",
    )(proj, proj, proj, proj, small, expand, conv_w, conv_w, conv_w, conv_b2, conv_b2, conv_b2,
      side_lanes(dt_bias), side_lanes(a_log), lane_expand(d_skip), norm_g.reshape(1, SSM_INNER))


def _ssd_expand_table():
    xw = SSD_GROUPS_PER_STEP * SSM_GW
    rows = jnp.arange(SMALL_W, dtype=jnp.int32)[None, :, None]
    cols = jnp.arange(xw, dtype=jnp.int32)[None, None, :]
    step = jnp.arange(SSM_INNER // xw, dtype=jnp.int32)[:, None, None]
    return (rows == SMALL_DT0 + (step * xw + cols) // SSM_HEADDIM).astype(BF16)


def _rope_table_kernel(pos_ref, freq_ref, sign_ref, cos_ref, sin_ref):
    ang = pos_ref[...] * freq_ref[...]
    cos_ref[...] = jnp.cos(ang)
    sin_ref[...] = jnp.sin(ang) * sign_ref[...]


def _rope_tables(positions):
    s = positions.shape[-1]
    half = RET_DK // 2
    inv_freq = ROPE_BASE ** (-jnp.arange(half, dtype=F32) / half)
    freq = jnp.concatenate([inv_freq, inv_freq]).reshape(1, RET_DK)
    sign = jnp.concatenate([-jnp.ones((half,), F32), jnp.ones((half,), F32)]).reshape(1, RET_DK)
    pos = positions.astype(F32).reshape(s, 1)
    rows = min(1024, s)
    tab = jax.ShapeDtypeStruct((s, RET_DK), F32)
    return pl.pallas_call(
        _rope_table_kernel,
        out_shape=[tab, tab],
        grid=(s // rows,),
        in_specs=[pl.BlockSpec((rows, 1), lambda i: (i, 0)),
                  pl.BlockSpec((1, RET_DK), lambda i: (0, 0)),
                  pl.BlockSpec((1, RET_DK), lambda i: (0, 0))],
        out_specs=[pl.BlockSpec((rows, RET_DK), lambda i: (i, 0))] * 2,
        compiler_params=_cparams("parallel"),
        name="rope_tables",
    )(pos, freq, sign)


def _ret_kernel(q_ref, k_ref, v_ref, g_ref, cos_ref, sin_ref, dmat_ref, qdec_ref, kdec_ref, sdec_ref,
                ng_ref, o_ref, *s_refs):
    @pl.when(pl.program_id(1) == 0)
    def _():
        for s_ref in s_refs:
            s_ref[...] = jnp.zeros_like(s_ref)

    cos, sin = cos_ref[...], sin_ref[...]

    def rot(t):
        t = t.astype(F32)
        return t * cos + pltpu.roll(t, RET_DK // 2, axis=1) * sin

    for hh, s_ref in enumerate(s_refs):
        kl = slice(hh * RET_DK, (hh + 1) * RET_DK)
        vl = slice(hh * RET_DV, (hh + 1) * RET_DV)
        q = rot(q_ref[:, kl])
        k = rot(k_ref[:, kl]) * (RET_DK ** -0.5)
        v = v_ref[:, vl]
        scores = (_dot_nt(q.astype(BF16), k.astype(BF16)) * dmat_ref[hh]).astype(BF16)
        state = s_ref[...]
        o = _dot(scores, v) + _dot((q * qdec_ref[hh]).astype(BF16), state.astype(BF16))
        s_ref[...] = state * sdec_ref[hh] + _dot_tn((k * kdec_ref[hh]).astype(BF16), v)
        o = o * lax.rsqrt(jnp.mean(o * o, axis=-1, keepdims=True) + EPS) * ng_ref[:, vl]
        o_ref[:, vl] = (o * _silu(g_ref[:, vl].astype(F32))).astype(o_ref.dtype)


def _ret_tables(rows):
    log_gamma = jnp.log1p(-jnp.exp2(-5.0 - jnp.arange(RET_HEADS, dtype=F32)))[:, None, None]
    i = jnp.arange(rows, dtype=jnp.int32)
    dist = (i[:, None] - i[None, :]).astype(F32)[None]
    ch_i, ch_j = (i // CHUNK)[:, None], (i // CHUNK)[None, :]
    dmat = jnp.where((ch_i == ch_j)[None], jnp.exp(log_gamma * jnp.abs(dist)),
                     jnp.where((ch_j < ch_i)[None], jnp.exp(log_gamma * dist), 0.0))
    fi = i.astype(F32)[None, :, None]
    qdec = jnp.broadcast_to(jnp.exp(log_gamma * (fi + 1.0)), (RET_HEADS, rows, RET_DK))
    kdec = jnp.broadcast_to(jnp.exp(log_gamma * (rows - 1.0 - fi)), (RET_HEADS, rows, RET_DK))
    sdec = jnp.broadcast_to(jnp.exp(log_gamma * float(rows)), (RET_HEADS, 1, RET_DV))
    return dmat, qdec, kdec, sdec


def _retention(proj, cos_t, sin_t, tables, norm_g):
    s = proj.shape[0]
    rows = min(RET_BLOCK, s)
    dmat, qdec, kdec, sdec = tables
    hp = RET_HEADS_PER_STEP
    kw, vw = hp * RET_DK, hp * RET_DV
    qb, kb = P_RET_Q // kw, P_RET_K // kw
    vb, gb = P_RET_V // vw, P_RET_G // vw
    return pl.pallas_call(
        _ret_kernel,
        out_shape=jax.ShapeDtypeStruct((s, RET_V), BF16),
        grid=(RET_HEADS // hp, s // rows),
        in_specs=[pl.BlockSpec((rows, kw), lambda h, i: (i, qb + h)),
                  pl.BlockSpec((rows, kw), lambda h, i: (i, kb + h)),
                  pl.BlockSpec((rows, vw), lambda h, i: (i, vb + h)),
                  pl.BlockSpec((rows, vw), lambda h, i: (i, gb + h)),
                  pl.BlockSpec((rows, RET_DK), lambda h, i: (i, 0)),
                  pl.BlockSpec((rows, RET_DK), lambda h, i: (i, 0)),
                  pl.BlockSpec((hp, rows, rows), lambda h, i: (h, 0, 0)),
                  pl.BlockSpec((hp, rows, RET_DK), lambda h, i: (h, 0, 0)),
                  pl.BlockSpec((hp, rows, RET_DK), lambda h, i: (h, 0, 0)),
                  pl.BlockSpec((hp, 1, RET_DV), lambda h, i: (h, 0, 0)),
                  pl.BlockSpec((1, vw), lambda h, i: (0, h))],
        out_specs=pl.BlockSpec((rows, vw), lambda h, i: (i, h)),
        scratch_shapes=[pltpu.VMEM((RET_DK, RET_DV), F32)] * hp,
        compiler_params=_cparams("parallel", "arbitrary"),
        name="retention",
    )(proj, proj, proj, proj, cos_t, sin_t, dmat, qdec, kdec, sdec, norm_g.reshape(1, RET_V))


def _gated_merge_proj(gl_ref, y0_ref, y1_ref, y2_ref, gu0_ref, gu1_ref, gu2_ref, gb0_ref, gb1_ref, gb2_ref,
                      wo0_ref, wo1_ref, wo2_ref):
    gl = gl_ref[...]
    acc = None
    for y_ref, gu_ref, gb_ref, wo_ref in ((y0_ref, gu0_ref, gb0_ref, wo0_ref),
                                          (y1_ref, gu1_ref, gb1_ref, wo1_ref),
                                          (y2_ref, gu2_ref, gb2_ref, wo2_ref)):
        gate = jax.nn.sigmoid(_dot(gl, gu_ref[...]) + gb_ref[...])
        term = gate * _dot(y_ref[...], wo_ref[...])
        acc = term if acc is None else acc + term
    return acc


def _merge_post(proj, ys, gate_w2, gate_b, w_out, layer, x, g_post, g_next, bm=1024, bn=256):
    s, d = x.shape
    nm = s // bm
    gate_b4 = gate_b.reshape(DEPTH, N_BRANCH, 1, d)
    row = lambda i: jnp.minimum(i, nm - 1)
    y_spec = pl.BlockSpec((bm, BRANCH_W), lambda i, j: (row(i), 0))
    specs = [pl.BlockSpec((bm, GATE_RANK), lambda i, j: (row(i), P_GATE // GATE_RANK)), y_spec, y_spec, y_spec]
    specs += [pl.BlockSpec((None, GATE_RANK, bn),
                           functools.partial(lambda n, i, j: (layer * N_BRANCH + n, 0, j), n))
              for n in range(N_BRANCH)]
    specs += [pl.BlockSpec((None, None, 1, bn), functools.partial(lambda n, i, j: (layer, n, 0, j), n))
              for n in range(N_BRANCH)]
    specs += [pl.BlockSpec((None, BRANCH_W, bn),
                           functools.partial(lambda n, i, j: (layer * N_BRANCH + n, 0, j), n))
              for n in range(N_BRANCH)]
    args = (proj, ys[0], ys[1], ys[2], gate_w2, gate_w2, gate_w2, gate_b4, gate_b4, gate_b4, w_out, w_out, w_out)
    return _fused_proj_post(_gated_merge_proj, args, specs, x, g_post, 1.0, g_next, "merge_post", bm, bn)


def _xattn_kernel(q_ref, kv_ref, o_ref):
    for h in range(XA_HEADS):
        lo, hi = h * XA_DIM, (h + 1) * XA_DIM
        s = _dot_nt(q_ref[:, lo:hi], kv_ref[:, lo:hi]) * (XA_DIM ** -0.5)
        p = jnp.exp(s - jnp.max(s, axis=-1, keepdims=True))
        p = p / jnp.sum(p, axis=-1, keepdims=True)
        o_ref[:, lo:hi] = _dot(p.astype(BF16), kv_ref[:, XA_W + lo:XA_W + hi]).astype(o_ref.dtype)


def _xattn(q, kv, rows=512):
    s = q.shape[0]
    rows = min(rows, s)
    n_mem = kv.shape[0]
    return pl.pallas_call(
        _xattn_kernel,
        out_shape=jax.ShapeDtypeStruct((s, XA_W), BF16),
        grid=(s // rows,),
        in_specs=[pl.BlockSpec((rows, XA_W), lambda i: (i, 0)),
                  pl.BlockSpec((n_mem, 2 * XA_W), lambda i: (0, 0))],
        out_specs=pl.BlockSpec((rows, XA_W), lambda i: (i, 0)),
        compiler_params=_cparams("parallel"),
        name="xattn",
    )(q, kv)


def kernel(x, mem, positions, norm_g, mem_norm_g, ffn1_w_in, ffn1_w_out, mix_w_in, gla_w_gk, gla_b_gk,
           gla_norm_g, ssm_conv_w, ssm_conv_b, ssm_dt_bias, ssm_a_log, ssm_d, ssm_norm_g, ret_norm_g,
           gate_w_up, gate_b, mix_w_out, xattn_w_q, xattn_w_kv, xattn_w_o, ffn2_w_in, ffn2_w_out):
    b, s, d = x.shape
    assert b == 1 and d == D_MODEL and s % max(RET_BLOCK, MIX_ROWS, 1024) == 0
    nl = DEPTH
    xr = x.reshape(s, d)
    memr = mem.reshape(mem.shape[1], d)
    cos_t, sin_t = _rope_tables(positions)
    ret_tabs = _ret_tables(min(RET_BLOCK, s))
    expand = _ssd_expand_table()

    ffn1_out = _cast_bf16(ffn1_w_out.reshape(nl * D_FF, d), 512).reshape(nl, D_FF, d)
    ffn2_out = _cast_bf16(ffn2_w_out.reshape(nl * D_FF, d), 512).reshape(nl, D_FF, d)
    xa_out = _cast_bf16(xattn_w_o.reshape(nl * XA_W, d), 512).reshape(nl, XA_W, d)
    mix_out = _cast_bf16(mix_w_out.reshape(nl * N_BRANCH * BRANCH_W, d), 512).reshape(nl * N_BRANCH, BRANCH_W, d)
    gate_up = _cast_bf16(jnp.transpose(gate_w_up, (0, 2, 1, 3)).reshape(nl * N_BRANCH * GATE_RANK, d),
                         512).reshape(nl * N_BRANCH, GATE_RANK, d)
    mix_w_in_t = jnp.transpose(mix_w_in, (0, 2, 1))
    mix_in_big = _repack_mix_w_in(mix_w_in_t)
    mix_in_side_t = _side_mix_w_in(mix_w_in_t)
    w_gk_pad = jnp.concatenate([gla_w_gk, jnp.zeros((nl, SMALL_W - GLA_RANK, GLA_QK), F32)], axis=1)

    h = (_norm_cast(xr, norm_g[0, 0]), None)
    for l in range(nl):
        ng = norm_g[l]
        a = _swiglu_in(h, ffn1_w_in, l)
        xr, h = _out_proj_post(a, ffn1_out, l, xr, ng[1], 0.5, ng[2], "ffn_out_post")
        proj, small = _mix_in(h, mix_in_big, mix_in_side_t, l)
        y_gla = _gla(proj, small, w_gk_pad[l], gla_b_gk[l], gla_norm_g[l])
        y_ssd = _ssd(proj, small, expand, ssm_conv_w[l], ssm_conv_b[l], ssm_dt_bias[l], ssm_a_log[l],
                     ssm_d[l], ssm_norm_g[l])
        y_ret = _retention(proj, cos_t, sin_t, ret_tabs, ret_norm_g[l])
        xr, h = _merge_post(proj, (y_gla, y_ssd, y_ret), gate_up, gate_b, mix_out, l, xr, ng[3], ng[4])
        q = _matmul_wcast(h, xattn_w_q, l, BF16, 1024, 512, "xattn_q")
        mem_n = _norm_cast(memr, mem_norm_g[l])
        kv = _matmul_wcast((mem_n, None), xattn_w_kv, l, BF16, 256, 512, "xattn_kv")
        o = _xattn(q, kv)
        xr, h = _out_proj_post(o, xa_out, l, xr, ng[5], 1.0, ng[6], "xattn_out_post", bn=1024)
        a = _swiglu_in(h, ffn2_w_in, l)
        g_next = norm_g[l + 1, 0] if l + 1 < nl else None
        xr, h = _out_proj_post(a, ffn2_out, l, xr, ng[7], 0.5, g_next, "ffn_out_post")
    return xr.reshape(b, s, d)
```

```python
import functools

import numpy as np
import jax
import jax.numpy as jnp
from jax import lax
from jax.experimental import pallas as pl
from jax.experimental.pallas import tpu as pltpu

F32 = jnp.float32
BF16 = jnp.bfloat16

D_MODEL = 4096
DEPTH = 4
CHUNK = 64
D_FF = 4096
EPS = 1e-6
GLA_HEADS, GLA_DK, GLA_DV, GLA_RANK, GLA_TAU = 4, 256, 512, 16, 16.0
GLA_QK, GLA_V = GLA_HEADS * GLA_DK, GLA_HEADS * GLA_DV
SSM_HEADS, SSM_HEADDIM, SSM_GROUPS, SSM_STATE, SSM_CONV = 32, 64, 4, 128, 4
SSM_INNER = SSM_HEADS * SSM_HEADDIM
SSM_BC = SSM_GROUPS * SSM_STATE
SSM_CONV_DIM = SSM_INNER + 2 * SSM_BC
SSM_HPG = SSM_HEADS // SSM_GROUPS
SSM_GW = SSM_INNER // SSM_GROUPS
RET_HEADS, RET_DK, RET_DV = 8, 128, 256
RET_QK, RET_V = RET_HEADS * RET_DK, RET_HEADS * RET_DV
ROPE_BASE = 10000.0
XA_HEADS, XA_DIM = 4, 256
XA_W = XA_HEADS * XA_DIM
N_BRANCH, BRANCH_W, GATE_RANK = 3, 2048, 512

_IN_WIDTHS = (GLA_QK, GLA_QK, GLA_V, GLA_V, GLA_RANK, SSM_INNER, SSM_CONV_DIM, SSM_HEADS,
              RET_QK, RET_QK, RET_V, RET_V, GATE_RANK)
_IN_OFF = np.concatenate([[0], np.cumsum(_IN_WIDTHS)]).tolist()
IN_COLS = _IN_OFF[-1]
P_GLA_Q, P_GLA_K, P_GLA_V, P_GLA_R = 0, 1024, 2048, 4096
P_SSM_Z, P_SSM_X, P_SSM_B, P_SSM_C = 6144, 8192, 10240, 10752
P_RET_Q, P_RET_K, P_RET_V, P_RET_G = 11264, 12288, 13312, 15360
P_GATE = 17408
P_COLS = 17920
SMALL_W = 128
SMALL_DT0 = GLA_RANK

V7X_VMEM_LIMIT = 60 * 1024 * 1024
V7X_SUBLANES = 8
CHUNK_SHIFT = CHUNK.bit_length() - 1
HEAD_SHIFT = SSM_HEADDIM.bit_length() - 1

RET_BLOCK = 512
MIX_ROWS = 512
CUM_BLOCK = 256
GLA_HEADS_PER_STEP = 4
SSD_GROUPS_PER_STEP = 2
RET_HEADS_PER_STEP = 4


def _cparams(*sem):
    return pltpu.CompilerParams(dimension_semantics=sem, vmem_limit_bytes=V7X_VMEM_LIMIT)


def _silu(x):
    return x * jax.nn.sigmoid(x)


def _dot(a, b):
    return jnp.dot(a, b, preferred_element_type=F32)


def _dot_nt(a, b):
    return lax.dot_general(a, b, (((1,), (1,)), ((), ())), preferred_element_type=F32)


def _dot_tn(a, b):
    return lax.dot_general(a, b, (((0,), (0,)), ((), ())), preferred_element_type=F32)


def _split3(x):
    hi = x.astype(BF16)
    r1 = x - hi.astype(F32)
    mid = r1.astype(BF16)
    lo = (r1 - mid.astype(F32)).astype(BF16)
    return hi, mid, lo


def _sel_dot(sel, x):
    hi, mid, lo = _split3(x)
    return _dot(sel, hi) + _dot(sel, mid) + _dot(sel, lo)


def _sel_dot_rhs(x, sel):
    hi, mid, lo = _split3(x)
    return _dot(hi, sel) + _dot(mid, sel) + _dot(lo, sel)


def _chunk_tril(n):
    ri = lax.broadcasted_iota(jnp.int32, (n, n), 0)
    ci = lax.broadcasted_iota(jnp.int32, (n, n), 1)
    same = jnp.right_shift(ri, CHUNK_SHIFT) == jnp.right_shift(ci, CHUNK_SHIFT)
    return jnp.where(same & (ri >= ci), 1.0, 0.0).astype(BF16)


def _cast_kernel(w_ref, o_ref):
    o_ref[...] = w_ref[...].astype(o_ref.dtype)


def _cast_bf16(w, rows):
    r, c = w.shape
    return pl.pallas_call(
        _cast_kernel,
        out_shape=jax.ShapeDtypeStruct((r, c), BF16),
        grid=(r // rows,),
        in_specs=[pl.BlockSpec((rows, c), lambda i: (i, 0))],
        out_specs=pl.BlockSpec((rows, c), lambda i: (i, 0)),
        compiler_params=_cparams("parallel"),
        name="cast_bf16",
    )(w)


REPACK_COLS = 512


def _packed_to_source_col(c):
    skipped = jnp.where(c >= P_SSM_Z, GLA_RANK, 0) + jnp.where(c >= P_RET_Q, SSM_HEADS, 0)
    return pl.multiple_of(c + skipped, GLA_RANK)


def _repack_kernel(w_ref, o_ref):
    o_ref[...] = w_ref[...].T.astype(o_ref.dtype)


def _repack_mix_w_in(w_t):
    nl, _, d = w_t.shape
    return pl.pallas_call(
        _repack_kernel,
        out_shape=jax.ShapeDtypeStruct((nl, d, P_COLS), BF16),
        grid=(nl, P_COLS // REPACK_COLS),
        in_specs=[pl.BlockSpec((pl.Squeezed(), pl.Element(REPACK_COLS), pl.Element(d)),
                               lambda l, j: (l, _packed_to_source_col(j * REPACK_COLS), 0))],
        out_specs=pl.BlockSpec((None, d, REPACK_COLS), lambda l, j: (l, 0, j)),
        compiler_params=_cparams("parallel", "parallel"),
        name="repack_mix_w_in",
    )(w_t)


def _side_w_kernel(gk_ref, dt_ref, o_ref):
    o_ref[...] = jnp.zeros(o_ref.shape, o_ref.dtype)
    o_ref[0:GLA_RANK, :] = gk_ref[...].astype(o_ref.dtype)
    o_ref[GLA_RANK:GLA_RANK + SSM_HEADS, :] = dt_ref[...].astype(o_ref.dtype)


def _side_mix_w_in(w_t):
    nl, _, d = w_t.shape
    o = _IN_OFF
    return pl.pallas_call(
        _side_w_kernel,
        out_shape=jax.ShapeDtypeStruct((nl, SMALL_W, d), BF16),
        grid=(nl,),
        in_specs=[pl.BlockSpec((pl.Squeezed(), pl.Element(GLA_RANK), pl.Element(d)), lambda l: (l, o[4], 0)),
                  pl.BlockSpec((pl.Squeezed(), pl.Element(SSM_HEADS), pl.Element(d)), lambda l: (l, o[7], 0))],
        out_specs=pl.BlockSpec((None, SMALL_W, d), lambda l: (l, 0, 0)),
        compiler_params=_cparams("parallel"),
        name="side_mix_w_in",
    )(w_t, w_t)


def _norm_cast_kernel(x_ref, g_ref, o_ref):
    x = x_ref[...]
    r = lax.rsqrt(jnp.mean(x * x, axis=-1, keepdims=True) + EPS)
    o_ref[...] = (x * r * g_ref[...]).astype(o_ref.dtype)


def _norm_cast(x, g, rows=256):
    m, d = x.shape
    rows = min(rows, m)
    return pl.pallas_call(
        _norm_cast_kernel,
        out_shape=jax.ShapeDtypeStruct((m, d), BF16),
        grid=(m // rows,),
        in_specs=[pl.BlockSpec((rows, d), lambda i: (i, 0)), pl.BlockSpec((1, d), lambda i: (0, 0))],
        out_specs=pl.BlockSpec((rows, d), lambda i: (i, 0)),
        compiler_params=_cparams("parallel"),
        name="norm_cast",
    )(x, g.reshape(1, d))


ROW_SCALE_W = 128


def _mix_in_kernel(x_ref, rs_ref, w_ref, wside_ref, o_ref, side_ref):
    x = x_ref[...]
    rs = rs_ref[:, 0:1]
    o_ref[...] = (_dot(x, w_ref[...]) * rs).astype(o_ref.dtype)

    @pl.when(pl.program_id(1) == 0)
    def _():
        side_ref[...] = _dot_nt(x, wside_ref[...]) * rs


def _mix_in(act, w, w_side_t, layer, bm=1024, bn=1280):
    x, rs = act
    m, k = x.shape
    n = w.shape[2]
    return pl.pallas_call(
        _mix_in_kernel,
        out_shape=[jax.ShapeDtypeStruct((m, n), BF16), jax.ShapeDtypeStruct((m, SMALL_W), F32)],
        grid=(m // bm, n // bn),
        in_specs=[pl.BlockSpec((bm, k), lambda i, j: (i, 0)),
                  pl.BlockSpec((bm, ROW_SCALE_W), lambda i, j: (i, 0)),
                  pl.BlockSpec((None, k, bn), lambda i, j: (layer, 0, j)),
                  pl.BlockSpec((None, SMALL_W, k), lambda i, j: (layer, 0, 0))],
        out_specs=[pl.BlockSpec((bm, bn), lambda i, j: (i, j)),
                   pl.BlockSpec((bm, SMALL_W), lambda i, j: (i, 0))],
        compiler_params=_cparams("parallel", "arbitrary"),
        name="mix_in",
    )(x, rs, w, w_side_t)


def _mm_wcast_kernel(scaled, x_ref, *refs):
    w_ref, o_ref, w_scr = refs[-3:]

    @pl.when(pl.program_id(1) == 0)
    def _():
        w_scr[...] = w_ref[...].astype(BF16)

    r = _dot(x_ref[...], w_scr[...])
    if scaled:
        r = r * refs[0][:, 0:1]
    o_ref[...] = r.astype(o_ref.dtype)


def _matmul_wcast(act, w, layer, out_dtype, bm, bn, name):
    x, rs = act
    m, k = x.shape
    n = w.shape[2]
    bm, bn = min(bm, m), min(bn, n)
    scaled = rs is not None
    in_specs = [pl.BlockSpec((bm, k), lambda j, i: (i, 0))]
    if scaled:
        in_specs.append(pl.BlockSpec((bm, ROW_SCALE_W), lambda j, i: (i, 0)))
    in_specs.append(pl.BlockSpec((None, k, bn), lambda j, i: (layer, 0, j)))
    return pl.pallas_call(
        functools.partial(_mm_wcast_kernel, scaled),
        out_shape=jax.ShapeDtypeStruct((m, n), out_dtype),
        grid=(n // bn, m // bm),
        in_specs=in_specs,
        out_specs=pl.BlockSpec((bm, bn), lambda j, i: (i, j)),
        scratch_shapes=[pltpu.VMEM((k, bn), BF16)],
        compiler_params=_cparams("arbitrary", "arbitrary"),
        name=name,
    )(*((x, rs, w) if scaled else (x, w)))


def _swiglu_kernel(bn, scaled, n_side, x_ref, *refs):
    first_w = 1 if scaled else 0
    wg_ref, wu_ref = refs[first_w:first_w + 2]
    side_refs = refs[first_w + 2:first_w + 2 + n_side]
    o_ref = refs[first_w + 2 + n_side]
    side_o_refs = refs[first_w + 3 + n_side:first_w + 3 + 2 * n_side]
    w_scr = refs[-1]

    @pl.when(pl.program_id(1) == 0)
    def _():
        w_scr[:, 0:bn] = wg_ref[...].astype(BF16)
        w_scr[:, bn:2 * bn] = wu_ref[...].astype(BF16)

    r = _dot(x_ref[...], w_scr[...])
    if scaled:
        r = r * refs[0][:, 0:1]
    o_ref[...] = (_silu(r[:, 0:bn]) * r[:, bn:2 * bn]).astype(o_ref.dtype)
    for s_ref, so_ref in zip(side_refs, side_o_refs):
        so_ref[...] = s_ref[...].astype(so_ref.dtype)


def _swiglu_in(act, w, layer, side_ws, bm=512, bn=512):
    x, rs = act
    m, k = x.shape
    f = w.shape[2] // 2
    nb, nmb = f // bn, m // bm
    steps = nb * nmb
    scaled = rs is not None
    in_specs = [pl.BlockSpec((bm, k), lambda j, i: (i, 0))]
    if scaled:
        in_specs.append(pl.BlockSpec((bm, ROW_SCALE_W), lambda j, i: (i, 0)))
    in_specs += [pl.BlockSpec((None, k, bn), lambda j, i: (layer, 0, j)),
                 pl.BlockSpec((None, k, bn), lambda j, i: (layer, 0, j + nb))]
    out_shape = [jax.ShapeDtypeStruct((m, f), BF16)]
    out_specs = [pl.BlockSpec((bm, bn), lambda j, i: (i, j))]
    for sw in side_ws:
        _, r_side, c_side = sw.shape
        slab = r_side // steps
        in_specs.append(pl.BlockSpec((None, slab, c_side), lambda j, i: (layer, j * nmb + i, 0)))
        out_shape.append(jax.ShapeDtypeStruct((1, r_side, c_side), BF16))
        out_specs.append(pl.BlockSpec((None, slab, c_side), lambda j, i: (0, j * nmb + i, 0)))
    args = ((x, rs) if scaled else (x,)) + (w, w) + tuple(side_ws)
    return pl.pallas_call(
        functools.partial(_swiglu_kernel, bn, scaled, len(side_ws)),
        out_shape=out_shape,
        grid=(nb, nmb),
        in_specs=in_specs,
        out_specs=out_specs,
        scratch_shapes=[pltpu.VMEM((k, 2 * bn), BF16)],
        compiler_params=_cparams("arbitrary", "arbitrary"),
        name="swiglu_in",
    )(*args)


def _plain_out_proj(lhs_ref, w_ref):
    return _dot(lhs_ref[...], w_ref[...])


def _out_proj_kernel(nm, scale, emit_h, proj_fn, n_proj, *refs):
    proj_refs = refs[:n_proj]
    x_ref, gp_ref, gn_ref, xo_ref = refs[n_proj:n_proj + 4]
    rest = refs[n_proj + 4:]
    if emit_h:
        ho_ref, rs_ref, y_scr, ssq_scr, ssq2_scr = rest
    else:
        y_scr, ssq_scr = rest
    i, n = pl.program_id(0), pl.program_id(1)
    cur = lax.rem(i, 2)
    prev = 1 - cur
    inv_d = 1.0 / (y_scr.shape[0] * y_scr.shape[2])

    def matmul_part():
        yb = proj_fn(*proj_refs)
        y_scr[n] = yb.astype(BF16)
        part = jnp.sum(yb * yb, axis=-1, keepdims=True)
        ssq_scr[cur] = jnp.where(n == 0, part, ssq_scr[cur] + part)

    def residual_part():
        r = lax.rsqrt(ssq_scr[prev] * inv_d + EPS)
        xn = x_ref[...] + scale * (y_scr[n].astype(F32) * r * gp_ref[...])
        xo_ref[...] = xn
        if emit_h:
            ho_ref[...] = (xn * gn_ref[...]).astype(BF16)
            part = jnp.sum(xn * xn, axis=-1, keepdims=True)
            acc = jnp.where(n == 0, part, ssq2_scr[...] + part)
            ssq2_scr[...] = acc
            rs_ref[...] = jnp.broadcast_to(lax.rsqrt(acc * inv_d + EPS), rs_ref.shape)

    @pl.when(i == 0)
    def _():
        matmul_part()

    @pl.when((i > 0) & (i < nm))
    def _():
        residual_part()
        matmul_part()

    @pl.when(i == nm)
    def _():
        residual_part()


def _out_proj_post(lhs, w, layer, x, g_post, scale, g_next, name, bm=1024, bn=512):
    m, k = lhs.shape
    nm = m // bm
    proj_specs = [pl.BlockSpec((bm, k), lambda i, n: (jnp.minimum(i, nm - 1), 0)),
                  pl.BlockSpec((None, k, bn), lambda i, n: (layer, 0, n))]
    return _fused_proj_post(_plain_out_proj, (lhs, w), proj_specs, x, g_post, scale, g_next, name, bm, bn)


def _fused_proj_post(proj_fn, proj_args, proj_specs, x, g_post, scale, g_next, name, bm, bn):
    m, d = x.shape
    nm, nn = m // bm, d // bn
    emit_h = g_next is not None
    gn = g_next if emit_h else g_post
    res_map = lambda i, n: (jnp.maximum(i - 1, 0), jnp.where(i == 0, 0, n))
    vec_map = lambda i, n: (0, jnp.where(i == 0, 0, n))
    res_spec = pl.BlockSpec((bm, bn), res_map)
    out_shape = [jax.ShapeDtypeStruct((m, d), F32)]
    out_specs = [res_spec]
    scratch = [pltpu.VMEM((nn, bm, bn), BF16), pltpu.VMEM((2, bm, 1), F32)]
    if emit_h:
        out_shape += [jax.ShapeDtypeStruct((m, d), BF16), jax.ShapeDtypeStruct((m, ROW_SCALE_W), F32)]
        out_specs += [res_spec, pl.BlockSpec((bm, ROW_SCALE_W), lambda i, n: (jnp.maximum(i - 1, 0), 0))]
        scratch.append(pltpu.VMEM((bm, 1), F32))
    res = pl.pallas_call(
        functools.partial(_out_proj_kernel, nm, scale, emit_h, proj_fn, len(proj_args)),
        out_shape=out_shape,
        grid=(nm + 1, nn),
        in_specs=list(proj_specs) + [res_spec, pl.BlockSpec((1, bn), vec_map), pl.BlockSpec((1, bn), vec_map)],
        out_specs=out_specs,
        scratch_shapes=scratch,
        compiler_params=_cparams("arbitrary", "arbitrary"),
        name=name,
    )(*proj_args, x, g_post.reshape(1, d), gn.reshape(1, d))
    return (res[0], (res[1], res[2])) if emit_h else (res[0], None)


def _gla_kernel(q_ref, k_ref, v_ref, r_ref, sm_ref, wgk_ref, bgk_ref, ng_ref, o_ref, cum_ref, *st_refs):
    @pl.when(pl.program_id(1) == 0)
    def _():
        for st_ref in st_refs:
            st_ref[...] = jnp.zeros_like(st_ref)

    rows = q_ref.shape[0]
    a_hi, a_mid, _ = _split3(sm_ref[...])
    w_hi, w_mid, _ = _split3(wgk_ref[...])
    z = _dot(a_hi, w_hi) + _dot(a_hi, w_mid) + _dot(a_mid, w_hi) + bgk_ref[...]
    logf = (jnp.minimum(z, 0.0) - jnp.log1p(jnp.exp(-jnp.abs(z)))) * (1.0 / GLA_TAU)
    tri = _chunk_tril(CUM_BLOCK)
    for b in range(rows // CUM_BLOCK):
        sl = slice(b * CUM_BLOCK, (b + 1) * CUM_BLOCK)
        cum_ref[sl, :] = _sel_dot(tri, logf[sl])

    ri = lax.broadcasted_iota(jnp.int32, (CHUNK, CHUNK), 0)
    ci = lax.broadcasted_iota(jnp.int32, (CHUNK, CHUNK), 1)
    tril = ri >= ci
    for c in range(rows // CHUNK):
        sl = pl.ds(c * CHUNK, CHUNK)
        for hh, st_ref in enumerate(st_refs):
            kl = slice(hh * GLA_DK, (hh + 1) * GLA_DK)
            vl = slice(hh * GLA_DV, (hh + 1) * GLA_DV)
            cum = cum_ref[sl, kl]
            cum_end = cum[CHUNK - 1:CHUNK, :]
            up, down = jnp.exp(cum), jnp.exp(-cum)
            q = q_ref[sl, kl].astype(F32) * (GLA_DK ** -0.5)
            k = k_ref[sl, kl].astype(F32)
            v = v_ref[sl, vl]
            q_dec = (q * up).astype(BF16)
            a_past = _dot_nt(q_dec, (k * down).astype(BF16))
            a_future = _dot_nt((q * down).astype(BF16), (k * up).astype(BF16))
            attn = jnp.where(tril, a_past, a_future).astype(BF16)
            state_t = st_ref[...]
            o = _dot(attn, v) + _dot_nt(q_dec, state_t.astype(BF16))
            contrib_t = _dot_tn(v, (k * jnp.exp(cum_end - cum)).astype(BF16))
            st_ref[...] = state_t * jnp.exp(cum_end) + contrib_t
            o = o * lax.rsqrt(jnp.mean(o * o, axis=-1, keepdims=True) + EPS) * ng_ref[:, vl]
            o_ref[sl, vl] = (o * _silu(r_ref[sl, vl].astype(F32))).astype(o_ref.dtype)


def _gla(proj, small, w_gk_pad, b_gk, norm_g):
    s = proj.shape[0]
    rows = min(MIX_ROWS, s)
    hp = GLA_HEADS_PER_STEP
    kw, vw = hp * GLA_DK, hp * GLA_DV
    qb, vb = P_GLA_Q // kw, P_GLA_V // vw
    kb, rb = P_GLA_K // kw, P_GLA_R // vw
    return pl.pallas_call(
        _gla_kernel,
        out_shape=jax.ShapeDtypeStruct((s, GLA_V), BF16),
        grid=(GLA_HEADS // hp, s // rows),
        in_specs=[pl.BlockSpec((rows, kw), lambda h, i: (i, qb + h)),
                  pl.BlockSpec((rows, kw), lambda h, i: (i, kb + h)),
                  pl.BlockSpec((rows, vw), lambda h, i: (i, vb + h)),
                  pl.BlockSpec((rows, vw), lambda h, i: (i, rb + h)),
                  pl.BlockSpec((rows, SMALL_W), lambda h, i: (i, 0)),
                  pl.BlockSpec((SMALL_W, kw), lambda h, i: (0, h)),
                  pl.BlockSpec((1, kw), lambda h, i: (0, h)),
                  pl.BlockSpec((1, vw), lambda h, i: (0, h))],
        out_specs=pl.BlockSpec((rows, vw), lambda h, i: (i, h)),
        scratch_shapes=[pltpu.VMEM((rows, kw), F32)] + [pltpu.VMEM((GLA_DV, GLA_DK), F32)] * hp,
        compiler_params=_cparams("parallel", "arbitrary"),
        name="gla",
    )(proj, proj, proj, proj, small, w_gk_pad, b_gk.reshape(1, GLA_QK), norm_g.reshape(1, GLA_V))


def _causal_conv_silu(xp_ref, x_ref, w_ref, b_ref, first):
    rows = x_ref.shape[0]
    hist = V7X_SUBLANES

    @pl.when(first)
    def _():
        xp_ref[pl.ds(0, hist), :] = jnp.zeros((hist, xp_ref.shape[1]), F32)

    xp_ref[pl.ds(hist, rows), :] = x_ref[...].astype(F32)
    acc = b_ref[...]
    for tap in range(SSM_CONV):
        acc = acc + xp_ref[pl.ds(hist - (SSM_CONV - 1) + tap, rows), :] * w_ref[tap:tap + 1, :]
    xp_ref[pl.ds(0, hist), :] = xp_ref[pl.ds(rows, hist), :]
    return _silu(acc)


def _ssd_kernel(z_ref, xs_ref, b_ref, c_ref, sm_ref, e_ref, cwx_ref, cwb_ref, cwc_ref,
                cbx_ref, cbb_ref, cbc_ref, dtb_ref, alog_ref, dsk_ref, ng_ref,
                o_ref, xpx_ref, xpb_ref, xpc_ref, xs_scr, b_scr, c_scr, dt_scr, cum_scr, *s_refs):
    first = pl.program_id(1) == 0

    @pl.when(first)
    def _():
        for s_ref in s_refs:
            s_ref[...] = jnp.zeros_like(s_ref)

    rows = xs_ref.shape[0]
    xs_scr[...] = _causal_conv_silu(xpx_ref, xs_ref, cwx_ref, cbx_ref, first)
    b_scr[...] = _causal_conv_silu(xpb_ref, b_ref, cwb_ref, cbb_ref, first).astype(BF16)
    c_scr[...] = _causal_conv_silu(xpc_ref, c_ref, cwc_ref, cbc_ref, first).astype(BF16)

    dt_in = sm_ref[...] + dtb_ref[...]
    dt_c = jnp.maximum(dt_in, 0.0) + jnp.log1p(jnp.exp(-jnp.abs(dt_in)))
    da_c = dt_c * (-jnp.exp(alog_ref[...]))
    tri = _chunk_tril(CUM_BLOCK)
    cum_c = jnp.concatenate([_sel_dot(tri, da_c[b * CUM_BLOCK:(b + 1) * CUM_BLOCK])
                             for b in range(rows // CUM_BLOCK)], axis=0)
    dt_scr[...] = _sel_dot_rhs(dt_c, e_ref[0])
    cum_scr[...] = _sel_dot_rhs(cum_c, e_ref[0])

    gw = SSM_GW
    rw = lax.broadcasted_iota(jnp.int32, (CHUNK, gw), 0)
    cw = lax.broadcasted_iota(jnp.int32, (CHUNK, gw), 1)
    diag_tile = (jnp.bitwise_and(cw, SSM_HEADDIM - 1) == rw).astype(F32)
    half = gw // 2
    rb = lax.broadcasted_iota(jnp.int32, (half, half), 0)
    cb_ = lax.broadcasted_iota(jnp.int32, (half, half), 1)
    head_mask = jnp.right_shift(rb, HEAD_SHIFT) == jnp.right_shift(cb_, HEAD_SHIFT)
    reps = half // CHUNK
    for c in range(rows // CHUNK):
        sl = pl.ds(c * CHUNK, CHUNK)
        for gg, s_ref in enumerate(s_refs):
            xl = slice(gg * gw, (gg + 1) * gw)
            nl = slice(gg * SSM_STATE, (gg + 1) * SSM_STATE)
            cum = cum_scr[sl, xl]
            cum_row = jnp.sum(cum * diag_tile, axis=0, keepdims=True)
            seg = jnp.exp(-jnp.abs(cum - cum_row))
            xs = xs_scr[sl, xl]
            bm = b_scr[sl, nl]
            cm = c_scr[sl, nl]
            xdt = xs * dt_scr[sl, xl]
            cbt = _dot_nt(cm, jnp.concatenate([bm] * SSM_HPG, axis=0))
            m_all = (cbt * seg).astype(BF16)
            intra = []
            for hlf in range(2):
                xh = xdt[:, hlf * half:(hlf + 1) * half]
                bd = jnp.where(head_mask, jnp.concatenate([xh] * reps, axis=0), 0.0).astype(BF16)
                intra.append(_dot(m_all[:, hlf * half:(hlf + 1) * half], bd))
            intra = jnp.concatenate(intra, axis=1)
            cum_end = cum[CHUNK - 1:CHUNK, :]
            state = s_ref[...]
            inter = _dot(cm, state.astype(BF16)) * jnp.exp(cum)
            contrib = _dot_tn(bm, (xdt * jnp.exp(cum_end - cum)).astype(BF16))
            s_ref[...] = state * jnp.exp(cum_end) + contrib
            y = intra + inter + xs * dsk_ref[:, xl]
            y = y * _silu(z_ref[sl, xl].astype(F32))
            y = y * lax.rsqrt(jnp.mean(y * y, axis=-1, keepdims=True) + EPS) * ng_ref[:, xl]
            o_ref[sl, xl] = y.astype(o_ref.dtype)


def _ssd(proj, small, expand, conv_w, conv_b, dt_bias, a_log, d_skip, norm_g):
    s = proj.shape[0]
    rows = min(MIX_ROWS, s)
    gp = SSD_GROUPS_PER_STEP
    xw, nw = gp * SSM_GW, gp * SSM_STATE
    zb, xb = P_SSM_Z // xw, P_SSM_X // xw
    bb, cb = P_SSM_B // nw, P_SSM_C // nw
    cwb, cwc = SSM_INNER // nw, (SSM_INNER + SSM_BC) // nw

    def lane_expand(p):
        return jnp.repeat(p.astype(F32), SSM_HEADDIM).reshape(1, SSM_INNER)

    def side_lanes(p):
        return jnp.pad(p.astype(F32), (SMALL_DT0, SMALL_W - SMALL_DT0 - SSM_HEADS)).reshape(1, SMALL_W)

    conv_b2 = conv_b.reshape(1, SSM_CONV_DIM)
    vec = pl.BlockSpec((1, xw), lambda g, i: (0, g))
    side_vec = pl.BlockSpec((1, SMALL_W), lambda g, i: (0, 0))
    return pl.pallas_call(
        _ssd_kernel,
        out_shape=jax.ShapeDtypeStruct((s, SSM_INNER), BF16),
        grid=(SSM_GROUPS // gp, s // rows),
        in_specs=[pl.BlockSpec((rows, xw), lambda g, i: (i, zb + g)),
                  pl.BlockSpec((rows, xw), lambda g, i: (i, xb + g)),
                  pl.BlockSpec((rows, nw), lambda g, i: (i, bb + g)),
                  pl.BlockSpec((rows, nw), lambda g, i: (i, cb + g)),
                  pl.BlockSpec((rows, SMALL_W), lambda g, i: (i, 0)),
                  pl.BlockSpec((1, SMALL_W, xw), lambda g, i: (g, 0, 0)),
                  pl.BlockSpec((SSM_CONV, xw), lambda g, i: (0, g)),
                  pl.BlockSpec((SSM_CONV, nw), lambda g, i: (0, cwb + g)),
                  pl.BlockSpec((SSM_CONV, nw), lambda g, i: (0, cwc + g)),
                  pl.BlockSpec((1, xw), lambda g, i: (0, g)),
                  pl.BlockSpec((1, nw), lambda g, i: (0, cwb + g)),
                  pl.BlockSpec((1, nw), lambda g, i: (0, cwc + g)),
                  side_vec, side_vec, vec, vec],
        out_specs=pl.BlockSpec((rows, xw), lambda g, i: (i, g)),
        scratch_shapes=[pltpu.VMEM((rows + V7X_SUBLANES, xw), F32),
                        pltpu.VMEM((rows + V7X_SUBLANES, nw), F32),
                        pltpu.VMEM((rows + V7X_SUBLANES, nw), F32),
                        pltpu.VMEM((rows, xw), F32),
                        pltpu.VMEM((rows, nw), BF16),
                        pltpu.VMEM((rows, nw), BF16),
                        pltpu.VMEM((rows, xw), F32),
                        pltpu.VMEM((rows, xw), F32)] + [pltpu.VMEM((SSM_STATE, SSM_GW), F32)] * gp,
        compiler_params=_cparams("parallel", "arbitrary"),
        name="ssd",
    )(proj, proj, proj, proj, small, expand, conv_w, conv_w, conv_w, conv_b2, conv_b2, conv_b2,
      side_lanes(dt_bias), side_lanes(a_log), lane_expand(d_skip), norm_g.reshape(1, SSM_INNER))


def _ssd_expand_table():
    xw = SSD_GROUPS_PER_STEP * SSM_GW
    rows = jnp.arange(SMALL_W, dtype=jnp.int32)[None, :, None]
    cols = jnp.arange(xw, dtype=jnp.int32)[None, None, :]
    step = jnp.arange(SSM_INNER // xw, dtype=jnp.int32)[:, None, None]
    return (rows == SMALL_DT0 + (step * xw + cols) // SSM_HEADDIM).astype(BF16)


def _rope_table_kernel(pos_ref, freq_ref, sign_ref, cos_ref, sin_ref):
    ang = pos_ref[...] * freq_ref[...]
    cos_ref[...] = jnp.cos(ang)
    sin_ref[...] = jnp.sin(ang) * sign_ref[...]


def _rope_tables(positions):
    s = positions.shape[-1]
    half = RET_DK // 2
    inv_freq = ROPE_BASE ** (-jnp.arange(half, dtype=F32) / half)
    freq = jnp.concatenate([inv_freq, inv_freq]).reshape(1, RET_DK)
    sign = jnp.concatenate([-jnp.ones((half,), F32), jnp.ones((half,), F32)]).reshape(1, RET_DK)
    pos = positions.astype(F32).reshape(s, 1)
    rows = min(1024, s)
    tab = jax.ShapeDtypeStruct((s, RET_DK), F32)
    return pl.pallas_call(
        _rope_table_kernel,
        out_shape=[tab, tab],
        grid=(s // rows,),
        in_specs=[pl.BlockSpec((rows, 1), lambda i: (i, 0)),
                  pl.BlockSpec((1, RET_DK), lambda i: (0, 0)),
                  pl.BlockSpec((1, RET_DK), lambda i: (0, 0))],
        out_specs=[pl.BlockSpec((rows, RET_DK), lambda i: (i, 0))] * 2,
        compiler_params=_cparams("parallel"),
        name="rope_tables",
    )(pos, freq, sign)


def _ret_kernel(q_ref, k_ref, v_ref, g_ref, cos_ref, sin_ref, dmat_ref, qdec_ref, kdec_ref, sdec_ref,
                ng_ref, o_ref, *s_refs):
    @pl.when(pl.program_id(1) == 0)
    def _():
        for s_ref in s_refs:
            s_ref[...] = jnp.zeros_like(s_ref)

    cos, sin = cos_ref[...], sin_ref[...]

    def rot(t):
        t = t.astype(F32)
        return t * cos + pltpu.roll(t, RET_DK // 2, axis=1) * sin

    for hh, s_ref in enumerate(s_refs):
        kl = slice(hh * RET_DK, (hh + 1) * RET_DK)
        vl = slice(hh * RET_DV, (hh + 1) * RET_DV)
        q = rot(q_ref[:, kl])
        k = rot(k_ref[:, kl]) * (RET_DK ** -0.5)
        v = v_ref[:, vl]
        scores = (_dot_nt(q.astype(BF16), k.astype(BF16)) * dmat_ref[hh]).astype(BF16)
        state = s_ref[...]
        o = _dot(scores, v) + _dot((q * qdec_ref[hh]).astype(BF16), state.astype(BF16))
        s_ref[...] = state * sdec_ref[hh] + _dot_tn((k * kdec_ref[hh]).astype(BF16), v)
        o = o * lax.rsqrt(jnp.mean(o * o, axis=-1, keepdims=True) + EPS) * ng_ref[:, vl]
        o_ref[:, vl] = (o * _silu(g_ref[:, vl].astype(F32))).astype(o_ref.dtype)


def _ret_tables(rows):
    log_gamma = jnp.log1p(-jnp.exp2(-5.0 - jnp.arange(RET_HEADS, dtype=F32)))[:, None, None]
    i = jnp.arange(rows, dtype=jnp.int32)
    dist = (i[:, None] - i[None, :]).astype(F32)[None]
    ch_i, ch_j = (i // CHUNK)[:, None], (i // CHUNK)[None, :]
    dmat = jnp.where((ch_i == ch_j)[None], jnp.exp(log_gamma * jnp.abs(dist)),
                     jnp.where((ch_j < ch_i)[None], jnp.exp(log_gamma * dist), 0.0))
    fi = i.astype(F32)[None, :, None]
    qdec = jnp.broadcast_to(jnp.exp(log_gamma * (fi + 1.0)), (RET_HEADS, rows, RET_DK))
    kdec = jnp.broadcast_to(jnp.exp(log_gamma * (rows - 1.0 - fi)), (RET_HEADS, rows, RET_DK))
    sdec = jnp.broadcast_to(jnp.exp(log_gamma * float(rows)), (RET_HEADS, 1, RET_DV))
    return dmat, qdec, kdec, sdec


def _retention(proj, cos_t, sin_t, tables, norm_g):
    s = proj.shape[0]
    rows = min(RET_BLOCK, s)
    dmat, qdec, kdec, sdec = tables
    hp = RET_HEADS_PER_STEP
    kw, vw = hp * RET_DK, hp * RET_DV
    qb, kb = P_RET_Q // kw, P_RET_K // kw
    vb, gb = P_RET_V // vw, P_RET_G // vw
    return pl.pallas_call(
        _ret_kernel,
        out_shape=jax.ShapeDtypeStruct((s, RET_V), BF16),
        grid=(RET_HEADS // hp, s // rows),
        in_specs=[pl.BlockSpec((rows, kw), lambda h, i: (i, qb + h)),
                  pl.BlockSpec((rows, kw), lambda h, i: (i, kb + h)),
                  pl.BlockSpec((rows, vw), lambda h, i: (i, vb + h)),
                  pl.BlockSpec((rows, vw), lambda h, i: (i, gb + h)),
                  pl.BlockSpec((rows, RET_DK), lambda h, i: (i, 0)),
                  pl.BlockSpec((rows, RET_DK), lambda h, i: (i, 0)),
                  pl.BlockSpec((hp, rows, rows), lambda h, i: (h, 0, 0)),
                  pl.BlockSpec((hp, rows, RET_DK), lambda h, i: (h, 0, 0)),
                  pl.BlockSpec((hp, rows, RET_DK), lambda h, i: (h, 0, 0)),
                  pl.BlockSpec((hp, 1, RET_DV), lambda h, i: (h, 0, 0)),
                  pl.BlockSpec((1, vw), lambda h, i: (0, h))],
        out_specs=pl.BlockSpec((rows, vw), lambda h, i: (i, h)),
        scratch_shapes=[pltpu.VMEM((RET_DK, RET_DV), F32)] * hp,
        compiler_params=_cparams("parallel", "arbitrary"),
        name="retention",
    )(proj, proj, proj, proj, cos_t, sin_t, dmat, qdec, kdec, sdec, norm_g.reshape(1, RET_V))


def _gated_merge_proj(gl_ref, y0_ref, y1_ref, y2_ref, gu0_ref, gu1_ref, gu2_ref, gb0_ref, gb1_ref, gb2_ref,
                      wo0_ref, wo1_ref, wo2_ref):
    gl = gl_ref[...]
    acc = None
    for y_ref, gu_ref, gb_ref, wo_ref in ((y0_ref, gu0_ref, gb0_ref, wo0_ref),
                                          (y1_ref, gu1_ref, gb1_ref, wo1_ref),
                                          (y2_ref, gu2_ref, gb2_ref, wo2_ref)):
        gate = jax.nn.sigmoid(_dot(gl, gu_ref[...]) + gb_ref[...])
        term = gate * _dot(y_ref[...], wo_ref[...])
        acc = term if acc is None else acc + term
    return acc


def _merge_post(proj, ys, gate_w2, gate_b, w_out, layer, w_out_layer, x, g_post, g_next, bm=1024, bn=256):
    s, d = x.shape
    nm = s // bm
    gate_b4 = gate_b.reshape(DEPTH, N_BRANCH, 1, d)
    row = lambda i: jnp.minimum(i, nm - 1)
    y_spec = pl.BlockSpec((bm, BRANCH_W), lambda i, j: (row(i), 0))
    specs = [pl.BlockSpec((bm, GATE_RANK), lambda i, j: (row(i), P_GATE // GATE_RANK)), y_spec, y_spec, y_spec]
    specs += [pl.BlockSpec((None, GATE_RANK, bn),
                           functools.partial(lambda n, i, j: (layer * N_BRANCH + n, 0, j), n))
              for n in range(N_BRANCH)]
    specs += [pl.BlockSpec((None, None, 1, bn), functools.partial(lambda n, i, j: (layer, n, 0, j), n))
              for n in range(N_BRANCH)]
    specs += [pl.BlockSpec((None, BRANCH_W, bn),
                           functools.partial(lambda n, i, j: (w_out_layer * N_BRANCH + n, 0, j), n))
              for n in range(N_BRANCH)]
    args = (proj, ys[0], ys[1], ys[2], gate_w2, gate_w2, gate_w2, gate_b4, gate_b4, gate_b4, w_out, w_out, w_out)
    return _fused_proj_post(_gated_merge_proj, args, specs, x, g_post, 1.0, g_next, "merge_post", bm, bn)


def _xattn_kernel(q_ref, kv_ref, o_ref):
    for h in range(XA_HEADS):
        lo, hi = h * XA_DIM, (h + 1) * XA_DIM
        s = _dot_nt(q_ref[:, lo:hi], kv_ref[:, lo:hi]) * (XA_DIM ** -0.5)
        p = jnp.exp(s - jnp.max(s, axis=-1, keepdims=True))
        p = p / jnp.sum(p, axis=-1, keepdims=True)
        o_ref[:, lo:hi] = _dot(p.astype(BF16), kv_ref[:, XA_W + lo:XA_W + hi]).astype(o_ref.dtype)


def _xattn(q, kv, rows=512):
    s = q.shape[0]
    rows = min(rows, s)
    n_mem = kv.shape[0]
    return pl.pallas_call(
        _xattn_kernel,
        out_shape=jax.ShapeDtypeStruct((s, XA_W), BF16),
        grid=(s // rows,),
        in_specs=[pl.BlockSpec((rows, XA_W), lambda i: (i, 0)),
                  pl.BlockSpec((n_mem, 2 * XA_W), lambda i: (0, 0))],
        out_specs=pl.BlockSpec((rows, XA_W), lambda i: (i, 0)),
        compiler_params=_cparams("parallel"),
        name="xattn",
    )(q, kv)


def kernel(x, mem, positions, norm_g, mem_norm_g, ffn1_w_in, ffn1_w_out, mix_w_in, gla_w_gk, gla_b_gk,
           gla_norm_g, ssm_conv_w, ssm_conv_b, ssm_dt_bias, ssm_a_log, ssm_d, ssm_norm_g, ret_norm_g,
           gate_w_up, gate_b, mix_w_out, xattn_w_q, xattn_w_kv, xattn_w_o, ffn2_w_in, ffn2_w_out):
    b, s, d = x.shape
    assert b == 1 and d == D_MODEL and s % max(RET_BLOCK, MIX_ROWS, 1024) == 0
    nl = DEPTH
    xr = x.reshape(s, d)
    memr = mem.reshape(mem.shape[1], d)
    cos_t, sin_t = _rope_tables(positions)
    ret_tabs = _ret_tables(min(RET_BLOCK, s))
    expand = _ssd_expand_table()

    xa_out = _cast_bf16(xattn_w_o.reshape(nl * XA_W, d), 512).reshape(nl, XA_W, d)
    mix_w_out3 = mix_w_out.reshape(nl, N_BRANCH * BRANCH_W, d)
    gate_up = _cast_bf16(jnp.transpose(gate_w_up, (0, 2, 1, 3)).reshape(nl * N_BRANCH * GATE_RANK, d),
                         512).reshape(nl * N_BRANCH, GATE_RANK, d)
    mix_w_in_t = jnp.transpose(mix_w_in, (0, 2, 1))
    mix_in_big = _repack_mix_w_in(mix_w_in_t)
    mix_in_side_t = _side_mix_w_in(mix_w_in_t)
    w_gk_pad = jnp.concatenate([gla_w_gk, jnp.zeros((nl, SMALL_W - GLA_RANK, GLA_QK), F32)], axis=1)

    h = (_norm_cast(xr, norm_g[0, 0]), None)
    for l in range(nl):
        ng = norm_g[l]
        a, w_out, mix_out = _swiglu_in(h, ffn1_w_in, l, (ffn1_w_out, mix_w_out3))
        mix_out = mix_out.reshape(N_BRANCH, BRANCH_W, d)
        xr, h = _out_proj_post(a, w_out, 0, xr, ng[1], 0.5, ng[2], "ffn_out_post")
        proj, small = _mix_in(h, mix_in_big, mix_in_side_t, l)
        y_gla = _gla(proj, small, w_gk_pad[l], gla_b_gk[l], gla_norm_g[l])
        y_ssd = _ssd(proj, small, expand, ssm_conv_w[l], ssm_conv_b[l], ssm_dt_bias[l], ssm_a_log[l],
                     ssm_d[l], ssm_norm_g[l])
        y_ret = _retention(proj, cos_t, sin_t, ret_tabs, ret_norm_g[l])
        xr, h = _merge_post(proj, (y_gla, y_ssd, y_ret), gate_up, gate_b, mix_out, l, 0, xr, ng[3], ng[4])
        q = _matmul_wcast(h, xattn_w_q, l, BF16, 1024, 512, "xattn_q")
        mem_n = _norm_cast(memr, mem_norm_g[l])
        kv = _matmul_wcast((mem_n, None), xattn_w_kv, l, BF16, 256, 512, "xattn_kv")
        o = _xattn(q, kv)
        xr, h = _out_proj_post(o, xa_out, l, xr, ng[5], 1.0, ng[6], "xattn_out_post", bn=1024)
        a, w_out = _swiglu_in(h, ffn2_w_in, l, (ffn2_w_out,))
        g_next = norm_g[l + 1, 0] if l + 1 < nl else None
        xr, h = _out_proj_post(a, w_out, 0, xr, ng[7], 0.5, g_next, "ffn_out_post")
    return xr.reshape(b, s, d)
```

```python
import functools

import numpy as np
import jax
import jax.numpy as jnp
from jax import lax
from jax.experimental import pallas as pl
from jax.experimental.pallas import tpu as pltpu

F32 = jnp.float32
BF16 = jnp.bfloat16

D_MODEL = 4096
DEPTH = 4
CHUNK = 64
D_FF = 4096
EPS = 1e-6
GLA_HEADS, GLA_DK, GLA_DV, GLA_RANK, GLA_TAU = 4, 256, 512, 16, 16.0
GLA_QK, GLA_V = GLA_HEADS * GLA_DK, GLA_HEADS * GLA_DV
SSM_HEADS, SSM_HEADDIM, SSM_GROUPS, SSM_STATE, SSM_CONV = 32, 64, 4, 128, 4
SSM_INNER = SSM_HEADS * SSM_HEADDIM
SSM_BC = SSM_GROUPS * SSM_STATE
SSM_CONV_DIM = SSM_INNER + 2 * SSM_BC
SSM_HPG = SSM_HEADS // SSM_GROUPS
SSM_GW = SSM_INNER // SSM_GROUPS
RET_HEADS, RET_DK, RET_DV = 8, 128, 256
RET_QK, RET_V = RET_HEADS * RET_DK, RET_HEADS * RET_DV
ROPE_BASE = 10000.0
XA_HEADS, XA_DIM = 4, 256
XA_W = XA_HEADS * XA_DIM
N_BRANCH, BRANCH_W, GATE_RANK = 3, 2048, 512

_IN_WIDTHS = (GLA_QK, GLA_QK, GLA_V, GLA_V, GLA_RANK, SSM_INNER, SSM_CONV_DIM, SSM_HEADS,
              RET_QK, RET_QK, RET_V, RET_V, GATE_RANK)
_IN_OFF = np.concatenate([[0], np.cumsum(_IN_WIDTHS)]).tolist()
IN_COLS = _IN_OFF[-1]
P_GLA_Q, P_GLA_K, P_GLA_V, P_GLA_R = 0, 1024, 2048, 4096
P_SSM_Z, P_SSM_X, P_SSM_B, P_SSM_C = 6144, 8192, 10240, 10752
P_RET_Q, P_RET_K, P_RET_V, P_RET_G = 11264, 12288, 13312, 15360
P_GATE = 17408
P_COLS = 17920
SMALL_W = 128
SMALL_DT0 = GLA_RANK

V7X_VMEM_LIMIT = 60 * 1024 * 1024
V7X_SUBLANES = 8
CHUNK_SHIFT = CHUNK.bit_length() - 1
HEAD_SHIFT = SSM_HEADDIM.bit_length() - 1

RET_BLOCK = 512
MIX_ROWS = 512
CUM_BLOCK = 256
GLA_HEADS_PER_STEP = 4
SSD_GROUPS_PER_STEP = 2
RET_HEADS_PER_STEP = 4


def _cparams(*sem):
    return pltpu.CompilerParams(dimension_semantics=sem, vmem_limit_bytes=V7X_VMEM_LIMIT)


def _silu(x):
    return x * jax.nn.sigmoid(x)


def _dot(a, b):
    return jnp.dot(a, b, preferred_element_type=F32)


def _dot_nt(a, b):
    return lax.dot_general(a, b, (((1,), (1,)), ((), ())), preferred_element_type=F32)


def _dot_tn(a, b):
    return lax.dot_general(a, b, (((0,), (0,)), ((), ())), preferred_element_type=F32)


def _split3(x):
    hi = x.astype(BF16)
    r1 = x - hi.astype(F32)
    mid = r1.astype(BF16)
    lo = (r1 - mid.astype(F32)).astype(BF16)
    return hi, mid, lo


def _sel_dot(sel, x):
    hi, mid, lo = _split3(x)
    return _dot(sel, hi) + _dot(sel, mid) + _dot(sel, lo)


def _sel_dot_rhs(x, sel):
    hi, mid, lo = _split3(x)
    return _dot(hi, sel) + _dot(mid, sel) + _dot(lo, sel)


def _chunk_tril(n):
    ri = lax.broadcasted_iota(jnp.int32, (n, n), 0)
    ci = lax.broadcasted_iota(jnp.int32, (n, n), 1)
    same = jnp.right_shift(ri, CHUNK_SHIFT) == jnp.right_shift(ci, CHUNK_SHIFT)
    return jnp.where(same & (ri >= ci), 1.0, 0.0).astype(BF16)


def _cast_kernel(w_ref, o_ref):
    o_ref[...] = w_ref[...].astype(o_ref.dtype)


def _cast_bf16(w, rows):
    r, c = w.shape
    return pl.pallas_call(
        _cast_kernel,
        out_shape=jax.ShapeDtypeStruct((r, c), BF16),
        grid=(r // rows,),
        in_specs=[pl.BlockSpec((rows, c), lambda i: (i, 0))],
        out_specs=pl.BlockSpec((rows, c), lambda i: (i, 0)),
        compiler_params=_cparams("parallel"),
        name="cast_bf16",
    )(w)


REPACK_COLS = 512


def _packed_to_source_col(c):
    skipped = jnp.where(c >= P_SSM_Z, GLA_RANK, 0) + jnp.where(c >= P_RET_Q, SSM_HEADS, 0)
    return pl.multiple_of(c + skipped, GLA_RANK)


def _repack_kernel(w_ref, o_ref):
    o_ref[...] = w_ref[...].T.astype(o_ref.dtype)


def _repack_mix_w_in(w_t):
    nl, _, d = w_t.shape
    return pl.pallas_call(
        _repack_kernel,
        out_shape=jax.ShapeDtypeStruct((nl, d, P_COLS), BF16),
        grid=(nl, P_COLS // REPACK_COLS),
        in_specs=[pl.BlockSpec((pl.Squeezed(), pl.Element(REPACK_COLS), pl.Element(d)),
                               lambda l, j: (l, _packed_to_source_col(j * REPACK_COLS), 0))],
        out_specs=pl.BlockSpec((None, d, REPACK_COLS), lambda l, j: (l, 0, j)),
        compiler_params=_cparams("parallel", "parallel"),
        name="repack_mix_w_in",
    )(w_t)


def _side_w_kernel(gk_ref, dt_ref, o_ref):
    o_ref[...] = jnp.zeros(o_ref.shape, o_ref.dtype)
    o_ref[0:GLA_RANK, :] = gk_ref[...].astype(o_ref.dtype)
    o_ref[GLA_RANK:GLA_RANK + SSM_HEADS, :] = dt_ref[...].astype(o_ref.dtype)


def _side_mix_w_in(w_t):
    nl, _, d = w_t.shape
    o = _IN_OFF
    return pl.pallas_call(
        _side_w_kernel,
        out_shape=jax.ShapeDtypeStruct((nl, SMALL_W, d), BF16),
        grid=(nl,),
        in_specs=[pl.BlockSpec((pl.Squeezed(), pl.Element(GLA_RANK), pl.Element(d)), lambda l: (l, o[4], 0)),
                  pl.BlockSpec((pl.Squeezed(), pl.Element(SSM_HEADS), pl.Element(d)), lambda l: (l, o[7], 0))],
        out_specs=pl.BlockSpec((None, SMALL_W, d), lambda l: (l, 0, 0)),
        compiler_params=_cparams("parallel"),
        name="side_mix_w_in",
    )(w_t, w_t)


def _norm_cast_kernel(x_ref, g_ref, o_ref):
    x = x_ref[...]
    r = lax.rsqrt(jnp.mean(x * x, axis=-1, keepdims=True) + EPS)
    o_ref[...] = (x * r * g_ref[...]).astype(o_ref.dtype)


def _norm_cast(x, g, rows=256):
    m, d = x.shape
    rows = min(rows, m)
    return pl.pallas_call(
        _norm_cast_kernel,
        out_shape=jax.ShapeDtypeStruct((m, d), BF16),
        grid=(m // rows,),
        in_specs=[pl.BlockSpec((rows, d), lambda i: (i, 0)), pl.BlockSpec((1, d), lambda i: (0, 0))],
        out_specs=pl.BlockSpec((rows, d), lambda i: (i, 0)),
        compiler_params=_cparams("parallel"),
        name="norm_cast",
    )(x, g.reshape(1, d))


ROW_SCALE_W = 128


def _mix_in_kernel(x_ref, rs_ref, w_ref, wside_ref, o_ref, side_ref):
    x = x_ref[...]
    rs = rs_ref[:, 0:1]
    o_ref[...] = (_dot(x, w_ref[...]) * rs).astype(o_ref.dtype)

    @pl.when(pl.program_id(1) == 0)
    def _():
        side_ref[...] = _dot_nt(x, wside_ref[...]) * rs


def _mix_in(act, w, w_side_t, layer, bm=1024, bn=1280):
    x, rs = act
    m, k = x.shape
    n = w.shape[2]
    return pl.pallas_call(
        _mix_in_kernel,
        out_shape=[jax.ShapeDtypeStruct((m, n), BF16), jax.ShapeDtypeStruct((m, SMALL_W), F32)],
        grid=(m // bm, n // bn),
        in_specs=[pl.BlockSpec((bm, k), lambda i, j: (i, 0)),
                  pl.BlockSpec((bm, ROW_SCALE_W), lambda i, j: (i, 0)),
                  pl.BlockSpec((None, k, bn), lambda i, j: (layer, 0, j)),
                  pl.BlockSpec((None, SMALL_W, k), lambda i, j: (layer, 0, 0))],
        out_specs=[pl.BlockSpec((bm, bn), lambda i, j: (i, j)),
                   pl.BlockSpec((bm, SMALL_W), lambda i, j: (i, 0))],
        compiler_params=_cparams("parallel", "arbitrary"),
        name="mix_in",
    )(x, rs, w, w_side_t)


def _mm_wcast_kernel(x_ref, w_ref, o_ref):
    o_ref[...] = _dot(x_ref[...], w_ref[...].astype(BF16)).astype(o_ref.dtype)


def _matmul_wcast(x, w, layer, out_dtype, bn, name):
    m, k = x.shape
    n = w.shape[2]
    return pl.pallas_call(
        _mm_wcast_kernel,
        out_shape=jax.ShapeDtypeStruct((m, n), out_dtype),
        grid=(n // bn,),
        in_specs=[pl.BlockSpec((m, k), lambda j: (0, 0)),
                  pl.BlockSpec((None, k, bn), lambda j: (layer, 0, j))],
        out_specs=pl.BlockSpec((m, bn), lambda j: (0, j)),
        compiler_params=_cparams("parallel"),
        name=name,
    )(x, w)


def _swiglu_kernel(bn, scaled, n_side, x_ref, *refs):
    first_w = 1 if scaled else 0
    wg_ref, wu_ref = refs[first_w:first_w + 2]
    side_refs = refs[first_w + 2:first_w + 2 + n_side]
    o_ref = refs[first_w + 2 + n_side]
    side_o_refs = refs[first_w + 3 + n_side:first_w + 3 + 2 * n_side]
    w_scr = refs[-1]

    @pl.when(pl.program_id(1) == 0)
    def _():
        w_scr[:, 0:bn] = wg_ref[...].astype(BF16)
        w_scr[:, bn:2 * bn] = wu_ref[...].astype(BF16)

    r = _dot(x_ref[...], w_scr[...])
    if scaled:
        r = r * refs[0][:, 0:1]
    o_ref[...] = (_silu(r[:, 0:bn]) * r[:, bn:2 * bn]).astype(o_ref.dtype)
    for s_ref, so_ref in zip(side_refs, side_o_refs):
        so_ref[...] = s_ref[...].astype(so_ref.dtype)


def _swiglu_in(act, w, layer, side_ws, bm=512, bn=512):
    x, rs = act
    m, k = x.shape
    f = w.shape[2] // 2
    nb, nmb = f // bn, m // bm
    steps = nb * nmb
    scaled = rs is not None
    in_specs = [pl.BlockSpec((bm, k), lambda j, i: (i, 0))]
    if scaled:
        in_specs.append(pl.BlockSpec((bm, ROW_SCALE_W), lambda j, i: (i, 0)))
    in_specs += [pl.BlockSpec((None, k, bn), lambda j, i: (layer, 0, j)),
                 pl.BlockSpec((None, k, bn), lambda j, i: (layer, 0, j + nb))]
    out_shape = [jax.ShapeDtypeStruct((m, f), BF16)]
    out_specs = [pl.BlockSpec((bm, bn), lambda j, i: (i, j))]
    for sw in side_ws:
        _, r_side, c_side = sw.shape
        slab = r_side // steps
        in_specs.append(pl.BlockSpec((None, slab, c_side), lambda j, i: (layer, j * nmb + i, 0)))
        out_shape.append(jax.ShapeDtypeStruct((1, r_side, c_side), BF16))
        out_specs.append(pl.BlockSpec((None, slab, c_side), lambda j, i: (0, j * nmb + i, 0)))
    args = ((x, rs) if scaled else (x,)) + (w, w) + tuple(side_ws)
    return pl.pallas_call(
        functools.partial(_swiglu_kernel, bn, scaled, len(side_ws)),
        out_shape=out_shape,
        grid=(nb, nmb),
        in_specs=in_specs,
        out_specs=out_specs,
        scratch_shapes=[pltpu.VMEM((k, 2 * bn), BF16)],
        compiler_params=_cparams("arbitrary", "arbitrary"),
        name="swiglu_in",
    )(*args)


def _plain_out_proj(lhs_ref, w_ref):
    return _dot(lhs_ref[...], w_ref[...])


def _out_proj_kernel(nm, scale, emit_h, proj_fn, n_proj, *refs):
    proj_refs = refs[:n_proj]
    x_ref, gp_ref, gn_ref, xo_ref = refs[n_proj:n_proj + 4]
    rest = refs[n_proj + 4:]
    if emit_h:
        ho_ref, rs_ref, y_scr, ssq_scr, ssq2_scr = rest
    else:
        y_scr, ssq_scr = rest
    i, n = pl.program_id(0), pl.program_id(1)
    cur = lax.rem(i, 2)
    prev = 1 - cur
    inv_d = 1.0 / (y_scr.shape[0] * y_scr.shape[2])

    def matmul_part():
        yb = proj_fn(*proj_refs)
        y_scr[n] = yb.astype(BF16)
        part = jnp.sum(yb * yb, axis=-1, keepdims=True)
        ssq_scr[cur] = jnp.where(n == 0, part, ssq_scr[cur] + part)

    def residual_part():
        r = lax.rsqrt(ssq_scr[prev] * inv_d + EPS)
        xn = x_ref[...] + scale * (y_scr[n].astype(F32) * r * gp_ref[...])
        xo_ref[...] = xn
        if emit_h:
            ho_ref[...] = (xn * gn_ref[...]).astype(BF16)
            part = jnp.sum(xn * xn, axis=-1, keepdims=True)
            acc = jnp.where(n == 0, part, ssq2_scr[...] + part)
            ssq2_scr[...] = acc
            rs_ref[...] = jnp.broadcast_to(lax.rsqrt(acc * inv_d + EPS), rs_ref.shape)

    @pl.when(i == 0)
    def _():
        matmul_part()

    @pl.when((i > 0) & (i < nm))
    def _():
        residual_part()
        matmul_part()

    @pl.when(i == nm)
    def _():
        residual_part()


def _out_proj_post(lhs, w, layer, x, g_post, scale, g_next, name, bm=1024, bn=512):
    m, k = lhs.shape
    nm = m // bm
    proj_specs = [pl.BlockSpec((bm, k), lambda i, n: (jnp.minimum(i, nm - 1), 0)),
                  pl.BlockSpec((None, k, bn), lambda i, n: (layer, 0, n))]
    return _fused_proj_post(_plain_out_proj, (lhs, w), proj_specs, x, g_post, scale, g_next, name, bm, bn)


def _fused_proj_post(proj_fn, proj_args, proj_specs, x, g_post, scale, g_next, name, bm, bn):
    m, d = x.shape
    nm, nn = m // bm, d // bn
    emit_h = g_next is not None
    gn = g_next if emit_h else g_post
    res_map = lambda i, n: (jnp.maximum(i - 1, 0), jnp.where(i == 0, 0, n))
    vec_map = lambda i, n: (0, jnp.where(i == 0, 0, n))
    res_spec = pl.BlockSpec((bm, bn), res_map)
    out_shape = [jax.ShapeDtypeStruct((m, d), F32)]
    out_specs = [res_spec]
    scratch = [pltpu.VMEM((nn, bm, bn), BF16), pltpu.VMEM((2, bm, 1), F32)]
    if emit_h:
        out_shape += [jax.ShapeDtypeStruct((m, d), BF16), jax.ShapeDtypeStruct((m, ROW_SCALE_W), F32)]
        out_specs += [res_spec, pl.BlockSpec((bm, ROW_SCALE_W), lambda i, n: (jnp.maximum(i - 1, 0), 0))]
        scratch.append(pltpu.VMEM((bm, 1), F32))
    res = pl.pallas_call(
        functools.partial(_out_proj_kernel, nm, scale, emit_h, proj_fn, len(proj_args)),
        out_shape=out_shape,
        grid=(nm + 1, nn),
        in_specs=list(proj_specs) + [res_spec, pl.BlockSpec((1, bn), vec_map), pl.BlockSpec((1, bn), vec_map)],
        out_specs=out_specs,
        scratch_shapes=scratch,
        compiler_params=_cparams("arbitrary", "arbitrary"),
        name=name,
    )(*proj_args, x, g_post.reshape(1, d), gn.reshape(1, d))
    return (res[0], (res[1], res[2])) if emit_h else (res[0], None)


def _gla_kernel(q_ref, k_ref, v_ref, r_ref, sm_ref, wgk_ref, bgk_ref, ng_ref, o_ref, cum_ref, *st_refs):
    @pl.when(pl.program_id(1) == 0)
    def _():
        for st_ref in st_refs:
            st_ref[...] = jnp.zeros_like(st_ref)

    rows = q_ref.shape[0]
    a_hi, a_mid, _ = _split3(sm_ref[...])
    w_hi, w_mid, _ = _split3(wgk_ref[...])
    z = _dot(a_hi, w_hi) + _dot(a_hi, w_mid) + _dot(a_mid, w_hi) + bgk_ref[...]
    logf = (jnp.minimum(z, 0.0) - jnp.log1p(jnp.exp(-jnp.abs(z)))) * (1.0 / GLA_TAU)
    tri = _chunk_tril(CUM_BLOCK)
    for b in range(rows // CUM_BLOCK):
        sl = slice(b * CUM_BLOCK, (b + 1) * CUM_BLOCK)
        cum_ref[sl, :] = _sel_dot(tri, logf[sl])

    ri = lax.broadcasted_iota(jnp.int32, (CHUNK, CHUNK), 0)
    ci = lax.broadcasted_iota(jnp.int32, (CHUNK, CHUNK), 1)
    tril = ri >= ci
    for c in range(rows // CHUNK):
        sl = pl.ds(c * CHUNK, CHUNK)
        for hh, st_ref in enumerate(st_refs):
            kl = slice(hh * GLA_DK, (hh + 1) * GLA_DK)
            vl = slice(hh * GLA_DV, (hh + 1) * GLA_DV)
            cum = cum_ref[sl, kl]
            cum_end = cum[CHUNK - 1:CHUNK, :]
            up, down = jnp.exp(cum), jnp.exp(-cum)
            q = q_ref[sl, kl].astype(F32) * (GLA_DK ** -0.5)
            k = k_ref[sl, kl].astype(F32)
            v = v_ref[sl, vl]
            q_dec = (q * up).astype(BF16)
            a_past = _dot_nt(q_dec, (k * down).astype(BF16))
            a_future = _dot_nt((q * down).astype(BF16), (k * up).astype(BF16))
            attn = jnp.where(tril, a_past, a_future).astype(BF16)
            state_t = st_ref[...]
            o = _dot(attn, v) + _dot_nt(q_dec, state_t.astype(BF16))
            contrib_t = _dot_tn(v, (k * jnp.exp(cum_end - cum)).astype(BF16))
            st_ref[...] = state_t * jnp.exp(cum_end) + contrib_t
            o = o * lax.rsqrt(jnp.mean(o * o, axis=-1, keepdims=True) + EPS) * ng_ref[:, vl]
            o_ref[sl, vl] = (o * _silu(r_ref[sl, vl].astype(F32))).astype(o_ref.dtype)


def _gla(proj, small, w_gk_pad, b_gk, norm_g):
    s = proj.shape[0]
    rows = min(MIX_ROWS, s)
    hp = GLA_HEADS_PER_STEP
    kw, vw = hp * GLA_DK, hp * GLA_DV
    qb, vb = P_GLA_Q // kw, P_GLA_V // vw
    kb, rb = P_GLA_K // kw, P_GLA_R // vw
    return pl.pallas_call(
        _gla_kernel,
        out_shape=jax.ShapeDtypeStruct((s, GLA_V), BF16),
        grid=(GLA_HEADS // hp, s // rows),
        in_specs=[pl.BlockSpec((rows, kw), lambda h, i: (i, qb + h)),
                  pl.BlockSpec((rows, kw), lambda h, i: (i, kb + h)),
                  pl.BlockSpec((rows, vw), lambda h, i: (i, vb + h)),
                  pl.BlockSpec((rows, vw), lambda h, i: (i, rb + h)),
                  pl.BlockSpec((rows, SMALL_W), lambda h, i: (i, 0)),
                  pl.BlockSpec((SMALL_W, kw), lambda h, i: (0, h)),
                  pl.BlockSpec((1, kw), lambda h, i: (0, h)),
                  pl.BlockSpec((1, vw), lambda h, i: (0, h))],
        out_specs=pl.BlockSpec((rows, vw), lambda h, i: (i, h)),
        scratch_shapes=[pltpu.VMEM((rows, kw), F32)] + [pltpu.VMEM((GLA_DV, GLA_DK), F32)] * hp,
        compiler_params=_cparams("parallel", "arbitrary"),
        name="gla",
    )(proj, proj, proj, proj, small, w_gk_pad, b_gk.reshape(1, GLA_QK), norm_g.reshape(1, GLA_V))


def _causal_conv_silu(xp_ref, x_ref, w_ref, b_ref, first):
    rows = x_ref.shape[0]
    hist = V7X_SUBLANES

    @pl.when(first)
    def _():
        xp_ref[pl.ds(0, hist), :] = jnp.zeros((hist, xp_ref.shape[1]), F32)

    xp_ref[pl.ds(hist, rows), :] = x_ref[...].astype(F32)
    acc = b_ref[...]
    for tap in range(SSM_CONV):
        acc = acc + xp_ref[pl.ds(hist - (SSM_CONV - 1) + tap, rows), :] * w_ref[tap:tap + 1, :]
    xp_ref[pl.ds(0, hist), :] = xp_ref[pl.ds(rows, hist), :]
    return _silu(acc)


def _ssd_kernel(z_ref, xs_ref, b_ref, c_ref, sm_ref, e_ref, cwx_ref, cwb_ref, cwc_ref,
                cbx_ref, cbb_ref, cbc_ref, dtb_ref, alog_ref, dsk_ref, ng_ref,
                o_ref, xpx_ref, xpb_ref, xpc_ref, xs_scr, b_scr, c_scr, dt_scr, cum_scr, *s_refs):
    first = pl.program_id(1) == 0

    @pl.when(first)
    def _():
        for s_ref in s_refs:
            s_ref[...] = jnp.zeros_like(s_ref)

    rows = xs_ref.shape[0]
    xs_scr[...] = _causal_conv_silu(xpx_ref, xs_ref, cwx_ref, cbx_ref, first)
    b_scr[...] = _causal_conv_silu(xpb_ref, b_ref, cwb_ref, cbb_ref, first).astype(BF16)
    c_scr[...] = _causal_conv_silu(xpc_ref, c_ref, cwc_ref, cbc_ref, first).astype(BF16)

    dt_in = sm_ref[...] + dtb_ref[...]
    dt_c = jnp.maximum(dt_in, 0.0) + jnp.log1p(jnp.exp(-jnp.abs(dt_in)))
    da_c = dt_c * (-jnp.exp(alog_ref[...]))
    tri = _chunk_tril(CUM_BLOCK)
    cum_c = jnp.concatenate([_sel_dot(tri, da_c[b * CUM_BLOCK:(b + 1) * CUM_BLOCK])
                             for b in range(rows // CUM_BLOCK)], axis=0)
    dt_scr[...] = _sel_dot_rhs(dt_c, e_ref[0])
    cum_scr[...] = _sel_dot_rhs(cum_c, e_ref[0])

    gw = SSM_GW
    rw = lax.broadcasted_iota(jnp.int32, (CHUNK, gw), 0)
    cw = lax.broadcasted_iota(jnp.int32, (CHUNK, gw), 1)
    diag_tile = (jnp.bitwise_and(cw, SSM_HEADDIM - 1) == rw).astype(F32)
    half = gw // 2
    rb = lax.broadcasted_iota(jnp.int32, (half, half), 0)
    cb_ = lax.broadcasted_iota(jnp.int32, (half, half), 1)
    head_mask = jnp.right_shift(rb, HEAD_SHIFT) == jnp.right_shift(cb_, HEAD_SHIFT)
    reps = half // CHUNK
    for c in range(rows // CHUNK):
        sl = pl.ds(c * CHUNK, CHUNK)
        for gg, s_ref in enumerate(s_refs):
            xl = slice(gg * gw, (gg + 1) * gw)
            nl = slice(gg * SSM_STATE, (gg + 1) * SSM_STATE)
            cum = cum_scr[sl, xl]
            cum_row = jnp.sum(cum * diag_tile, axis=0, keepdims=True)
            seg = jnp.exp(-jnp.abs(cum - cum_row))
            xs = xs_scr[sl, xl]
            bm = b_scr[sl, nl]
            cm = c_scr[sl, nl]
            xdt = xs * dt_scr[sl, xl]
            cbt = _dot_nt(cm, jnp.concatenate([bm] * SSM_HPG, axis=0))
            m_all = (cbt * seg).astype(BF16)
            intra = []
            for hlf in range(2):
                xh = xdt[:, hlf * half:(hlf + 1) * half]
                bd = jnp.where(head_mask, jnp.concatenate([xh] * reps, axis=0), 0.0).astype(BF16)
                intra.append(_dot(m_all[:, hlf * half:(hlf + 1) * half], bd))
            intra = jnp.concatenate(intra, axis=1)
            cum_end = cum[CHUNK - 1:CHUNK, :]
            state = s_ref[...]
            inter = _dot(cm, state.astype(BF16)) * jnp.exp(cum)
            contrib = _dot_tn(bm, (xdt * jnp.exp(cum_end - cum)).astype(BF16))
            s_ref[...] = state * jnp.exp(cum_end) + contrib
            y = intra + inter + xs * dsk_ref[:, xl]
            y = y * _silu(z_ref[sl, xl].astype(F32))
            y = y * lax.rsqrt(jnp.mean(y * y, axis=-1, keepdims=True) + EPS) * ng_ref[:, xl]
            o_ref[sl, xl] = y.astype(o_ref.dtype)


def _ssd(proj, small, expand, conv_w, conv_b, dt_bias, a_log, d_skip, norm_g):
    s = proj.shape[0]
    rows = min(MIX_ROWS, s)
    gp = SSD_GROUPS_PER_STEP
    xw, nw = gp * SSM_GW, gp * SSM_STATE
    zb, xb = P_SSM_Z // xw, P_SSM_X // xw
    bb, cb = P_SSM_B // nw, P_SSM_C // nw
    cwb, cwc = SSM_INNER // nw, (SSM_INNER + SSM_BC) // nw

    def lane_expand(p):
        return jnp.repeat(p.astype(F32), SSM_HEADDIM).reshape(1, SSM_INNER)

    def side_lanes(p):
        return jnp.pad(p.astype(F32), (SMALL_DT0, SMALL_W - SMALL_DT0 - SSM_HEADS)).reshape(1, SMALL_W)

    conv_b2 = conv_b.reshape(1, SSM_CONV_DIM)
    vec = pl.BlockSpec((1, xw), lambda g, i: (0, g))
    side_vec = pl.BlockSpec((1, SMALL_W), lambda g, i: (0, 0))
    return pl.pallas_call(
        _ssd_kernel,
        out_shape=jax.ShapeDtypeStruct((s, SSM_INNER), BF16),
        grid=(SSM_GROUPS // gp, s // rows),
        in_specs=[pl.BlockSpec((rows, xw), lambda g, i: (i, zb + g)),
                  pl.BlockSpec((rows, xw), lambda g, i: (i, xb + g)),
                  pl.BlockSpec((rows, nw), lambda g, i: (i, bb + g)),
                  pl.BlockSpec((rows, nw), lambda g, i: (i, cb + g)),
                  pl.BlockSpec((rows, SMALL_W), lambda g, i: (i, 0)),
                  pl.BlockSpec((1, SMALL_W, xw), lambda g, i: (g, 0, 0)),
                  pl.BlockSpec((SSM_CONV, xw), lambda g, i: (0, g)),
                  pl.BlockSpec((SSM_CONV, nw), lambda g, i: (0, cwb + g)),
                  pl.BlockSpec((SSM_CONV, nw), lambda g, i: (0, cwc + g)),
                  pl.BlockSpec((1, xw), lambda g, i: (0, g)),
                  pl.BlockSpec((1, nw), lambda g, i: (0, cwb + g)),
                  pl.BlockSpec((1, nw), lambda g, i: (0, cwc + g)),
                  side_vec, side_vec, vec, vec],
        out_specs=pl.BlockSpec((rows, xw), lambda g, i: (i, g)),
        scratch_shapes=[pltpu.VMEM((rows + V7X_SUBLANES, xw), F32),
                        pltpu.VMEM((rows + V7X_SUBLANES, nw), F32),
                        pltpu.VMEM((rows + V7X_SUBLANES, nw), F32),
                        pltpu.VMEM((rows, xw), F32),
                        pltpu.VMEM((rows, nw), BF16),
                        pltpu.VMEM((rows, nw), BF16),
                        pltpu.VMEM((rows, xw), F32),
                        pltpu.VMEM((rows, xw), F32)] + [pltpu.VMEM((SSM_STATE, SSM_GW), F32)] * gp,
        compiler_params=_cparams("parallel", "arbitrary"),
        name="ssd",
    )(proj, proj, proj, proj, small, expand, conv_w, conv_w, conv_w, conv_b2, conv_b2, conv_b2,
      side_lanes(dt_bias), side_lanes(a_log), lane_expand(d_skip), norm_g.reshape(1, SSM_INNER))


def _ssd_expand_table():
    xw = SSD_GROUPS_PER_STEP * SSM_GW
    rows = jnp.arange(SMALL_W, dtype=jnp.int32)[None, :, None]
    cols = jnp.arange(xw, dtype=jnp.int32)[None, None, :]
    step = jnp.arange(SSM_INNER // xw, dtype=jnp.int32)[:, None, None]
    return (rows == SMALL_DT0 + (step * xw + cols) // SSM_HEADDIM).astype(BF16)


def _rope_table_kernel(pos_ref, freq_ref, sign_ref, cos_ref, sin_ref):
    ang = pos_ref[...] * freq_ref[...]
    cos_ref[...] = jnp.cos(ang)
    sin_ref[...] = jnp.sin(ang) * sign_ref[...]


def _rope_tables(positions):
    s = positions.shape[-1]
    half = RET_DK // 2
    inv_freq = ROPE_BASE ** (-jnp.arange(half, dtype=F32) / half)
    freq = jnp.concatenate([inv_freq, inv_freq]).reshape(1, RET_DK)
    sign = jnp.concatenate([-jnp.ones((half,), F32), jnp.ones((half,), F32)]).reshape(1, RET_DK)
    pos = positions.astype(F32).reshape(s, 1)
    rows = min(1024, s)
    tab = jax.ShapeDtypeStruct((s, RET_DK), F32)
    return pl.pallas_call(
        _rope_table_kernel,
        out_shape=[tab, tab],
        grid=(s // rows,),
        in_specs=[pl.BlockSpec((rows, 1), lambda i: (i, 0)),
                  pl.BlockSpec((1, RET_DK), lambda i: (0, 0)),
                  pl.BlockSpec((1, RET_DK), lambda i: (0, 0))],
        out_specs=[pl.BlockSpec((rows, RET_DK), lambda i: (i, 0))] * 2,
        compiler_params=_cparams("parallel"),
        name="rope_tables",
    )(pos, freq, sign)


def _ret_kernel(q_ref, k_ref, v_ref, g_ref, cos_ref, sin_ref, dmat_ref, qdec_ref, kdec_ref, sdec_ref,
                ng_ref, o_ref, *s_refs):
    @pl.when(pl.program_id(1) == 0)
    def _():
        for s_ref in s_refs:
            s_ref[...] = jnp.zeros_like(s_ref)

    cos, sin = cos_ref[...], sin_ref[...]

    def rot(t):
        t = t.astype(F32)
        return t * cos + pltpu.roll(t, RET_DK // 2, axis=1) * sin

    for hh, s_ref in enumerate(s_refs):
        kl = slice(hh * RET_DK, (hh + 1) * RET_DK)
        vl = slice(hh * RET_DV, (hh + 1) * RET_DV)
        q = rot(q_ref[:, kl])
        k = rot(k_ref[:, kl]) * (RET_DK ** -0.5)
        v = v_ref[:, vl]
        scores = (_dot_nt(q.astype(BF16), k.astype(BF16)) * dmat_ref[hh]).astype(BF16)
        state = s_ref[...]
        o = _dot(scores, v) + _dot((q * qdec_ref[hh]).astype(BF16), state.astype(BF16))
        s_ref[...] = state * sdec_ref[hh] + _dot_tn((k * kdec_ref[hh]).astype(BF16), v)
        o = o * lax.rsqrt(jnp.mean(o * o, axis=-1, keepdims=True) + EPS) * ng_ref[:, vl]
        o_ref[:, vl] = (o * _silu(g_ref[:, vl].astype(F32))).astype(o_ref.dtype)


def _ret_tables(rows):
    log_gamma = jnp.log1p(-jnp.exp2(-5.0 - jnp.arange(RET_HEADS, dtype=F32)))[:, None, None]
    i = jnp.arange(rows, dtype=jnp.int32)
    dist = (i[:, None] - i[None, :]).astype(F32)[None]
    ch_i, ch_j = (i // CHUNK)[:, None], (i // CHUNK)[None, :]
    dmat = jnp.where((ch_i == ch_j)[None], jnp.exp(log_gamma * jnp.abs(dist)),
                     jnp.where((ch_j < ch_i)[None], jnp.exp(log_gamma * dist), 0.0))
    fi = i.astype(F32)[None, :, None]
    qdec = jnp.broadcast_to(jnp.exp(log_gamma * (fi + 1.0)), (RET_HEADS, rows, RET_DK))
    kdec = jnp.broadcast_to(jnp.exp(log_gamma * (rows - 1.0 - fi)), (RET_HEADS, rows, RET_DK))
    sdec = jnp.broadcast_to(jnp.exp(log_gamma * float(rows)), (RET_HEADS, 1, RET_DV))
    return dmat, qdec, kdec, sdec


def _retention(proj, cos_t, sin_t, tables, norm_g):
    s = proj.shape[0]
    rows = min(RET_BLOCK, s)
    dmat, qdec, kdec, sdec = tables
    hp = RET_HEADS_PER_STEP
    kw, vw = hp * RET_DK, hp * RET_DV
    qb, kb = P_RET_Q // kw, P_RET_K // kw
    vb, gb = P_RET_V // vw, P_RET_G // vw
    return pl.pallas_call(
        _ret_kernel,
        out_shape=jax.ShapeDtypeStruct((s, RET_V), BF16),
        grid=(RET_HEADS // hp, s // rows),
        in_specs=[pl.BlockSpec((rows, kw), lambda h, i: (i, qb + h)),
                  pl.BlockSpec((rows, kw), lambda h, i: (i, kb + h)),
                  pl.BlockSpec((rows, vw), lambda h, i: (i, vb + h)),
                  pl.BlockSpec((rows, vw), lambda h, i: (i, gb + h)),
                  pl.BlockSpec((rows, RET_DK), lambda h, i: (i, 0)),
                  pl.BlockSpec((rows, RET_DK), lambda h, i: (i, 0)),
                  pl.BlockSpec((hp, rows, rows), lambda h, i: (h, 0, 0)),
                  pl.BlockSpec((hp, rows, RET_DK), lambda h, i: (h, 0, 0)),
                  pl.BlockSpec((hp, rows, RET_DK), lambda h, i: (h, 0, 0)),
                  pl.BlockSpec((hp, 1, RET_DV), lambda h, i: (h, 0, 0)),
                  pl.BlockSpec((1, vw), lambda h, i: (0, h))],
        out_specs=pl.BlockSpec((rows, vw), lambda h, i: (i, h)),
        scratch_shapes=[pltpu.VMEM((RET_DK, RET_DV), F32)] * hp,
        compiler_params=_cparams("parallel", "arbitrary"),
        name="retention",
    )(proj, proj, proj, proj, cos_t, sin_t, dmat, qdec, kdec, sdec, norm_g.reshape(1, RET_V))


def _gated_merge_proj(gl_ref, y0_ref, y1_ref, y2_ref, gu0_ref, gu1_ref, gu2_ref, gb0_ref, gb1_ref, gb2_ref,
                      wo0_ref, wo1_ref, wo2_ref):
    gl = gl_ref[...]
    acc = None
    for y_ref, gu_ref, gb_ref, wo_ref in ((y0_ref, gu0_ref, gb0_ref, wo0_ref),
                                          (y1_ref, gu1_ref, gb1_ref, wo1_ref),
                                          (y2_ref, gu2_ref, gb2_ref, wo2_ref)):
        gate = jax.nn.sigmoid(_dot(gl, gu_ref[...]) + gb_ref[...])
        term = gate * _dot(y_ref[...], wo_ref[...])
        acc = term if acc is None else acc + term
    return acc


def _merge_post(proj, ys, gate_w2, gate_b, w_out, layer, w_out_layer, x, g_post, g_next, bm=1024, bn=256):
    s, d = x.shape
    nm = s // bm
    gate_b4 = gate_b.reshape(DEPTH, N_BRANCH, 1, d)
    row = lambda i: jnp.minimum(i, nm - 1)
    y_spec = pl.BlockSpec((bm, BRANCH_W), lambda i, j: (row(i), 0))
    specs = [pl.BlockSpec((bm, GATE_RANK), lambda i, j: (row(i), P_GATE // GATE_RANK)), y_spec, y_spec, y_spec]
    specs += [pl.BlockSpec((None, GATE_RANK, bn),
                           functools.partial(lambda n, i, j: (layer * N_BRANCH + n, 0, j), n))
              for n in range(N_BRANCH)]
    specs += [pl.BlockSpec((None, None, 1, bn), functools.partial(lambda n, i, j: (layer, n, 0, j), n))
              for n in range(N_BRANCH)]
    specs += [pl.BlockSpec((None, BRANCH_W, bn),
                           functools.partial(lambda n, i, j: (w_out_layer * N_BRANCH + n, 0, j), n))
              for n in range(N_BRANCH)]
    args = (proj, ys[0], ys[1], ys[2], gate_w2, gate_w2, gate_w2, gate_b4, gate_b4, gate_b4, w_out, w_out, w_out)
    return _fused_proj_post(_gated_merge_proj, args, specs, x, g_post, 1.0, g_next, "merge_post", bm, bn)


def _xattn_kernel(x_ref, rs_ref, w_ref, k_ref, v_ref, o_ref, w_scr):
    @pl.when(pl.program_id(1) == 0)
    def _():
        w_scr[...] = w_ref[...].astype(BF16)

    q = _dot(x_ref[...], w_scr[...]) * rs_ref[:, 0:1]
    for h in range(q.shape[1] // XA_DIM):
        lo, hi = h * XA_DIM, (h + 1) * XA_DIM
        s = _dot_nt(q[:, lo:hi].astype(BF16), k_ref[:, lo:hi]) * (XA_DIM ** -0.5)
        p = jnp.exp(s - jnp.max(s, axis=-1, keepdims=True))
        p = p / jnp.sum(p, axis=-1, keepdims=True)
        o_ref[:, lo:hi] = _dot(p.astype(BF16), v_ref[:, lo:hi]).astype(o_ref.dtype)


def _xattn(act, w_q, layer, kv, bm=1024, bn=512):
    x, rs = act
    m, k = x.shape
    n_mem = kv.shape[0]
    nb = XA_W // bn
    return pl.pallas_call(
        _xattn_kernel,
        out_shape=jax.ShapeDtypeStruct((m, XA_W), BF16),
        grid=(nb, m // bm),
        in_specs=[pl.BlockSpec((bm, k), lambda j, i: (i, 0)),
                  pl.BlockSpec((bm, ROW_SCALE_W), lambda j, i: (i, 0)),
                  pl.BlockSpec((None, k, bn), lambda j, i: (layer, 0, j)),
                  pl.BlockSpec((n_mem, bn), lambda j, i: (0, j)),
                  pl.BlockSpec((n_mem, bn), lambda j, i: (0, nb + j))],
        out_specs=pl.BlockSpec((bm, bn), lambda j, i: (i, j)),
        scratch_shapes=[pltpu.VMEM((k, bn), BF16)],
        compiler_params=_cparams("arbitrary", "arbitrary"),
        name="xattn",
    )(x, rs, w_q, kv, kv)


def kernel(x, mem, positions, norm_g, mem_norm_g, ffn1_w_in, ffn1_w_out, mix_w_in, gla_w_gk, gla_b_gk,
           gla_norm_g, ssm_conv_w, ssm_conv_b, ssm_dt_bias, ssm_a_log, ssm_d, ssm_norm_g, ret_norm_g,
           gate_w_up, gate_b, mix_w_out, xattn_w_q, xattn_w_kv, xattn_w_o, ffn2_w_in, ffn2_w_out):
    b, s, d = x.shape
    assert b == 1 and d == D_MODEL and s % max(RET_BLOCK, MIX_ROWS, 1024) == 0
    nl = DEPTH
    xr = x.reshape(s, d)
    memr = mem.reshape(mem.shape[1], d)
    cos_t, sin_t = _rope_tables(positions)
    ret_tabs = _ret_tables(min(RET_BLOCK, s))
    expand = _ssd_expand_table()

    xa_out = _cast_bf16(xattn_w_o.reshape(nl * XA_W, d), 512).reshape(nl, XA_W, d)
    mix_w_out3 = mix_w_out.reshape(nl, N_BRANCH * BRANCH_W, d)
    gate_up = _cast_bf16(jnp.transpose(gate_w_up, (0, 2, 1, 3)).reshape(nl * N_BRANCH * GATE_RANK, d),
                         512).reshape(nl * N_BRANCH, GATE_RANK, d)
    mix_w_in_t = jnp.transpose(mix_w_in, (0, 2, 1))
    mix_in_big = _repack_mix_w_in(mix_w_in_t)
    mix_in_side_t = _side_mix_w_in(mix_w_in_t)
    w_gk_pad = jnp.concatenate([gla_w_gk, jnp.zeros((nl, SMALL_W - GLA_RANK, GLA_QK), F32)], axis=1)

    h = (_norm_cast(xr, norm_g[0, 0]), None)
    for l in range(nl):
        ng = norm_g[l]
        a, w_out, mix_out = _swiglu_in(h, ffn1_w_in, l, (ffn1_w_out, mix_w_out3))
        mix_out = mix_out.reshape(N_BRANCH, BRANCH_W, d)
        xr, h = _out_proj_post(a, w_out, 0, xr, ng[1], 0.5, ng[2], "ffn_out_post")
        proj, small = _mix_in(h, mix_in_big, mix_in_side_t, l)
        y_gla = _gla(proj, small, w_gk_pad[l], gla_b_gk[l], gla_norm_g[l])
        y_ssd = _ssd(proj, small, expand, ssm_conv_w[l], ssm_conv_b[l], ssm_dt_bias[l], ssm_a_log[l],
                     ssm_d[l], ssm_norm_g[l])
        y_ret = _retention(proj, cos_t, sin_t, ret_tabs, ret_norm_g[l])
        xr, h = _merge_post(proj, (y_gla, y_ssd, y_ret), gate_up, gate_b, mix_out, l, 0, xr, ng[3], ng[4])
        mem_n = _norm_cast(memr, mem_norm_g[l])
        kv = _matmul_wcast(mem_n, xattn_w_kv, l, BF16, 512, "xattn_kv")
        o = _xattn(h, xattn_w_q, l, kv)
        xr, h = _out_proj_post(o, xa_out, l, xr, ng[5], 1.0, ng[6], "xattn_out_post", bn=1024)
        a, w_out = _swiglu_in(h, ffn2_w_in, l, (ffn2_w_out,))
        g_next = norm_g[l + 1, 0] if l + 1 < nl else None
        xr, h = _out_proj_post(a, w_out, 0, xr, ng[7], 0.5, g_next, "ffn_out_post")
    return xr.reshape(b, s, d)
```

```python
import functools

import numpy as np
import jax
import jax.numpy as jnp
from jax import lax
from jax.experimental import pallas as pl
from jax.experimental.pallas import tpu as pltpu

F32 = jnp.float32
BF16 = jnp.bfloat16

D_MODEL = 4096
DEPTH = 4
CHUNK = 64
D_FF = 4096
EPS = 1e-6
GLA_HEADS, GLA_DK, GLA_DV, GLA_RANK, GLA_TAU = 4, 256, 512, 16, 16.0
GLA_QK, GLA_V = GLA_HEADS * GLA_DK, GLA_HEADS * GLA_DV
SSM_HEADS, SSM_HEADDIM, SSM_GROUPS, SSM_STATE, SSM_CONV = 32, 64, 4, 128, 4
SSM_INNER = SSM_HEADS * SSM_HEADDIM
SSM_BC = SSM_GROUPS * SSM_STATE
SSM_CONV_DIM = SSM_INNER + 2 * SSM_BC
SSM_HPG = SSM_HEADS // SSM_GROUPS
SSM_GW = SSM_INNER // SSM_GROUPS
RET_HEADS, RET_DK, RET_DV = 8, 128, 256
RET_QK, RET_V = RET_HEADS * RET_DK, RET_HEADS * RET_DV
ROPE_BASE = 10000.0
XA_HEADS, XA_DIM = 4, 256
XA_W = XA_HEADS * XA_DIM
N_BRANCH, BRANCH_W, GATE_RANK = 3, 2048, 512

_IN_WIDTHS = (GLA_QK, GLA_QK, GLA_V, GLA_V, GLA_RANK, SSM_INNER, SSM_CONV_DIM, SSM_HEADS,
              RET_QK, RET_QK, RET_V, RET_V, GATE_RANK)
_IN_OFF = np.concatenate([[0], np.cumsum(_IN_WIDTHS)]).tolist()
IN_COLS = _IN_OFF[-1]
P_GLA_Q, P_GLA_K, P_GLA_V, P_GLA_R = 0, 1024, 2048, 4096
P_SSM_Z, P_SSM_X, P_SSM_B, P_SSM_C = 6144, 8192, 10240, 10752
P_RET_Q, P_RET_K, P_RET_V, P_RET_G = 11264, 12288, 13312, 15360
P_GATE = 17408
P_COLS = 17920
SMALL_W = 128
SMALL_DT0 = GLA_RANK

V7X_VMEM_LIMIT = 60 * 1024 * 1024
V7X_SUBLANES = 8
CHUNK_SHIFT = CHUNK.bit_length() - 1
HEAD_SHIFT = SSM_HEADDIM.bit_length() - 1

RET_BLOCK = 512
MIX_ROWS = 512
CUM_BLOCK = 256
GLA_HEADS_PER_STEP = 4
SSD_GROUPS_PER_STEP = 4
RET_HEADS_PER_STEP = 4


def _block_index(col_offset, block_cols):
    assert col_offset % block_cols == 0, (col_offset, block_cols)
    return col_offset // block_cols


def _cparams(*sem):
    return pltpu.CompilerParams(dimension_semantics=sem, vmem_limit_bytes=V7X_VMEM_LIMIT)


def _silu(x):
    return x * jax.nn.sigmoid(x)


def _dot(a, b):
    return jnp.dot(a, b, preferred_element_type=F32)


def _dot_nt(a, b):
    return lax.dot_general(a, b, (((1,), (1,)), ((), ())), preferred_element_type=F32)


def _dot_tn(a, b):
    return lax.dot_general(a, b, (((0,), (0,)), ((), ())), preferred_element_type=F32)


def _split3(x):
    hi = x.astype(BF16)
    r1 = x - hi.astype(F32)
    mid = r1.astype(BF16)
    lo = (r1 - mid.astype(F32)).astype(BF16)
    return hi, mid, lo


def _sel_dot(sel, x):
    hi, mid, lo = _split3(x)
    return _dot(sel, hi) + _dot(sel, mid) + _dot(sel, lo)


def _sel_dot_rhs(x, sel):
    hi, mid, lo = _split3(x)
    return _dot(hi, sel) + _dot(mid, sel) + _dot(lo, sel)


def _chunk_tril(n):
    ri = lax.broadcasted_iota(jnp.int32, (n, n), 0)
    ci = lax.broadcasted_iota(jnp.int32, (n, n), 1)
    same = jnp.right_shift(ri, CHUNK_SHIFT) == jnp.right_shift(ci, CHUNK_SHIFT)
    return jnp.where(same & (ri >= ci), 1.0, 0.0).astype(BF16)


def _cast_kernel(w_ref, o_ref):
    o_ref[...] = w_ref[...].astype(o_ref.dtype)


def _cast_bf16(w, rows):
    r, c = w.shape
    return pl.pallas_call(
        _cast_kernel,
        out_shape=jax.ShapeDtypeStruct((r, c), BF16),
        grid=(r // rows,),
        in_specs=[pl.BlockSpec((rows, c), lambda i: (i, 0))],
        out_specs=pl.BlockSpec((rows, c), lambda i: (i, 0)),
        compiler_params=_cparams("parallel"),
        name="cast_bf16",
    )(w)


REPACK_COLS = 512


def _packed_to_source_col(c):
    skipped = jnp.where(c >= P_SSM_Z, GLA_RANK, 0) + jnp.where(c >= P_RET_Q, SSM_HEADS, 0)
    return pl.multiple_of(c + skipped, GLA_RANK)


def _repack_kernel(w_ref, o_ref):
    o_ref[...] = w_ref[...].T.astype(o_ref.dtype)


def _repack_mix_w_in(w_t):
    nl, _, d = w_t.shape
    return pl.pallas_call(
        _repack_kernel,
        out_shape=jax.ShapeDtypeStruct((nl, d, P_COLS), BF16),
        grid=(nl, P_COLS // REPACK_COLS),
        in_specs=[pl.BlockSpec((pl.Squeezed(), pl.Element(REPACK_COLS), pl.Element(d)),
                               lambda l, j: (l, _packed_to_source_col(j * REPACK_COLS), 0))],
        out_specs=pl.BlockSpec((None, d, REPACK_COLS), lambda l, j: (l, 0, j)),
        compiler_params=_cparams("parallel", "parallel"),
        name="repack_mix_w_in",
    )(w_t)


def _side_w_kernel(gk_ref, dt_ref, o_ref):
    o_ref[...] = jnp.zeros(o_ref.shape, o_ref.dtype)
    o_ref[0:GLA_RANK, :] = gk_ref[...].astype(o_ref.dtype)
    o_ref[GLA_RANK:GLA_RANK + SSM_HEADS, :] = dt_ref[...].astype(o_ref.dtype)


def _side_mix_w_in(w_t):
    nl, _, d = w_t.shape
    o = _IN_OFF
    return pl.pallas_call(
        _side_w_kernel,
        out_shape=jax.ShapeDtypeStruct((nl, SMALL_W, d), BF16),
        grid=(nl,),
        in_specs=[pl.BlockSpec((pl.Squeezed(), pl.Element(GLA_RANK), pl.Element(d)), lambda l: (l, o[4], 0)),
                  pl.BlockSpec((pl.Squeezed(), pl.Element(SSM_HEADS), pl.Element(d)), lambda l: (l, o[7], 0))],
        out_specs=pl.BlockSpec((None, SMALL_W, d), lambda l: (l, 0, 0)),
        compiler_params=_cparams("parallel"),
        name="side_mix_w_in",
    )(w_t, w_t)


def _norm_cast_kernel(x_ref, g_ref, o_ref):
    x = x_ref[...]
    r = lax.rsqrt(jnp.mean(x * x, axis=-1, keepdims=True) + EPS)
    o_ref[...] = (x * r * g_ref[...]).astype(o_ref.dtype)


def _norm_cast(x, g, rows=256):
    m, d = x.shape
    rows = min(rows, m)
    return pl.pallas_call(
        _norm_cast_kernel,
        out_shape=jax.ShapeDtypeStruct((m, d), BF16),
        grid=(m // rows,),
        in_specs=[pl.BlockSpec((rows, d), lambda i: (i, 0)), pl.BlockSpec((1, d), lambda i: (0, 0))],
        out_specs=pl.BlockSpec((rows, d), lambda i: (i, 0)),
        compiler_params=_cparams("parallel"),
        name="norm_cast",
    )(x, g.reshape(1, d))


ROW_SCALE_W = 128


def _mix_in_kernel(x_ref, rs_ref, w_ref, wside_ref, o_ref, side_ref):
    x = x_ref[...]
    rs = rs_ref[:, 0:1]
    o_ref[...] = (_dot(x, w_ref[...]) * rs).astype(o_ref.dtype)

    @pl.when(pl.program_id(1) == 0)
    def _():
        side_ref[...] = _dot_nt(x, wside_ref[...]) * rs


def _mix_in(act, w, w_side_t, layer, bm=1024, bn=1280):
    x, rs = act
    m, k = x.shape
    n = w.shape[2]
    return pl.pallas_call(
        _mix_in_kernel,
        out_shape=[jax.ShapeDtypeStruct((m, n), BF16), jax.ShapeDtypeStruct((m, SMALL_W), F32)],
        grid=(m // bm, n // bn),
        in_specs=[pl.BlockSpec((bm, k), lambda i, j: (i, 0)),
                  pl.BlockSpec((bm, ROW_SCALE_W), lambda i, j: (i, 0)),
                  pl.BlockSpec((None, k, bn), lambda i, j: (layer, 0, j)),
                  pl.BlockSpec((None, SMALL_W, k), lambda i, j: (layer, 0, 0))],
        out_specs=[pl.BlockSpec((bm, bn), lambda i, j: (i, j)),
                   pl.BlockSpec((bm, SMALL_W), lambda i, j: (i, 0))],
        compiler_params=_cparams("parallel", "arbitrary"),
        name="mix_in",
    )(x, rs, w, w_side_t)


def _mm_wcast_kernel(x_ref, w_ref, o_ref):
    o_ref[...] = _dot(x_ref[...], w_ref[...].astype(BF16)).astype(o_ref.dtype)


def _matmul_wcast(x, w, layer, out_dtype, bn, name):
    m, k = x.shape
    n = w.shape[2]
    return pl.pallas_call(
        _mm_wcast_kernel,
        out_shape=jax.ShapeDtypeStruct((m, n), out_dtype),
        grid=(n // bn,),
        in_specs=[pl.BlockSpec((m, k), lambda j: (0, 0)),
                  pl.BlockSpec((None, k, bn), lambda j: (layer, 0, j))],
        out_specs=pl.BlockSpec((m, bn), lambda j: (0, j)),
        compiler_params=_cparams("parallel"),
        name=name,
    )(x, w)


def _swiglu_kernel(bn, scaled, n_side, x_ref, *refs):
    first_w = 1 if scaled else 0
    wg_ref, wu_ref = refs[first_w:first_w + 2]
    side_refs = refs[first_w + 2:first_w + 2 + n_side]
    o_ref = refs[first_w + 2 + n_side]
    side_o_refs = refs[first_w + 3 + n_side:first_w + 3 + 2 * n_side]
    w_scr = refs[-1]

    @pl.when(pl.program_id(1) == 0)
    def _():
        w_scr[:, 0:bn] = wg_ref[...].astype(BF16)
        w_scr[:, bn:2 * bn] = wu_ref[...].astype(BF16)

    r = _dot(x_ref[...], w_scr[...])
    if scaled:
        r = r * refs[0][:, 0:1]
    o_ref[...] = (_silu(r[:, 0:bn]) * r[:, bn:2 * bn]).astype(o_ref.dtype)
    for s_ref, so_ref in zip(side_refs, side_o_refs):
        so_ref[...] = s_ref[...].astype(so_ref.dtype)


def _swiglu_in(act, w, layer, side_ws, bm=512, bn=512):
    x, rs = act
    m, k = x.shape
    f = w.shape[2] // 2
    nb, nmb = f // bn, m // bm
    steps = nb * nmb
    scaled = rs is not None
    in_specs = [pl.BlockSpec((bm, k), lambda j, i: (i, 0))]
    if scaled:
        in_specs.append(pl.BlockSpec((bm, ROW_SCALE_W), lambda j, i: (i, 0)))
    in_specs += [pl.BlockSpec((None, k, bn), lambda j, i: (layer, 0, j)),
                 pl.BlockSpec((None, k, bn), lambda j, i: (layer, 0, j + nb))]
    out_shape = [jax.ShapeDtypeStruct((m, f), BF16)]
    out_specs = [pl.BlockSpec((bm, bn), lambda j, i: (i, j))]
    for sw in side_ws:
        _, r_side, c_side = sw.shape
        slab = r_side // steps
        in_specs.append(pl.BlockSpec((None, slab, c_side), lambda j, i: (layer, j * nmb + i, 0)))
        out_shape.append(jax.ShapeDtypeStruct((1, r_side, c_side), BF16))
        out_specs.append(pl.BlockSpec((None, slab, c_side), lambda j, i: (0, j * nmb + i, 0)))
    args = ((x, rs) if scaled else (x,)) + (w, w) + tuple(side_ws)
    return pl.pallas_call(
        functools.partial(_swiglu_kernel, bn, scaled, len(side_ws)),
        out_shape=out_shape,
        grid=(nb, nmb),
        in_specs=in_specs,
        out_specs=out_specs,
        scratch_shapes=[pltpu.VMEM((k, 2 * bn), BF16)],
        compiler_params=_cparams("arbitrary", "arbitrary"),
        name="swiglu_in",
    )(*args)


def _plain_out_proj(lhs_ref, w_ref):
    return _dot(lhs_ref[...], w_ref[...])


def _out_proj_kernel(nm, scale, emit_h, proj_fn, n_proj, *refs):
    proj_refs = refs[:n_proj]
    x_ref, gp_ref, gn_ref, xo_ref = refs[n_proj:n_proj + 4]
    rest = refs[n_proj + 4:]
    if emit_h:
        ho_ref, rs_ref, y_scr, ssq_scr, ssq2_scr = rest
    else:
        y_scr, ssq_scr = rest
    i, n = pl.program_id(0), pl.program_id(1)
    cur = lax.rem(i, 2)
    prev = 1 - cur
    inv_d = 1.0 / (y_scr.shape[0] * y_scr.shape[2])

    def matmul_part():
        yb = proj_fn(*proj_refs)
        y_scr[n] = yb.astype(BF16)
        part = jnp.sum(yb * yb, axis=-1, keepdims=True)
        ssq_scr[cur] = jnp.where(n == 0, part, ssq_scr[cur] + part)

    def residual_part():
        r = lax.rsqrt(ssq_scr[prev] * inv_d + EPS)
        xn = x_ref[...] + scale * (y_scr[n].astype(F32) * r * gp_ref[...])
        xo_ref[...] = xn
        if emit_h:
            ho_ref[...] = (xn * gn_ref[...]).astype(BF16)
            part = jnp.sum(xn * xn, axis=-1, keepdims=True)
            acc = jnp.where(n == 0, part, ssq2_scr[...] + part)
            ssq2_scr[...] = acc
            rs_ref[...] = jnp.broadcast_to(lax.rsqrt(acc * inv_d + EPS), rs_ref.shape)

    @pl.when(i == 0)
    def _():
        matmul_part()

    @pl.when((i > 0) & (i < nm))
    def _():
        residual_part()
        matmul_part()

    @pl.when(i == nm)
    def _():
        residual_part()


def _out_proj_post(lhs, w, layer, x, g_post, scale, g_next, name, bm=1024, bn=512):
    m, k = lhs.shape
    nm = m // bm
    proj_specs = [pl.BlockSpec((bm, k), lambda i, n: (jnp.minimum(i, nm - 1), 0)),
                  pl.BlockSpec((None, k, bn), lambda i, n: (layer, 0, n))]
    return _fused_proj_post(_plain_out_proj, (lhs, w), proj_specs, x, g_post, scale, g_next, name, bm, bn)


def _fused_proj_post(proj_fn, proj_args, proj_specs, x, g_post, scale, g_next, name, bm, bn):
    m, d = x.shape
    nm, nn = m // bm, d // bn
    emit_h = g_next is not None
    gn = g_next if emit_h else g_post
    res_map = lambda i, n: (jnp.maximum(i - 1, 0), jnp.where(i == 0, 0, n))
    vec_map = lambda i, n: (0, jnp.where(i == 0, 0, n))
    res_spec = pl.BlockSpec((bm, bn), res_map)
    out_shape = [jax.ShapeDtypeStruct((m, d), F32)]
    out_specs = [res_spec]
    scratch = [pltpu.VMEM((nn, bm, bn), BF16), pltpu.VMEM((2, bm, 1), F32)]
    if emit_h:
        out_shape += [jax.ShapeDtypeStruct((m, d), BF16), jax.ShapeDtypeStruct((m, ROW_SCALE_W), F32)]
        out_specs += [res_spec, pl.BlockSpec((bm, ROW_SCALE_W), lambda i, n: (jnp.maximum(i - 1, 0), 0))]
        scratch.append(pltpu.VMEM((bm, 1), F32))
    res = pl.pallas_call(
        functools.partial(_out_proj_kernel, nm, scale, emit_h, proj_fn, len(proj_args)),
        out_shape=out_shape,
        grid=(nm + 1, nn),
        in_specs=list(proj_specs) + [res_spec, pl.BlockSpec((1, bn), vec_map), pl.BlockSpec((1, bn), vec_map)],
        out_specs=out_specs,
        scratch_shapes=scratch,
        compiler_params=_cparams("arbitrary", "arbitrary"),
        name=name,
    )(*proj_args, x, g_post.reshape(1, d), gn.reshape(1, d))
    return (res[0], (res[1], res[2])) if emit_h else (res[0], None)


def _gla_kernel(q_ref, k_ref, v_ref, r_ref, sm_ref, wgk_ref, bgk_ref, ng_ref, o_ref, cum_ref, *st_refs):
    @pl.when(pl.program_id(1) == 0)
    def _():
        for st_ref in st_refs:
            st_ref[...] = jnp.zeros_like(st_ref)

    rows = q_ref.shape[0]
    a_hi, a_mid, _ = _split3(sm_ref[...])
    w_hi, w_mid, _ = _split3(wgk_ref[...])
    z = _dot(a_hi, w_hi) + _dot(a_hi, w_mid) + _dot(a_mid, w_hi) + bgk_ref[...]
    logf = (jnp.minimum(z, 0.0) - jnp.log1p(jnp.exp(-jnp.abs(z)))) * (1.0 / GLA_TAU)
    tri = _chunk_tril(CUM_BLOCK)
    for b in range(rows // CUM_BLOCK):
        sl = slice(b * CUM_BLOCK, (b + 1) * CUM_BLOCK)
        cum_ref[sl, :] = _sel_dot(tri, logf[sl])

    ri = lax.broadcasted_iota(jnp.int32, (CHUNK, CHUNK), 0)
    ci = lax.broadcasted_iota(jnp.int32, (CHUNK, CHUNK), 1)
    tril = ri >= ci
    for c in range(rows // CHUNK):
        sl = pl.ds(c * CHUNK, CHUNK)
        for hh, st_ref in enumerate(st_refs):
            kl = slice(hh * GLA_DK, (hh + 1) * GLA_DK)
            vl = slice(hh * GLA_DV, (hh + 1) * GLA_DV)
            cum = cum_ref[sl, kl]
            cum_end = cum[CHUNK - 1:CHUNK, :]
            up, down = jnp.exp(cum), jnp.exp(-cum)
            q = q_ref[sl, kl].astype(F32) * (GLA_DK ** -0.5)
            k = k_ref[sl, kl].astype(F32)
            v = v_ref[sl, vl]
            q_dec = (q * up).astype(BF16)
            a_past = _dot_nt(q_dec, (k * down).astype(BF16))
            a_future = _dot_nt((q * down).astype(BF16), (k * up).astype(BF16))
            attn = jnp.where(tril, a_past, a_future).astype(BF16)
            state_t = st_ref[...]
            o = _dot(attn, v) + _dot_nt(q_dec, state_t.astype(BF16))
            contrib_t = _dot_tn(v, (k * jnp.exp(cum_end - cum)).astype(BF16))
            st_ref[...] = state_t * jnp.exp(cum_end) + contrib_t
            o = o * lax.rsqrt(jnp.mean(o * o, axis=-1, keepdims=True) + EPS) * ng_ref[:, vl]
            o_ref[sl, vl] = (o * _silu(r_ref[sl, vl].astype(F32))).astype(o_ref.dtype)


def _gla(proj, small, w_gk_pad, b_gk, norm_g):
    s = proj.shape[0]
    rows = min(MIX_ROWS, s)
    hp = GLA_HEADS_PER_STEP
    kw, vw = hp * GLA_DK, hp * GLA_DV
    qb, vb = _block_index(P_GLA_Q, kw), _block_index(P_GLA_V, vw)
    kb, rb = _block_index(P_GLA_K, kw), _block_index(P_GLA_R, vw)
    return pl.pallas_call(
        _gla_kernel,
        out_shape=jax.ShapeDtypeStruct((s, GLA_V), BF16),
        grid=(GLA_HEADS // hp, s // rows),
        in_specs=[pl.BlockSpec((rows, kw), lambda h, i: (i, qb + h)),
                  pl.BlockSpec((rows, kw), lambda h, i: (i, kb + h)),
                  pl.BlockSpec((rows, vw), lambda h, i: (i, vb + h)),
                  pl.BlockSpec((rows, vw), lambda h, i: (i, rb + h)),
                  pl.BlockSpec((rows, SMALL_W), lambda h, i: (i, 0)),
                  pl.BlockSpec((SMALL_W, kw), lambda h, i: (0, h)),
                  pl.BlockSpec((1, kw), lambda h, i: (0, h)),
                  pl.BlockSpec((1, vw), lambda h, i: (0, h))],
        out_specs=pl.BlockSpec((rows, vw), lambda h, i: (i, h)),
        scratch_shapes=[pltpu.VMEM((rows, kw), F32)] + [pltpu.VMEM((GLA_DV, GLA_DK), F32)] * hp,
        compiler_params=_cparams("parallel", "arbitrary"),
        name="gla",
    )(proj, proj, proj, proj, small, w_gk_pad, b_gk.reshape(1, GLA_QK), norm_g.reshape(1, GLA_V))


def _causal_conv_silu(xp_ref, x_ref, w_ref, b_ref, first):
    rows = x_ref.shape[0]
    hist = V7X_SUBLANES

    @pl.when(first)
    def _():
        xp_ref[pl.ds(0, hist), :] = jnp.zeros((hist, xp_ref.shape[1]), F32)

    xp_ref[pl.ds(hist, rows), :] = x_ref[...].astype(F32)
    acc = b_ref[...]
    for tap in range(SSM_CONV):
        acc = acc + xp_ref[pl.ds(hist - (SSM_CONV - 1) + tap, rows), :] * w_ref[tap:tap + 1, :]
    xp_ref[pl.ds(0, hist), :] = xp_ref[pl.ds(rows, hist), :]
    return _silu(acc)


def _ssd_kernel(z_ref, xs_ref, b_ref, c_ref, sm_ref, e_ref, cwx_ref, cwb_ref, cwc_ref,
                cbx_ref, cbb_ref, cbc_ref, dtb_ref, alog_ref, dsk_ref, ng_ref,
                o_ref, xpx_ref, xpb_ref, xpc_ref, xs_scr, b_scr, c_scr, dt_scr, cum_scr, *s_refs):
    first = pl.program_id(1) == 0

    @pl.when(first)
    def _():
        for s_ref in s_refs:
            s_ref[...] = jnp.zeros_like(s_ref)

    rows = xs_ref.shape[0]
    xs_scr[...] = _causal_conv_silu(xpx_ref, xs_ref, cwx_ref, cbx_ref, first)
    b_scr[...] = _causal_conv_silu(xpb_ref, b_ref, cwb_ref, cbb_ref, first).astype(BF16)
    c_scr[...] = _causal_conv_silu(xpc_ref, c_ref, cwc_ref, cbc_ref, first).astype(BF16)

    dt_in = sm_ref[...] + dtb_ref[...]
    dt_c = jnp.maximum(dt_in, 0.0) + jnp.log1p(jnp.exp(-jnp.abs(dt_in)))
    da_c = dt_c * (-jnp.exp(alog_ref[...]))
    tri = _chunk_tril(CUM_BLOCK)
    cum_c = jnp.concatenate([_sel_dot(tri, da_c[b * CUM_BLOCK:(b + 1) * CUM_BLOCK])
                             for b in range(rows // CUM_BLOCK)], axis=0)
    dt_scr[...] = _sel_dot_rhs(dt_c, e_ref[0])
    cum_scr[...] = _sel_dot_rhs(cum_c, e_ref[0])

    gw = SSM_GW
    rw = lax.broadcasted_iota(jnp.int32, (CHUNK, gw), 0)
    cw = lax.broadcasted_iota(jnp.int32, (CHUNK, gw), 1)
    diag_tile = (jnp.bitwise_and(cw, SSM_HEADDIM - 1) == rw).astype(F32)
    half = gw // 2
    rb = lax.broadcasted_iota(jnp.int32, (half, half), 0)
    cb_ = lax.broadcasted_iota(jnp.int32, (half, half), 1)
    head_mask = jnp.right_shift(rb, HEAD_SHIFT) == jnp.right_shift(cb_, HEAD_SHIFT)
    reps = half // CHUNK
    for c in range(rows // CHUNK):
        sl = pl.ds(c * CHUNK, CHUNK)
        for gg, s_ref in enumerate(s_refs):
            xl = slice(gg * gw, (gg + 1) * gw)
            nl = slice(gg * SSM_STATE, (gg + 1) * SSM_STATE)
            cum = cum_scr[sl, xl]
            cum_row = jnp.sum(cum * diag_tile, axis=0, keepdims=True)
            seg = jnp.exp(-jnp.abs(cum - cum_row))
            xs = xs_scr[sl, xl]
            bm = b_scr[sl, nl]
            cm = c_scr[sl, nl]
            xdt = xs * dt_scr[sl, xl]
            cbt = _dot_nt(cm, jnp.concatenate([bm] * SSM_HPG, axis=0))
            m_all = (cbt * seg).astype(BF16)
            intra = []
            for hlf in range(2):
                xh = xdt[:, hlf * half:(hlf + 1) * half]
                bd = jnp.where(head_mask, jnp.concatenate([xh] * reps, axis=0), 0.0).astype(BF16)
                intra.append(_dot(m_all[:, hlf * half:(hlf + 1) * half], bd))
            intra = jnp.concatenate(intra, axis=1)
            cum_end = cum[CHUNK - 1:CHUNK, :]
            state = s_ref[...]
            inter = _dot(cm, state.astype(BF16)) * jnp.exp(cum)
            contrib = _dot_tn(bm, (xdt * jnp.exp(cum_end - cum)).astype(BF16))
            s_ref[...] = state * jnp.exp(cum_end) + contrib
            y = intra + inter + xs * dsk_ref[:, xl]
            y = y * _silu(z_ref[sl, xl].astype(F32))
            y = y * lax.rsqrt(jnp.mean(y * y, axis=-1, keepdims=True) + EPS) * ng_ref[:, xl]
            o_ref[sl, xl] = y.astype(o_ref.dtype)


def _ssd(proj, small, expand, conv_w, conv_b, dt_bias, a_log, d_skip, norm_g):
    s = proj.shape[0]
    rows = min(MIX_ROWS, s)
    gp = SSD_GROUPS_PER_STEP
    xw, nw = gp * SSM_GW, gp * SSM_STATE
    zb, xb = _block_index(P_SSM_Z, xw), _block_index(P_SSM_X, xw)
    bb, cb = _block_index(P_SSM_B, nw), _block_index(P_SSM_C, nw)
    cwb, cwc = _block_index(SSM_INNER, nw), _block_index(SSM_INNER + SSM_BC, nw)

    def lane_expand(p):
        return jnp.repeat(p.astype(F32), SSM_HEADDIM).reshape(1, SSM_INNER)

    def side_lanes(p):
        return jnp.pad(p.astype(F32), (SMALL_DT0, SMALL_W - SMALL_DT0 - SSM_HEADS)).reshape(1, SMALL_W)

    conv_b2 = conv_b.reshape(1, SSM_CONV_DIM)
    vec = pl.BlockSpec((1, xw), lambda g, i: (0, g))
    side_vec = pl.BlockSpec((1, SMALL_W), lambda g, i: (0, 0))
    return pl.pallas_call(
        _ssd_kernel,
        out_shape=jax.ShapeDtypeStruct((s, SSM_INNER), BF16),
        grid=(SSM_GROUPS // gp, s // rows),
        in_specs=[pl.BlockSpec((rows, xw), lambda g, i: (i, zb + g)),
                  pl.BlockSpec((rows, xw), lambda g, i: (i, xb + g)),
                  pl.BlockSpec((rows, nw), lambda g, i: (i, bb + g)),
                  pl.BlockSpec((rows, nw), lambda g, i: (i, cb + g)),
                  pl.BlockSpec((rows, SMALL_W), lambda g, i: (i, 0)),
                  pl.BlockSpec((1, SMALL_W, xw), lambda g, i: (g, 0, 0)),
                  pl.BlockSpec((SSM_CONV, xw), lambda g, i: (0, g)),
                  pl.BlockSpec((SSM_CONV, nw), lambda g, i: (0, cwb + g)),
                  pl.BlockSpec((SSM_CONV, nw), lambda g, i: (0, cwc + g)),
                  pl.BlockSpec((1, xw), lambda g, i: (0, g)),
                  pl.BlockSpec((1, nw), lambda g, i: (0, cwb + g)),
                  pl.BlockSpec((1, nw), lambda g, i: (0, cwc + g)),
                  side_vec, side_vec, vec, vec],
        out_specs=pl.BlockSpec((rows, xw), lambda g, i: (i, g)),
        scratch_shapes=[pltpu.VMEM((rows + V7X_SUBLANES, xw), F32),
                        pltpu.VMEM((rows + V7X_SUBLANES, nw), F32),
                        pltpu.VMEM((rows + V7X_SUBLANES, nw), F32),
                        pltpu.VMEM((rows, xw), F32),
                        pltpu.VMEM((rows, nw), BF16),
                        pltpu.VMEM((rows, nw), BF16),
                        pltpu.VMEM((rows, xw), F32),
                        pltpu.VMEM((rows, xw), F32)] + [pltpu.VMEM((SSM_STATE, SSM_GW), F32)] * gp,
        compiler_params=_cparams("parallel", "arbitrary"),
        name="ssd",
    )(proj, proj, proj, proj, small, expand, conv_w, conv_w, conv_w, conv_b2, conv_b2, conv_b2,
      side_lanes(dt_bias), side_lanes(a_log), lane_expand(d_skip), norm_g.reshape(1, SSM_INNER))


def _ssd_expand_table():
    xw = SSD_GROUPS_PER_STEP * SSM_GW
    rows = jnp.arange(SMALL_W, dtype=jnp.int32)[None, :, None]
    cols = jnp.arange(xw, dtype=jnp.int32)[None, None, :]
    step = jnp.arange(SSM_INNER // xw, dtype=jnp.int32)[:, None, None]
    return (rows == SMALL_DT0 + (step * xw + cols) // SSM_HEADDIM).astype(BF16)


def _rope_table_kernel(pos_ref, freq_ref, sign_ref, cos_ref, sin_ref):
    ang = pos_ref[...] * freq_ref[...]
    cos_ref[...] = jnp.cos(ang)
    sin_ref[...] = jnp.sin(ang) * sign_ref[...]


def _rope_tables(positions):
    s = positions.shape[-1]
    half = RET_DK // 2
    inv_freq = ROPE_BASE ** (-jnp.arange(half, dtype=F32) / half)
    freq = jnp.concatenate([inv_freq, inv_freq]).reshape(1, RET_DK)
    sign = jnp.concatenate([-jnp.ones((half,), F32), jnp.ones((half,), F32)]).reshape(1, RET_DK)
    pos = positions.astype(F32).reshape(s, 1)
    rows = min(1024, s)
    tab = jax.ShapeDtypeStruct((s, RET_DK), F32)
    return pl.pallas_call(
        _rope_table_kernel,
        out_shape=[tab, tab],
        grid=(s // rows,),
        in_specs=[pl.BlockSpec((rows, 1), lambda i: (i, 0)),
                  pl.BlockSpec((1, RET_DK), lambda i: (0, 0)),
                  pl.BlockSpec((1, RET_DK), lambda i: (0, 0))],
        out_specs=[pl.BlockSpec((rows, RET_DK), lambda i: (i, 0))] * 2,
        compiler_params=_cparams("parallel"),
        name="rope_tables",
    )(pos, freq, sign)


def _ret_kernel(q_ref, k_ref, v_ref, g_ref, cos_ref, sin_ref, dmat_ref, qdec_ref, kdec_ref, sdec_ref,
                ng_ref, o_ref, *s_refs):
    @pl.when(pl.program_id(1) == 0)
    def _():
        for s_ref in s_refs:
            s_ref[...] = jnp.zeros_like(s_ref)

    cos, sin = cos_ref[...], sin_ref[...]

    def rot(t):
        t = t.astype(F32)
        return t * cos + pltpu.roll(t, RET_DK // 2, axis=1) * sin

    for hh, s_ref in enumerate(s_refs):
        kl = slice(hh * RET_DK, (hh + 1) * RET_DK)
        vl = slice(hh * RET_DV, (hh + 1) * RET_DV)
        q = rot(q_ref[:, kl])
        k = rot(k_ref[:, kl]) * (RET_DK ** -0.5)
        v = v_ref[:, vl]
        scores = (_dot_nt(q.astype(BF16), k.astype(BF16)) * dmat_ref[hh]).astype(BF16)
        state = s_ref[...]
        o = _dot(scores, v) + _dot((q * qdec_ref[hh]).astype(BF16), state.astype(BF16))
        s_ref[...] = state * sdec_ref[hh] + _dot_tn((k * kdec_ref[hh]).astype(BF16), v)
        o = o * lax.rsqrt(jnp.mean(o * o, axis=-1, keepdims=True) + EPS) * ng_ref[:, vl]
        o_ref[:, vl] = (o * _silu(g_ref[:, vl].astype(F32))).astype(o_ref.dtype)


def _ret_tables(rows):
    log_gamma = jnp.log1p(-jnp.exp2(-5.0 - jnp.arange(RET_HEADS, dtype=F32)))[:, None, None]
    i = jnp.arange(rows, dtype=jnp.int32)
    dist = (i[:, None] - i[None, :]).astype(F32)[None]
    ch_i, ch_j = (i // CHUNK)[:, None], (i // CHUNK)[None, :]
    dmat = jnp.where((ch_i == ch_j)[None], jnp.exp(log_gamma * jnp.abs(dist)),
                     jnp.where((ch_j < ch_i)[None], jnp.exp(log_gamma * dist), 0.0))
    fi = i.astype(F32)[None, :, None]
    qdec = jnp.broadcast_to(jnp.exp(log_gamma * (fi + 1.0)), (RET_HEADS, rows, RET_DK))
    kdec = jnp.broadcast_to(jnp.exp(log_gamma * (rows - 1.0 - fi)), (RET_HEADS, rows, RET_DK))
    sdec = jnp.broadcast_to(jnp.exp(log_gamma * float(rows)), (RET_HEADS, 1, RET_DV))
    return dmat, qdec, kdec, sdec


def _retention(proj, cos_t, sin_t, tables, norm_g):
    s = proj.shape[0]
    rows = min(RET_BLOCK, s)
    dmat, qdec, kdec, sdec = tables
    hp = RET_HEADS_PER_STEP
    kw, vw = hp * RET_DK, hp * RET_DV
    qb, kb = _block_index(P_RET_Q, kw), _block_index(P_RET_K, kw)
    vb, gb = _block_index(P_RET_V, vw), _block_index(P_RET_G, vw)
    return pl.pallas_call(
        _ret_kernel,
        out_shape=jax.ShapeDtypeStruct((s, RET_V), BF16),
        grid=(RET_HEADS // hp, s // rows),
        in_specs=[pl.BlockSpec((rows, kw), lambda h, i: (i, qb + h)),
                  pl.BlockSpec((rows, kw), lambda h, i: (i, kb + h)),
                  pl.BlockSpec((rows, vw), lambda h, i: (i, vb + h)),
                  pl.BlockSpec((rows, vw), lambda h, i: (i, gb + h)),
                  pl.BlockSpec((rows, RET_DK), lambda h, i: (i, 0)),
                  pl.BlockSpec((rows, RET_DK), lambda h, i: (i, 0)),
                  pl.BlockSpec((hp, rows, rows), lambda h, i: (h, 0, 0)),
                  pl.BlockSpec((hp, rows, RET_DK), lambda h, i: (h, 0, 0)),
                  pl.BlockSpec((hp, rows, RET_DK), lambda h, i: (h, 0, 0)),
                  pl.BlockSpec((hp, 1, RET_DV), lambda h, i: (h, 0, 0)),
                  pl.BlockSpec((1, vw), lambda h, i: (0, h))],
        out_specs=pl.BlockSpec((rows, vw), lambda h, i: (i, h)),
        scratch_shapes=[pltpu.VMEM((RET_DK, RET_DV), F32)] * hp,
        compiler_params=_cparams("parallel", "arbitrary"),
        name="retention",
    )(proj, proj, proj, proj, cos_t, sin_t, dmat, qdec, kdec, sdec, norm_g.reshape(1, RET_V))


def _gated_merge_proj(gl_ref, y0_ref, y1_ref, y2_ref, gu0_ref, gu1_ref, gu2_ref, gb0_ref, gb1_ref, gb2_ref,
                      wo0_ref, wo1_ref, wo2_ref):
    gl = gl_ref[...]
    acc = None
    for y_ref, gu_ref, gb_ref, wo_ref in ((y0_ref, gu0_ref, gb0_ref, wo0_ref),
                                          (y1_ref, gu1_ref, gb1_ref, wo1_ref),
                                          (y2_ref, gu2_ref, gb2_ref, wo2_ref)):
        gate = jax.nn.sigmoid(_dot(gl, gu_ref[...]) + gb_ref[...])
        term = gate * _dot(y_ref[...], wo_ref[...])
        acc = term if acc is None else acc + term
    return acc


def _merge_post(proj, ys, gate_w2, gate_b, w_out, layer, w_out_layer, x, g_post, g_next, bm=1024, bn=256):
    s, d = x.shape
    nm = s // bm
    gate_b4 = gate_b.reshape(DEPTH, N_BRANCH, 1, d)
    row = lambda i: jnp.minimum(i, nm - 1)
    y_spec = pl.BlockSpec((bm, BRANCH_W), lambda i, j: (row(i), 0))
    specs = [pl.BlockSpec((bm, GATE_RANK), lambda i, j: (row(i), P_GATE // GATE_RANK)), y_spec, y_spec, y_spec]
    specs += [pl.BlockSpec((None, GATE_RANK, bn),
                           functools.partial(lambda n, i, j: (layer * N_BRANCH + n, 0, j), n))
              for n in range(N_BRANCH)]
    specs += [pl.BlockSpec((None, None, 1, bn), functools.partial(lambda n, i, j: (layer, n, 0, j), n))
              for n in range(N_BRANCH)]
    specs += [pl.BlockSpec((None, BRANCH_W, bn),
                           functools.partial(lambda n, i, j: (w_out_layer * N_BRANCH + n, 0, j), n))
              for n in range(N_BRANCH)]
    args = (proj, ys[0], ys[1], ys[2], gate_w2, gate_w2, gate_w2, gate_b4, gate_b4, gate_b4, w_out, w_out, w_out)
    return _fused_proj_post(_gated_merge_proj, args, specs, x, g_post, 1.0, g_next, "merge_post", bm, bn)


def _xattn_kernel(x_ref, rs_ref, w_ref, k_ref, v_ref, o_ref, w_scr):
    @pl.when(pl.program_id(1) == 0)
    def _():
        w_scr[...] = w_ref[...].astype(BF16)

    q = _dot(x_ref[...], w_scr[...]) * rs_ref[:, 0:1]
    for h in range(q.shape[1] // XA_DIM):
        lo, hi = h * XA_DIM, (h + 1) * XA_DIM
        s = _dot_nt(q[:, lo:hi].astype(BF16), k_ref[:, lo:hi]) * (XA_DIM ** -0.5)
        p = jnp.exp(s - jnp.max(s, axis=-1, keepdims=True))
        p = p / jnp.sum(p, axis=-1, keepdims=True)
        o_ref[:, lo:hi] = _dot(p.astype(BF16), v_ref[:, lo:hi]).astype(o_ref.dtype)


def _xattn(act, w_q, layer, kv, bm=1024, bn=512):
    x, rs = act
    m, k = x.shape
    n_mem = kv.shape[0]
    nb = XA_W // bn
    return pl.pallas_call(
        _xattn_kernel,
        out_shape=jax.ShapeDtypeStruct((m, XA_W), BF16),
        grid=(nb, m // bm),
        in_specs=[pl.BlockSpec((bm, k), lambda j, i: (i, 0)),
                  pl.BlockSpec((bm, ROW_SCALE_W), lambda j, i: (i, 0)),
                  pl.BlockSpec((None, k, bn), lambda j, i: (layer, 0, j)),
                  pl.BlockSpec((n_mem, bn), lambda j, i: (0, j)),
                  pl.BlockSpec((n_mem, bn), lambda j, i: (0, nb + j))],
        out_specs=pl.BlockSpec((bm, bn), lambda j, i: (i, j)),
        scratch_shapes=[pltpu.VMEM((k, bn), BF16)],
        compiler_params=_cparams("arbitrary", "arbitrary"),
        name="xattn",
    )(x, rs, w_q, kv, kv)


def kernel(x, mem, positions, norm_g, mem_norm_g, ffn1_w_in, ffn1_w_out, mix_w_in, gla_w_gk, gla_b_gk,
           gla_norm_g, ssm_conv_w, ssm_conv_b, ssm_dt_bias, ssm_a_log, ssm_d, ssm_norm_g, ret_norm_g,
           gate_w_up, gate_b, mix_w_out, xattn_w_q, xattn_w_kv, xattn_w_o, ffn2_w_in, ffn2_w_out):
    b, s, d = x.shape
    assert b == 1 and d == D_MODEL and s % max(RET_BLOCK, MIX_ROWS, 1024) == 0
    nl = DEPTH
    xr = x.reshape(s, d)
    memr = mem.reshape(mem.shape[1], d)
    cos_t, sin_t = _rope_tables(positions)
    ret_tabs = _ret_tables(min(RET_BLOCK, s))
    expand = _ssd_expand_table()

    xa_out = _cast_bf16(xattn_w_o.reshape(nl * XA_W, d), 512).reshape(nl, XA_W, d)
    mix_w_out3 = mix_w_out.reshape(nl, N_BRANCH * BRANCH_W, d)
    gate_up = _cast_bf16(jnp.transpose(gate_w_up, (0, 2, 1, 3)).reshape(nl * N_BRANCH * GATE_RANK, d),
                         512).reshape(nl * N_BRANCH, GATE_RANK, d)
    mix_w_in_t = jnp.transpose(mix_w_in, (0, 2, 1))
    mix_in_big = _repack_mix_w_in(mix_w_in_t)
    mix_in_side_t = _side_mix_w_in(mix_w_in_t)
    w_gk_pad = jnp.concatenate([gla_w_gk, jnp.zeros((nl, SMALL_W - GLA_RANK, GLA_QK), F32)], axis=1)

    h = (_norm_cast(xr, norm_g[0, 0]), None)
    for l in range(nl):
        ng = norm_g[l]
        a, w_out, mix_out = _swiglu_in(h, ffn1_w_in, l, (ffn1_w_out, mix_w_out3))
        mix_out = mix_out.reshape(N_BRANCH, BRANCH_W, d)
        xr, h = _out_proj_post(a, w_out, 0, xr, ng[1], 0.5, ng[2], "ffn_out_post")
        proj, small = _mix_in(h, mix_in_big, mix_in_side_t, l)
        y_gla = _gla(proj, small, w_gk_pad[l], gla_b_gk[l], gla_norm_g[l])
        y_ssd = _ssd(proj, small, expand, ssm_conv_w[l], ssm_conv_b[l], ssm_dt_bias[l], ssm_a_log[l],
                     ssm_d[l], ssm_norm_g[l])
        y_ret = _retention(proj, cos_t, sin_t, ret_tabs, ret_norm_g[l])
        xr, h = _merge_post(proj, (y_gla, y_ssd, y_ret), gate_up, gate_b, mix_out, l, 0, xr, ng[3], ng[4])
        mem_n = _norm_cast(memr, mem_norm_g[l])
        kv = _matmul_wcast(mem_n, xattn_w_kv, l, BF16, 512, "xattn_kv")
        o = _xattn(h, xattn_w_q, l, kv)
        xr, h = _out_proj_post(o, xa_out, l, xr, ng[5], 1.0, ng[6], "xattn_out_post", bn=1024)
        a, w_out = _swiglu_in(h, ffn2_w_in, l, (ffn2_w_out,))
        g_next = norm_g[l + 1, 0] if l + 1 < nl else None
        xr, h = _out_proj_post(a, w_out, 0, xr, ng[7], 0.5, g_next, "ffn_out_post")
    return xr.reshape(b, s, d)
```

```python
import functools

import numpy as np
import jax
import jax.numpy as jnp
from jax import lax
from jax.experimental import pallas as pl
from jax.experimental.pallas import tpu as pltpu

F32 = jnp.float32
BF16 = jnp.bfloat16

D_MODEL = 4096
DEPTH = 4
CHUNK = 64
D_FF = 4096
EPS = 1e-6
GLA_HEADS, GLA_DK, GLA_DV, GLA_RANK, GLA_TAU = 4, 256, 512, 16, 16.0
GLA_QK, GLA_V = GLA_HEADS * GLA_DK, GLA_HEADS * GLA_DV
SSM_HEADS, SSM_HEADDIM, SSM_GROUPS, SSM_STATE, SSM_CONV = 32, 64, 4, 128, 4
SSM_INNER = SSM_HEADS * SSM_HEADDIM
SSM_BC = SSM_GROUPS * SSM_STATE
SSM_CONV_DIM = SSM_INNER + 2 * SSM_BC
SSM_HPG = SSM_HEADS // SSM_GROUPS
SSM_GW = SSM_INNER // SSM_GROUPS
RET_HEADS, RET_DK, RET_DV = 8, 128, 256
RET_QK, RET_V = RET_HEADS * RET_DK, RET_HEADS * RET_DV
ROPE_BASE = 10000.0
XA_HEADS, XA_DIM = 4, 256
XA_W = XA_HEADS * XA_DIM
N_BRANCH, BRANCH_W, GATE_RANK = 3, 2048, 512

_IN_WIDTHS = (GLA_QK, GLA_QK, GLA_V, GLA_V, GLA_RANK, SSM_INNER, SSM_CONV_DIM, SSM_HEADS,
              RET_QK, RET_QK, RET_V, RET_V, GATE_RANK)
_IN_OFF = np.concatenate([[0], np.cumsum(_IN_WIDTHS)]).tolist()
IN_COLS = _IN_OFF[-1]
P_GLA_Q, P_GLA_K, P_GLA_V, P_GLA_R = 0, 1024, 2048, 4096
P_SSM_Z, P_SSM_X, P_SSM_B, P_SSM_C = 6144, 8192, 10240, 10752
P_RET_Q, P_RET_K, P_RET_V, P_RET_G = 11264, 12288, 13312, 15360
P_GATE = 17408
P_COLS = 17920
SMALL_W = 128
SMALL_DT0 = GLA_RANK

V7X_VMEM_LIMIT = 60 * 1024 * 1024
V7X_SUBLANES = 8
CHUNK_SHIFT = CHUNK.bit_length() - 1
HEAD_SHIFT = SSM_HEADDIM.bit_length() - 1

RET_BLOCK = 512
MIX_ROWS = 512
CUM_BLOCK = 256
GLA_HEADS_PER_STEP = 4
SSD_GROUPS_PER_STEP = 4
RET_HEADS_PER_STEP = 4


def _block_index(col_offset, block_cols):
    assert col_offset % block_cols == 0, (col_offset, block_cols)
    return col_offset // block_cols


def _cparams(*sem):
    return pltpu.CompilerParams(dimension_semantics=sem, vmem_limit_bytes=V7X_VMEM_LIMIT)


def _silu(x):
    return x * jax.nn.sigmoid(x)


def _dot(a, b):
    return jnp.dot(a, b, preferred_element_type=F32)


def _dot_nt(a, b):
    return lax.dot_general(a, b, (((1,), (1,)), ((), ())), preferred_element_type=F32)


def _dot_tn(a, b):
    return lax.dot_general(a, b, (((0,), (0,)), ((), ())), preferred_element_type=F32)


def _split3(x):
    hi = x.astype(BF16)
    r1 = x - hi.astype(F32)
    mid = r1.astype(BF16)
    lo = (r1 - mid.astype(F32)).astype(BF16)
    return hi, mid, lo


def _sel_dot(sel, x):
    hi, mid, lo = _split3(x)
    return _dot(sel, hi) + _dot(sel, mid) + _dot(sel, lo)


def _sel_dot_rhs(x, sel):
    hi, mid, lo = _split3(x)
    return _dot(hi, sel) + _dot(mid, sel) + _dot(lo, sel)


def _chunk_tril(n):
    ri = lax.broadcasted_iota(jnp.int32, (n, n), 0)
    ci = lax.broadcasted_iota(jnp.int32, (n, n), 1)
    same = jnp.right_shift(ri, CHUNK_SHIFT) == jnp.right_shift(ci, CHUNK_SHIFT)
    return jnp.where(same & (ri >= ci), 1.0, 0.0).astype(BF16)


def _cast_kernel(w_ref, o_ref):
    o_ref[...] = w_ref[...].astype(o_ref.dtype)


def _cast_bf16(w, rows):
    r, c = w.shape
    return pl.pallas_call(
        _cast_kernel,
        out_shape=jax.ShapeDtypeStruct((r, c), BF16),
        grid=(r // rows,),
        in_specs=[pl.BlockSpec((rows, c), lambda i: (i, 0))],
        out_specs=pl.BlockSpec((rows, c), lambda i: (i, 0)),
        compiler_params=_cparams("parallel"),
        name="cast_bf16",
    )(w)


REPACK_COLS = 512


def _packed_to_source_col(c):
    skipped = jnp.where(c >= P_SSM_Z, GLA_RANK, 0) + jnp.where(c >= P_RET_Q, SSM_HEADS, 0)
    return pl.multiple_of(c + skipped, GLA_RANK)


def _repack_kernel(w_ref, o_ref):
    o_ref[...] = w_ref[...].T.astype(o_ref.dtype)


def _repack_mix_w_in(w_t):
    nl, _, d = w_t.shape
    return pl.pallas_call(
        _repack_kernel,
        out_shape=jax.ShapeDtypeStruct((nl, d, P_COLS), BF16),
        grid=(nl, P_COLS // REPACK_COLS),
        in_specs=[pl.BlockSpec((pl.Squeezed(), pl.Element(REPACK_COLS), pl.Element(d)),
                               lambda l, j: (l, _packed_to_source_col(j * REPACK_COLS), 0))],
        out_specs=pl.BlockSpec((None, d, REPACK_COLS), lambda l, j: (l, 0, j)),
        compiler_params=_cparams("parallel", "parallel"),
        name="repack_mix_w_in",
    )(w_t)


def _side_w_kernel(gk_ref, dt_ref, o_ref):
    o_ref[...] = jnp.zeros(o_ref.shape, o_ref.dtype)
    o_ref[0:GLA_RANK, :] = gk_ref[...].astype(o_ref.dtype)
    o_ref[GLA_RANK:GLA_RANK + SSM_HEADS, :] = dt_ref[...].astype(o_ref.dtype)


def _side_mix_w_in(w_t):
    nl, _, d = w_t.shape
    o = _IN_OFF
    return pl.pallas_call(
        _side_w_kernel,
        out_shape=jax.ShapeDtypeStruct((nl, SMALL_W, d), BF16),
        grid=(nl,),
        in_specs=[pl.BlockSpec((pl.Squeezed(), pl.Element(GLA_RANK), pl.Element(d)), lambda l: (l, o[4], 0)),
                  pl.BlockSpec((pl.Squeezed(), pl.Element(SSM_HEADS), pl.Element(d)), lambda l: (l, o[7], 0))],
        out_specs=pl.BlockSpec((None, SMALL_W, d), lambda l: (l, 0, 0)),
        compiler_params=_cparams("parallel"),
        name="side_mix_w_in",
    )(w_t, w_t)


def _norm_cast_kernel(x_ref, g_ref, o_ref):
    x = x_ref[...]
    r = lax.rsqrt(jnp.mean(x * x, axis=-1, keepdims=True) + EPS)
    o_ref[...] = (x * r * g_ref[...]).astype(o_ref.dtype)


def _norm_cast(x, g, rows=256):
    m, d = x.shape
    rows = min(rows, m)
    return pl.pallas_call(
        _norm_cast_kernel,
        out_shape=jax.ShapeDtypeStruct((m, d), BF16),
        grid=(m // rows,),
        in_specs=[pl.BlockSpec((rows, d), lambda i: (i, 0)), pl.BlockSpec((1, d), lambda i: (0, 0))],
        out_specs=pl.BlockSpec((rows, d), lambda i: (i, 0)),
        compiler_params=_cparams("parallel"),
        name="norm_cast",
    )(x, g.reshape(1, d))


ROW_SCALE_W = 128


def _mix_in_kernel(x_ref, rs_ref, w_ref, wside_ref, o_ref, side_ref):
    x = x_ref[...]
    rs = rs_ref[:, 0:1]
    o_ref[...] = (_dot(x, w_ref[...]) * rs).astype(o_ref.dtype)

    @pl.when(pl.program_id(1) == 0)
    def _():
        side_ref[...] = _dot_nt(x, wside_ref[...]) * rs


def _mix_in(act, w, w_side_t, layer, bm=1024, bn=1280):
    x, rs = act
    m, k = x.shape
    n = w.shape[2]
    return pl.pallas_call(
        _mix_in_kernel,
        out_shape=[jax.ShapeDtypeStruct((m, n), BF16), jax.ShapeDtypeStruct((m, SMALL_W), F32)],
        grid=(m // bm, n // bn),
        in_specs=[pl.BlockSpec((bm, k), lambda i, j: (i, 0)),
                  pl.BlockSpec((bm, ROW_SCALE_W), lambda i, j: (i, 0)),
                  pl.BlockSpec((None, k, bn), lambda i, j: (layer, 0, j)),
                  pl.BlockSpec((None, SMALL_W, k), lambda i, j: (layer, 0, 0))],
        out_specs=[pl.BlockSpec((bm, bn), lambda i, j: (i, j)),
                   pl.BlockSpec((bm, SMALL_W), lambda i, j: (i, 0))],
        compiler_params=_cparams("parallel", "arbitrary"),
        name="mix_in",
    )(x, rs, w, w_side_t)


def _mm_wcast_kernel(x_ref, w_ref, o_ref):
    o_ref[...] = _dot(x_ref[...], w_ref[...].astype(BF16)).astype(o_ref.dtype)


def _matmul_wcast(x, w, layer, out_dtype, bn, name):
    m, k = x.shape
    n = w.shape[2]
    return pl.pallas_call(
        _mm_wcast_kernel,
        out_shape=jax.ShapeDtypeStruct((m, n), out_dtype),
        grid=(n // bn,),
        in_specs=[pl.BlockSpec((m, k), lambda j: (0, 0)),
                  pl.BlockSpec((None, k, bn), lambda j: (layer, 0, j))],
        out_specs=pl.BlockSpec((m, bn), lambda j: (0, j)),
        compiler_params=_cparams("parallel"),
        name=name,
    )(x, w)


def _swiglu_kernel(layer, f, bn, scaled, n_side, x_ref, *refs):
    first_w = 1 if scaled else 0
    w_hbm = refs[first_w]
    side_refs = refs[first_w + 1:first_w + 1 + n_side]
    o_ref = refs[first_w + 1 + n_side]
    side_o_refs = refs[first_w + 2 + n_side:first_w + 2 + 2 * n_side]
    w_bf, stage, sem = refs[-3:]
    j, i = pl.program_id(0), pl.program_id(1)
    nb, n_slices = pl.num_programs(0), pl.num_programs(1)
    ks = stage.shape[1]
    cur = lax.rem(j, 2)

    def slice_copy(col_block, k_slice, part):
        col = pl.multiple_of(part * f + col_block * bn, bn)
        row = pl.multiple_of(k_slice * ks, ks)
        return pltpu.make_async_copy(w_hbm.at[layer, pl.ds(row, ks), pl.ds(col, bn)], stage.at[part], sem.at[part])

    def load_slice(col_block, k_slice, slot):
        for part in range(2):
            slice_copy(col_block, k_slice, part).wait()
            w_bf[slot, pl.ds(pl.multiple_of(k_slice * ks, ks), ks), part * bn:(part + 1) * bn] = (
                stage[part].astype(BF16))

    @pl.when((j == 0) & (i == 0))
    def _():
        for s in range(w_bf.shape[1] // ks):
            for part in range(2):
                slice_copy(0, s, part).start()
            load_slice(0, s, 0)

    has_next = j + 1 < nb

    @pl.when(has_next)
    def _():
        for part in range(2):
            slice_copy(j + 1, i, part).start()

    r = _dot(x_ref[...], w_bf[cur])
    if scaled:
        r = r * refs[0][:, 0:1]
    o_ref[...] = (_silu(r[:, 0:bn]) * r[:, bn:2 * bn]).astype(o_ref.dtype)
    for s_ref, so_ref in zip(side_refs, side_o_refs):
        so_ref[...] = s_ref[...].astype(so_ref.dtype)

    @pl.when(has_next)
    def _():
        load_slice(j + 1, i, 1 - cur)


def _swiglu_in(act, w, layer, side_ws, bm=1024, bn=512):
    x, rs = act
    m, k = x.shape
    f = w.shape[2] // 2
    nb, nmb = f // bn, m // bm
    assert k % nmb == 0
    steps = nb * nmb
    scaled = rs is not None
    in_specs = [pl.BlockSpec((bm, k), lambda j, i: (i, 0))]
    if scaled:
        in_specs.append(pl.BlockSpec((bm, ROW_SCALE_W), lambda j, i: (i, 0)))
    in_specs.append(pl.BlockSpec(memory_space=pl.ANY))
    out_shape = [jax.ShapeDtypeStruct((m, f), BF16)]
    out_specs = [pl.BlockSpec((bm, bn), lambda j, i: (i, j))]
    for sw in side_ws:
        _, r_side, c_side = sw.shape
        slab = r_side // steps
        in_specs.append(pl.BlockSpec((None, slab, c_side), lambda j, i: (layer, j * nmb + i, 0)))
        out_shape.append(jax.ShapeDtypeStruct((1, r_side, c_side), BF16))
        out_specs.append(pl.BlockSpec((None, slab, c_side), lambda j, i: (0, j * nmb + i, 0)))
    args = ((x, rs) if scaled else (x,)) + (w,) + tuple(side_ws)
    return pl.pallas_call(
        functools.partial(_swiglu_kernel, layer, f, bn, scaled, len(side_ws)),
        out_shape=out_shape,
        grid=(nb, nmb),
        in_specs=in_specs,
        out_specs=out_specs,
        scratch_shapes=[pltpu.VMEM((2, k, 2 * bn), BF16),
                        pltpu.VMEM((2, k // nmb, bn), F32),
                        pltpu.SemaphoreType.DMA((2,))],
        compiler_params=_cparams("arbitrary", "arbitrary"),
        name="swiglu_in",
    )(*args)


def _plain_out_proj(lhs_ref, w_ref):
    return _dot(lhs_ref[...], w_ref[...])


def _out_proj_kernel(nm, scale, emit_h, proj_fn, n_proj, *refs):
    proj_refs = refs[:n_proj]
    x_ref, gp_ref, gn_ref, xo_ref = refs[n_proj:n_proj + 4]
    rest = refs[n_proj + 4:]
    if emit_h:
        ho_ref, rs_ref, y_scr, ssq_scr, ssq2_scr = rest
    else:
        y_scr, ssq_scr = rest
    i, n = pl.program_id(0), pl.program_id(1)
    cur = lax.rem(i, 2)
    prev = 1 - cur
    inv_d = 1.0 / (y_scr.shape[0] * y_scr.shape[2])

    def matmul_part():
        yb = proj_fn(*proj_refs)
        y_scr[n] = yb.astype(BF16)
        part = jnp.sum(yb * yb, axis=-1, keepdims=True)
        ssq_scr[cur] = jnp.where(n == 0, part, ssq_scr[cur] + part)

    def residual_part():
        r = lax.rsqrt(ssq_scr[prev] * inv_d + EPS)
        xn = x_ref[...] + scale * (y_scr[n].astype(F32) * r * gp_ref[...])
        xo_ref[...] = xn
        if emit_h:
            ho_ref[...] = (xn * gn_ref[...]).astype(BF16)
            part = jnp.sum(xn * xn, axis=-1, keepdims=True)
            acc = jnp.where(n == 0, part, ssq2_scr[...] + part)
            ssq2_scr[...] = acc
            rs_ref[...] = jnp.broadcast_to(lax.rsqrt(acc * inv_d + EPS), rs_ref.shape)

    @pl.when(i == 0)
    def _():
        matmul_part()

    @pl.when((i > 0) & (i < nm))
    def _():
        residual_part()
        matmul_part()

    @pl.when(i == nm)
    def _():
        residual_part()


def _out_proj_post(lhs, w, layer, x, g_post, scale, g_next, name, bm=1024, bn=512):
    m, k = lhs.shape
    nm = m // bm
    proj_specs = [pl.BlockSpec((bm, k), lambda i, n: (jnp.minimum(i, nm - 1), 0)),
                  pl.BlockSpec((None, k, bn), lambda i, n: (layer, 0, n))]
    return _fused_proj_post(_plain_out_proj, (lhs, w), proj_specs, x, g_post, scale, g_next, name, bm, bn)


def _fused_proj_post(proj_fn, proj_args, proj_specs, x, g_post, scale, g_next, name, bm, bn):
    m, d = x.shape
    nm, nn = m // bm, d // bn
    emit_h = g_next is not None
    gn = g_next if emit_h else g_post
    res_map = lambda i, n: (jnp.maximum(i - 1, 0), jnp.where(i == 0, 0, n))
    vec_map = lambda i, n: (0, jnp.where(i == 0, 0, n))
    res_spec = pl.BlockSpec((bm, bn), res_map)
    out_shape = [jax.ShapeDtypeStruct((m, d), F32)]
    out_specs = [res_spec]
    scratch = [pltpu.VMEM((nn, bm, bn), BF16), pltpu.VMEM((2, bm, 1), F32)]
    if emit_h:
        out_shape += [jax.ShapeDtypeStruct((m, d), BF16), jax.ShapeDtypeStruct((m, ROW_SCALE_W), F32)]
        out_specs += [res_spec, pl.BlockSpec((bm, ROW_SCALE_W), lambda i, n: (jnp.maximum(i - 1, 0), 0))]
        scratch.append(pltpu.VMEM((bm, 1), F32))
    res = pl.pallas_call(
        functools.partial(_out_proj_kernel, nm, scale, emit_h, proj_fn, len(proj_args)),
        out_shape=out_shape,
        grid=(nm + 1, nn),
        in_specs=list(proj_specs) + [res_spec, pl.BlockSpec((1, bn), vec_map), pl.BlockSpec((1, bn), vec_map)],
        out_specs=out_specs,
        scratch_shapes=scratch,
        compiler_params=_cparams("arbitrary", "arbitrary"),
        name=name,
    )(*proj_args, x, g_post.reshape(1, d), gn.reshape(1, d))
    return (res[0], (res[1], res[2])) if emit_h else (res[0], None)


def _gla_kernel(q_ref, k_ref, v_ref, r_ref, sm_ref, wgk_ref, bgk_ref, ng_ref, o_ref, cum_ref, *st_refs):
    @pl.when(pl.program_id(1) == 0)
    def _():
        for st_ref in st_refs:
            st_ref[...] = jnp.zeros_like(st_ref)

    rows = q_ref.shape[0]
    a_hi, a_mid, _ = _split3(sm_ref[...])
    w_hi, w_mid, _ = _split3(wgk_ref[...])
    z = _dot(a_hi, w_hi) + _dot(a_hi, w_mid) + _dot(a_mid, w_hi) + bgk_ref[...]
    logf = (jnp.minimum(z, 0.0) - jnp.log1p(jnp.exp(-jnp.abs(z)))) * (1.0 / GLA_TAU)
    tri = _chunk_tril(CUM_BLOCK)
    for b in range(rows // CUM_BLOCK):
        sl = slice(b * CUM_BLOCK, (b + 1) * CUM_BLOCK)
        cum_ref[sl, :] = _sel_dot(tri, logf[sl])

    ri = lax.broadcasted_iota(jnp.int32, (CHUNK, CHUNK), 0)
    ci = lax.broadcasted_iota(jnp.int32, (CHUNK, CHUNK), 1)
    tril = ri >= ci
    for c in range(rows // CHUNK):
        sl = pl.ds(c * CHUNK, CHUNK)
        for hh, st_ref in enumerate(st_refs):
            kl = slice(hh * GLA_DK, (hh + 1) * GLA_DK)
            vl = slice(hh * GLA_DV, (hh + 1) * GLA_DV)
            cum = cum_ref[sl, kl]
            cum_end = cum[CHUNK - 1:CHUNK, :]
            up, down = jnp.exp(cum), jnp.exp(-cum)
            q = q_ref[sl, kl].astype(F32) * (GLA_DK ** -0.5)
            k = k_ref[sl, kl].astype(F32)
            v = v_ref[sl, vl]
            q_dec = (q * up).astype(BF16)
            a_past = _dot_nt(q_dec, (k * down).astype(BF16))
            a_future = _dot_nt((q * down).astype(BF16), (k * up).astype(BF16))
            attn = jnp.where(tril, a_past, a_future).astype(BF16)
            state_t = st_ref[...]
            o = _dot(attn, v) + _dot_nt(q_dec, state_t.astype(BF16))
            contrib_t = _dot_tn(v, (k * jnp.exp(cum_end - cum)).astype(BF16))
            st_ref[...] = state_t * jnp.exp(cum_end) + contrib_t
            o = o * lax.rsqrt(jnp.mean(o * o, axis=-1, keepdims=True) + EPS) * ng_ref[:, vl]
            o_ref[sl, vl] = (o * _silu(r_ref[sl, vl].astype(F32))).astype(o_ref.dtype)


def _gla(proj, small, w_gk_pad, b_gk, norm_g):
    s = proj.shape[0]
    rows = min(MIX_ROWS, s)
    hp = GLA_HEADS_PER_STEP
    kw, vw = hp * GLA_DK, hp * GLA_DV
    qb, vb = _block_index(P_GLA_Q, kw), _block_index(P_GLA_V, vw)
    kb, rb = _block_index(P_GLA_K, kw), _block_index(P_GLA_R, vw)
    return pl.pallas_call(
        _gla_kernel,
        out_shape=jax.ShapeDtypeStruct((s, GLA_V), BF16),
        grid=(GLA_HEADS // hp, s // rows),
        in_specs=[pl.BlockSpec((rows, kw), lambda h, i: (i, qb + h)),
                  pl.BlockSpec((rows, kw), lambda h, i: (i, kb + h)),
                  pl.BlockSpec((rows, vw), lambda h, i: (i, vb + h)),
                  pl.BlockSpec((rows, vw), lambda h, i: (i, rb + h)),
                  pl.BlockSpec((rows, SMALL_W), lambda h, i: (i, 0)),
                  pl.BlockSpec((SMALL_W, kw), lambda h, i: (0, h)),
                  pl.BlockSpec((1, kw), lambda h, i: (0, h)),
                  pl.BlockSpec((1, vw), lambda h, i: (0, h))],
        out_specs=pl.BlockSpec((rows, vw), lambda h, i: (i, h)),
        scratch_shapes=[pltpu.VMEM((rows, kw), F32)] + [pltpu.VMEM((GLA_DV, GLA_DK), F32)] * hp,
        compiler_params=_cparams("parallel", "arbitrary"),
        name="gla",
    )(proj, proj, proj, proj, small, w_gk_pad, b_gk.reshape(1, GLA_QK), norm_g.reshape(1, GLA_V))


def _causal_conv_silu(xp_ref, x_ref, w_ref, b_ref, first):
    rows = x_ref.shape[0]
    hist = V7X_SUBLANES

    @pl.when(first)
    def _():
        xp_ref[pl.ds(0, hist), :] = jnp.zeros((hist, xp_ref.shape[1]), F32)

    xp_ref[pl.ds(hist, rows), :] = x_ref[...].astype(F32)
    acc = b_ref[...]
    for tap in range(SSM_CONV):
        acc = acc + xp_ref[pl.ds(hist - (SSM_CONV - 1) + tap, rows), :] * w_ref[tap:tap + 1, :]
    xp_ref[pl.ds(0, hist), :] = xp_ref[pl.ds(rows, hist), :]
    return _silu(acc)


def _ssd_kernel(z_ref, xs_ref, b_ref, c_ref, sm_ref, e_ref, cwx_ref, cwb_ref, cwc_ref,
                cbx_ref, cbb_ref, cbc_ref, dtb_ref, alog_ref, dsk_ref, ng_ref,
                o_ref, xpx_ref, xpb_ref, xpc_ref, xs_scr, b_scr, c_scr, dt_scr, cum_scr, *s_refs):
    first = pl.program_id(1) == 0

    @pl.when(first)
    def _():
        for s_ref in s_refs:
            s_ref[...] = jnp.zeros_like(s_ref)

    rows = xs_ref.shape[0]
    xs_scr[...] = _causal_conv_silu(xpx_ref, xs_ref, cwx_ref, cbx_ref, first)
    b_scr[...] = _causal_conv_silu(xpb_ref, b_ref, cwb_ref, cbb_ref, first).astype(BF16)
    c_scr[...] = _causal_conv_silu(xpc_ref, c_ref, cwc_ref, cbc_ref, first).astype(BF16)

    dt_in = sm_ref[...] + dtb_ref[...]
    dt_c = jnp.maximum(dt_in, 0.0) + jnp.log1p(jnp.exp(-jnp.abs(dt_in)))
    da_c = dt_c * (-jnp.exp(alog_ref[...]))
    tri = _chunk_tril(CUM_BLOCK)
    cum_c = jnp.concatenate([_sel_dot(tri, da_c[b * CUM_BLOCK:(b + 1) * CUM_BLOCK])
                             for b in range(rows // CUM_BLOCK)], axis=0)
    dt_scr[...] = _sel_dot_rhs(dt_c, e_ref[0])
    cum_scr[...] = _sel_dot_rhs(cum_c, e_ref[0])

    gw = SSM_GW
    rw = lax.broadcasted_iota(jnp.int32, (CHUNK, gw), 0)
    cw = lax.broadcasted_iota(jnp.int32, (CHUNK, gw), 1)
    diag_tile = (jnp.bitwise_and(cw, SSM_HEADDIM - 1) == rw).astype(F32)
    half = gw // 2
    rb = lax.broadcasted_iota(jnp.int32, (half, half), 0)
    cb_ = lax.broadcasted_iota(jnp.int32, (half, half), 1)
    head_mask = jnp.right_shift(rb, HEAD_SHIFT) == jnp.right_shift(cb_, HEAD_SHIFT)
    reps = half // CHUNK
    for c in range(rows // CHUNK):
        sl = pl.ds(c * CHUNK, CHUNK)
        for gg, s_ref in enumerate(s_refs):
            xl = slice(gg * gw, (gg + 1) * gw)
            nl = slice(gg * SSM_STATE, (gg + 1) * SSM_STATE)
            cum = cum_scr[sl, xl]
            cum_row = jnp.sum(cum * diag_tile, axis=0, keepdims=True)
            seg = jnp.exp(-jnp.abs(cum - cum_row))
            xs = xs_scr[sl, xl]
            bm = b_scr[sl, nl]
            cm = c_scr[sl, nl]
            xdt = xs * dt_scr[sl, xl]
            cbt = _dot_nt(cm, jnp.concatenate([bm] * SSM_HPG, axis=0))
            m_all = (cbt * seg).astype(BF16)
            intra = []
            for hlf in range(2):
                xh = xdt[:, hlf * half:(hlf + 1) * half]
                bd = jnp.where(head_mask, jnp.concatenate([xh] * reps, axis=0), 0.0).astype(BF16)
                intra.append(_dot(m_all[:, hlf * half:(hlf + 1) * half], bd))
            intra = jnp.concatenate(intra, axis=1)
            cum_end = cum[CHUNK - 1:CHUNK, :]
            state = s_ref[...]
            inter = _dot(cm, state.astype(BF16)) * jnp.exp(cum)
            contrib = _dot_tn(bm, (xdt * jnp.exp(cum_end - cum)).astype(BF16))
            s_ref[...] = state * jnp.exp(cum_end) + contrib
            y = intra + inter + xs * dsk_ref[:, xl]
            y = y * _silu(z_ref[sl, xl].astype(F32))
            y = y * lax.rsqrt(jnp.mean(y * y, axis=-1, keepdims=True) + EPS) * ng_ref[:, xl]
            o_ref[sl, xl] = y.astype(o_ref.dtype)


def _ssd(proj, small, expand, conv_w, conv_b, dt_bias, a_log, d_skip, norm_g):
    s = proj.shape[0]
    rows = min(MIX_ROWS, s)
    gp = SSD_GROUPS_PER_STEP
    xw, nw = gp * SSM_GW, gp * SSM_STATE
    zb, xb = _block_index(P_SSM_Z, xw), _block_index(P_SSM_X, xw)
    bb, cb = _block_index(P_SSM_B, nw), _block_index(P_SSM_C, nw)
    cwb, cwc = _block_index(SSM_INNER, nw), _block_index(SSM_INNER + SSM_BC, nw)

    def lane_expand(p):
        return jnp.repeat(p.astype(F32), SSM_HEADDIM).reshape(1, SSM_INNER)

    def side_lanes(p):
        return jnp.pad(p.astype(F32), (SMALL_DT0, SMALL_W - SMALL_DT0 - SSM_HEADS)).reshape(1, SMALL_W)

    conv_b2 = conv_b.reshape(1, SSM_CONV_DIM)
    vec = pl.BlockSpec((1, xw), lambda g, i: (0, g))
    side_vec = pl.BlockSpec((1, SMALL_W), lambda g, i: (0, 0))
    return pl.pallas_call(
        _ssd_kernel,
        out_shape=jax.ShapeDtypeStruct((s, SSM_INNER), BF16),
        grid=(SSM_GROUPS // gp, s // rows),
        in_specs=[pl.BlockSpec((rows, xw), lambda g, i: (i, zb + g)),
                  pl.BlockSpec((rows, xw), lambda g, i: (i, xb + g)),
                  pl.BlockSpec((rows, nw), lambda g, i: (i, bb + g)),
                  pl.BlockSpec((rows, nw), lambda g, i: (i, cb + g)),
                  pl.BlockSpec((rows, SMALL_W), lambda g, i: (i, 0)),
                  pl.BlockSpec((1, SMALL_W, xw), lambda g, i: (g, 0, 0)),
                  pl.BlockSpec((SSM_CONV, xw), lambda g, i: (0, g)),
                  pl.BlockSpec((SSM_CONV, nw), lambda g, i: (0, cwb + g)),
                  pl.BlockSpec((SSM_CONV, nw), lambda g, i: (0, cwc + g)),
                  pl.BlockSpec((1, xw), lambda g, i: (0, g)),
                  pl.BlockSpec((1, nw), lambda g, i: (0, cwb + g)),
                  pl.BlockSpec((1, nw), lambda g, i: (0, cwc + g)),
                  side_vec, side_vec, vec, vec],
        out_specs=pl.BlockSpec((rows, xw), lambda g, i: (i, g)),
        scratch_shapes=[pltpu.VMEM((rows + V7X_SUBLANES, xw), F32),
                        pltpu.VMEM((rows + V7X_SUBLANES, nw), F32),
                        pltpu.VMEM((rows + V7X_SUBLANES, nw), F32),
                        pltpu.VMEM((rows, xw), F32),
                        pltpu.VMEM((rows, nw), BF16),
                        pltpu.VMEM((rows, nw), BF16),
                        pltpu.VMEM((rows, xw), F32),
                        pltpu.VMEM((rows, xw), F32)] + [pltpu.VMEM((SSM_STATE, SSM_GW), F32)] * gp,
        compiler_params=_cparams("parallel", "arbitrary"),
        name="ssd",
    )(proj, proj, proj, proj, small, expand, conv_w, conv_w, conv_w, conv_b2, conv_b2, conv_b2,
      side_lanes(dt_bias), side_lanes(a_log), lane_expand(d_skip), norm_g.reshape(1, SSM_INNER))


def _ssd_expand_table():
    xw = SSD_GROUPS_PER_STEP * SSM_GW
    rows = jnp.arange(SMALL_W, dtype=jnp.int32)[None, :, None]
    cols = jnp.arange(xw, dtype=jnp.int32)[None, None, :]
    step = jnp.arange(SSM_INNER // xw, dtype=jnp.int32)[:, None, None]
    return (rows == SMALL_DT0 + (step * xw + cols) // SSM_HEADDIM).astype(BF16)


def _rope_table_kernel(pos_ref, freq_ref, sign_ref, cos_ref, sin_ref):
    ang = pos_ref[...] * freq_ref[...]
    cos_ref[...] = jnp.cos(ang)
    sin_ref[...] = jnp.sin(ang) * sign_ref[...]


def _rope_tables(positions):
    s = positions.shape[-1]
    half = RET_DK // 2
    inv_freq = ROPE_BASE ** (-jnp.arange(half, dtype=F32) / half)
    freq = jnp.concatenate([inv_freq, inv_freq]).reshape(1, RET_DK)
    sign = jnp.concatenate([-jnp.ones((half,), F32), jnp.ones((half,), F32)]).reshape(1, RET_DK)
    pos = positions.astype(F32).reshape(s, 1)
    rows = min(1024, s)
    tab = jax.ShapeDtypeStruct((s, RET_DK), F32)
    return pl.pallas_call(
        _rope_table_kernel,
        out_shape=[tab, tab],
        grid=(s // rows,),
        in_specs=[pl.BlockSpec((rows, 1), lambda i: (i, 0)),
                  pl.BlockSpec((1, RET_DK), lambda i: (0, 0)),
                  pl.BlockSpec((1, RET_DK), lambda i: (0, 0))],
        out_specs=[pl.BlockSpec((rows, RET_DK), lambda i: (i, 0))] * 2,
        compiler_params=_cparams("parallel"),
        name="rope_tables",
    )(pos, freq, sign)


def _ret_kernel(q_ref, k_ref, v_ref, g_ref, cos_ref, sin_ref, dmat_ref, qdec_ref, kdec_ref, sdec_ref,
                ng_ref, o_ref, *s_refs):
    @pl.when(pl.program_id(1) == 0)
    def _():
        for s_ref in s_refs:
            s_ref[...] = jnp.zeros_like(s_ref)

    cos, sin = cos_ref[...], sin_ref[...]

    def rot(t):
        t = t.astype(F32)
        return t * cos + pltpu.roll(t, RET_DK // 2, axis=1) * sin

    for hh, s_ref in enumerate(s_refs):
        kl = slice(hh * RET_DK, (hh + 1) * RET_DK)
        vl = slice(hh * RET_DV, (hh + 1) * RET_DV)
        q = rot(q_ref[:, kl])
        k = rot(k_ref[:, kl]) * (RET_DK ** -0.5)
        v = v_ref[:, vl]
        scores = (_dot_nt(q.astype(BF16), k.astype(BF16)) * dmat_ref[hh]).astype(BF16)
        state = s_ref[...]
        o = _dot(scores, v) + _dot((q * qdec_ref[hh]).astype(BF16), state.astype(BF16))
        s_ref[...] = state * sdec_ref[hh] + _dot_tn((k * kdec_ref[hh]).astype(BF16), v)
        o = o * lax.rsqrt(jnp.mean(o * o, axis=-1, keepdims=True) + EPS) * ng_ref[:, vl]
        o_ref[:, vl] = (o * _silu(g_ref[:, vl].astype(F32))).astype(o_ref.dtype)


def _ret_tables(rows):
    log_gamma = jnp.log1p(-jnp.exp2(-5.0 - jnp.arange(RET_HEADS, dtype=F32)))[:, None, None]
    i = jnp.arange(rows, dtype=jnp.int32)
    dist = (i[:, None] - i[None, :]).astype(F32)[None]
    ch_i, ch_j = (i // CHUNK)[:, None], (i // CHUNK)[None, :]
    dmat = jnp.where((ch_i == ch_j)[None], jnp.exp(log_gamma * jnp.abs(dist)),
                     jnp.where((ch_j < ch_i)[None], jnp.exp(log_gamma * dist), 0.0))
    fi = i.astype(F32)[None, :, None]
    qdec = jnp.broadcast_to(jnp.exp(log_gamma * (fi + 1.0)), (RET_HEADS, rows, RET_DK))
    kdec = jnp.broadcast_to(jnp.exp(log_gamma * (rows - 1.0 - fi)), (RET_HEADS, rows, RET_DK))
    sdec = jnp.broadcast_to(jnp.exp(log_gamma * float(rows)), (RET_HEADS, 1, RET_DV))
    return dmat, qdec, kdec, sdec


def _retention(proj, cos_t, sin_t, tables, norm_g):
    s = proj.shape[0]
    rows = min(RET_BLOCK, s)
    dmat, qdec, kdec, sdec = tables
    hp = RET_HEADS_PER_STEP
    kw, vw = hp * RET_DK, hp * RET_DV
    qb, kb = _block_index(P_RET_Q, kw), _block_index(P_RET_K, kw)
    vb, gb = _block_index(P_RET_V, vw), _block_index(P_RET_G, vw)
    return pl.pallas_call(
        _ret_kernel,
        out_shape=jax.ShapeDtypeStruct((s, RET_V), BF16),
        grid=(RET_HEADS // hp, s // rows),
        in_specs=[pl.BlockSpec((rows, kw), lambda h, i: (i, qb + h)),
                  pl.BlockSpec((rows, kw), lambda h, i: (i, kb + h)),
                  pl.BlockSpec((rows, vw), lambda h, i: (i, vb + h)),
                  pl.BlockSpec((rows, vw), lambda h, i: (i, gb + h)),
                  pl.BlockSpec((rows, RET_DK), lambda h, i: (i, 0)),
                  pl.BlockSpec((rows, RET_DK), lambda h, i: (i, 0)),
                  pl.BlockSpec((hp, rows, rows), lambda h, i: (h, 0, 0)),
                  pl.BlockSpec((hp, rows, RET_DK), lambda h, i: (h, 0, 0)),
                  pl.BlockSpec((hp, rows, RET_DK), lambda h, i: (h, 0, 0)),
                  pl.BlockSpec((hp, 1, RET_DV), lambda h, i: (h, 0, 0)),
                  pl.BlockSpec((1, vw), lambda h, i: (0, h))],
        out_specs=pl.BlockSpec((rows, vw), lambda h, i: (i, h)),
        scratch_shapes=[pltpu.VMEM((RET_DK, RET_DV), F32)] * hp,
        compiler_params=_cparams("parallel", "arbitrary"),
        name="retention",
    )(proj, proj, proj, proj, cos_t, sin_t, dmat, qdec, kdec, sdec, norm_g.reshape(1, RET_V))


def _gated_merge_proj(gl_ref, y0_ref, y1_ref, y2_ref, gu0_ref, gu1_ref, gu2_ref, gb0_ref, gb1_ref, gb2_ref,
                      wo0_ref, wo1_ref, wo2_ref):
    gl = gl_ref[...]
    acc = None
    for y_ref, gu_ref, gb_ref, wo_ref in ((y0_ref, gu0_ref, gb0_ref, wo0_ref),
                                          (y1_ref, gu1_ref, gb1_ref, wo1_ref),
                                          (y2_ref, gu2_ref, gb2_ref, wo2_ref)):
        gate = jax.nn.sigmoid(_dot(gl, gu_ref[...]) + gb_ref[...])
        term = gate * _dot(y_ref[...], wo_ref[...])
        acc = term if acc is None else acc + term
    return acc


def _merge_post(proj, ys, gate_w2, gate_b, w_out, layer, w_out_layer, x, g_post, g_next, bm=1024, bn=256):
    s, d = x.shape
    nm = s // bm
    gate_b4 = gate_b.reshape(DEPTH, N_BRANCH, 1, d)
    row = lambda i: jnp.minimum(i, nm - 1)
    y_spec = pl.BlockSpec((bm, BRANCH_W), lambda i, j: (row(i), 0))
    specs = [pl.BlockSpec((bm, GATE_RANK), lambda i, j: (row(i), P_GATE // GATE_RANK)), y_spec, y_spec, y_spec]
    specs += [pl.BlockSpec((None, GATE_RANK, bn),
                           functools.partial(lambda n, i, j: (layer * N_BRANCH + n, 0, j), n))
              for n in range(N_BRANCH)]
    specs += [pl.BlockSpec((None, None, 1, bn), functools.partial(lambda n, i, j: (layer, n, 0, j), n))
              for n in range(N_BRANCH)]
    specs += [pl.BlockSpec((None, BRANCH_W, bn),
                           functools.partial(lambda n, i, j: (w_out_layer * N_BRANCH + n, 0, j), n))
              for n in range(N_BRANCH)]
    args = (proj, ys[0], ys[1], ys[2], gate_w2, gate_w2, gate_w2, gate_b4, gate_b4, gate_b4, w_out, w_out, w_out)
    return _fused_proj_post(_gated_merge_proj, args, specs, x, g_post, 1.0, g_next, "merge_post", bm, bn)


def _xattn_kernel(x_ref, rs_ref, w_ref, k_ref, v_ref, o_ref, w_scr):
    @pl.when(pl.program_id(1) == 0)
    def _():
        w_scr[...] = w_ref[...].astype(BF16)

    q = _dot(x_ref[...], w_scr[...]) * rs_ref[:, 0:1]
    for h in range(q.shape[1] // XA_DIM):
        lo, hi = h * XA_DIM, (h + 1) * XA_DIM
        s = _dot_nt(q[:, lo:hi].astype(BF16), k_ref[:, lo:hi]) * (XA_DIM ** -0.5)
        p = jnp.exp(s - jnp.max(s, axis=-1, keepdims=True))
        p = p / jnp.sum(p, axis=-1, keepdims=True)
        o_ref[:, lo:hi] = _dot(p.astype(BF16), v_ref[:, lo:hi]).astype(o_ref.dtype)


def _xattn(act, w_q, layer, kv, bm=1024, bn=512):
    x, rs = act
    m, k = x.shape
    n_mem = kv.shape[0]
    nb = XA_W // bn
    return pl.pallas_call(
        _xattn_kernel,
        out_shape=jax.ShapeDtypeStruct((m, XA_W), BF16),
        grid=(nb, m // bm),
        in_specs=[pl.BlockSpec((bm, k), lambda j, i: (i, 0)),
                  pl.BlockSpec((bm, ROW_SCALE_W), lambda j, i: (i, 0)),
                  pl.BlockSpec((None, k, bn), lambda j, i: (layer, 0, j)),
                  pl.BlockSpec((n_mem, bn), lambda j, i: (0, j)),
                  pl.BlockSpec((n_mem, bn), lambda j, i: (0, nb + j))],
        out_specs=pl.BlockSpec((bm, bn), lambda j, i: (i, j)),
        scratch_shapes=[pltpu.VMEM((k, bn), BF16)],
        compiler_params=_cparams("arbitrary", "arbitrary"),
        name="xattn",
    )(x, rs, w_q, kv, kv)


def kernel(x, mem, positions, norm_g, mem_norm_g, ffn1_w_in, ffn1_w_out, mix_w_in, gla_w_gk, gla_b_gk,
           gla_norm_g, ssm_conv_w, ssm_conv_b, ssm_dt_bias, ssm_a_log, ssm_d, ssm_norm_g, ret_norm_g,
           gate_w_up, gate_b, mix_w_out, xattn_w_q, xattn_w_kv, xattn_w_o, ffn2_w_in, ffn2_w_out):
    b, s, d = x.shape
    assert b == 1 and d == D_MODEL and s % max(RET_BLOCK, MIX_ROWS, 1024) == 0
    nl = DEPTH
    xr = x.reshape(s, d)
    memr = mem.reshape(mem.shape[1], d)
    cos_t, sin_t = _rope_tables(positions)
    ret_tabs = _ret_tables(min(RET_BLOCK, s))
    expand = _ssd_expand_table()

    xa_out = _cast_bf16(xattn_w_o.reshape(nl * XA_W, d), 512).reshape(nl, XA_W, d)
    mix_w_out3 = mix_w_out.reshape(nl, N_BRANCH * BRANCH_W, d)
    gate_up = _cast_bf16(jnp.transpose(gate_w_up, (0, 2, 1, 3)).reshape(nl * N_BRANCH * GATE_RANK, d),
                         512).reshape(nl * N_BRANCH, GATE_RANK, d)
    mix_w_in_t = jnp.transpose(mix_w_in, (0, 2, 1))
    mix_in_big = _repack_mix_w_in(mix_w_in_t)
    mix_in_side_t = _side_mix_w_in(mix_w_in_t)
    w_gk_pad = jnp.concatenate([gla_w_gk, jnp.zeros((nl, SMALL_W - GLA_RANK, GLA_QK), F32)], axis=1)

    h = (_norm_cast(xr, norm_g[0, 0]), None)
    for l in range(nl):
        ng = norm_g[l]
        a, w_out, mix_out = _swiglu_in(h, ffn1_w_in, l, (ffn1_w_out, mix_w_out3))
        mix_out = mix_out.reshape(N_BRANCH, BRANCH_W, d)
        xr, h = _out_proj_post(a, w_out, 0, xr, ng[1], 0.5, ng[2], "ffn_out_post")
        proj, small = _mix_in(h, mix_in_big, mix_in_side_t, l)
        y_gla = _gla(proj, small, w_gk_pad[l], gla_b_gk[l], gla_norm_g[l])
        y_ssd = _ssd(proj, small, expand, ssm_conv_w[l], ssm_conv_b[l], ssm_dt_bias[l], ssm_a_log[l],
                     ssm_d[l], ssm_norm_g[l])
        y_ret = _retention(proj, cos_t, sin_t, ret_tabs, ret_norm_g[l])
        xr, h = _merge_post(proj, (y_gla, y_ssd, y_ret), gate_up, gate_b, mix_out, l, 0, xr, ng[3], ng[4])
        mem_n = _norm_cast(memr, mem_norm_g[l])
        kv = _matmul_wcast(mem_n, xattn_w_kv, l, BF16, 512, "xattn_kv")
        o = _xattn(h, xattn_w_q, l, kv)
        xr, h = _out_proj_post(o, xa_out, l, xr, ng[5], 1.0, ng[6], "xattn_out_post", bn=1024)
        a, w_out = _swiglu_in(h, ffn2_w_in, l, (ffn2_w_out,))
        g_next = norm_g[l + 1, 0] if l + 1 < nl else None
        xr, h = _out_proj_post(a, w_out, 0, xr, ng[7], 0.5, g_next, "ffn_out_post")
    return xr.reshape(b, s, d)
```
